```python
import math
import numpy as np
import jax
import jax.numpy as jnp
from jax import lax

D_MODEL = 2048
BATCH = 4
SEQ = 2048
DEPTH = 2
DEC_BATCH = 1
DEC_SEQ = 16384
PAST_LEN = 128

LRU_WIDTH = D_MODEL // 2
LRU_BLOCKS = 16
LRU_BLOCK_W = LRU_WIDTH // LRU_BLOCKS
LRU_CONV_W = 4
LRU_C = 8.0
S5_WIDTH = D_MODEL // 2
S5_GROUP_CH = 16
S5_GROUPS = S5_WIDTH // S5_GROUP_CH
S5_STATE = 64
HEAD_DIM = 64
ATT_GROUPS = ((128, 1), (512, 4), (2048, 16))
N_ATT_GROUPS = 3
ATT_WIDTH = 3 * D_MODEL // 4
ATT_HEADS = ATT_WIDTH // HEAD_DIM
HEADS_PER_GROUP = ATT_HEADS // N_ATT_GROUPS
ATT_OUT = HEADS_PER_GROUP * HEAD_DIM
Q_BLOCK = 128
REL_BUCKETS = 32
REL_MAX_DIST = 1024
D_FF = 11 * D_MODEL // 4
N_NORMS = 6
N_IN = 2 * LRU_WIDTH + S5_WIDTH + 3 * ATT_WIDTH + 3 * D_MODEL
RMS_EPS = 1e-6
NEG_INF = -1e30

kernel_name = 'hybrid_bidir_rglru_s5_dilated_attn_encoder'


def _t5_buckets(rel):
    half = REL_BUCKETS // 2
    max_exact = half // 2
    sign = (rel > 0).astype(np.int32) * half
    n = np.abs(rel)
    large = max_exact + (np.log(np.maximum(n, 1) / max_exact)
                         / np.log(REL_MAX_DIST / max_exact) * (half - max_exact)).astype(np.int32)
    large = np.minimum(large, half - 1)
    return sign + np.where(n < max_exact, n, large)


def _rmsnorm(x, g):
    x32 = x.astype(jnp.float32)
    y = x32 * lax.rsqrt(jnp.mean(x32 * x32, axis=-1, keepdims=True) + RMS_EPS)
    return (y * g.astype(jnp.float32)).astype(x.dtype)


def _swiglu(h, w1, w3, w2):
    return (jax.nn.silu(h @ w1) * (h @ w3)) @ w2


def _centred_dwconv(x, w, b):
    T = x.shape[1]
    left = LRU_CONV_W // 2
    xp = jnp.pad(x, ((0, 0), (left, LRU_CONV_W - 1 - left), (0, 0)))
    y = b
    for kk in range(LRU_CONV_W):
        y = y + xp[:, kk:kk + T] * w[kk]
    return y


def _linear_combine(e1, e2):
    a1, b1 = e1
    a2, b2 = e2
    return a1 * a2, a2 * b1 + b2


def _complex_combine(e1, e2):
    ar1, ai1, br1, bi1 = e1
    ar2, ai2, br2, bi2 = e2
    return (ar2 * ar1 - ai2 * ai1, ar2 * ai1 + ai2 * ar1,
            ar2 * br1 - ai2 * bi1 + br2, ar2 * bi1 + ai2 * br1 + bi2)


def _rglru_direction(xc, wa, ba, wx, bx, lam, reverse):
    B, T, W = xc.shape
    xb = xc.reshape(B, T, LRU_BLOCKS, LRU_BLOCK_W)
    r = jax.nn.sigmoid(jnp.einsum('btni,nij->btnj', xb, wa).reshape(B, T, W) + ba)
    i = jax.nn.sigmoid(jnp.einsum('btni,nij->btnj', xb, wx).reshape(B, T, W) + bx)
    log_a = -LRU_C * r * jax.nn.softplus(-lam.astype(jnp.float32))
    a = jnp.exp(log_a)
    b = jnp.sqrt(-jnp.expm1(2.0 * log_a)) * (i * xc)
    _, h = lax.associative_scan(_linear_combine, (a, b), axis=1, reverse=reverse)
    return h


def _s5_direction(u, lam_re, lam_im, log_dt, b_re, b_im, c_re, c_im, reverse):
    B, T, _ = u.shape
    lr = lam_re.astype(jnp.float32)
    li = lam_im.astype(jnp.float32)
    dt = jnp.exp(log_dt.astype(jnp.float32))[:, None]
    mag = jnp.exp(lr * dt)
    ar = mag * jnp.cos(li * dt)
    ai = mag * jnp.sin(li * dt)
    den = lr * lr + li * li
    cr = ((ar - 1.0) * lr + ai * li) / den
    ci = (ai * lr - (ar - 1.0) * li) / den
    bbr = cr[..., None] * b_re - ci[..., None] * b_im
    bbi = cr[..., None] * b_im + ci[..., None] * b_re
    ug = u.reshape(B, T, S5_GROUPS, S5_GROUP_CH)
    bur = jnp.einsum('btgc,gpc->btgp', ug, bbr)
    bui = jnp.einsum('btgc,gpc->btgp', ug, bbi)
    arb = jnp.broadcast_to(ar, bur.shape)
    aib = jnp.broadcast_to(ai, bur.shape)
    _, _, xr, xi = lax.associative_scan(_complex_combine, (arb, aib, bur, bui), axis=1, reverse=reverse)
    y = jnp.einsum('btgp,gcp->btgc', xr, c_re) - jnp.einsum('btgp,gcp->btgc', xi, c_im)
    return y.reshape(B, T, S5_WIDTH)


def _dilated_attention(q, k, v, rel_bias):
    B, T, H, Dh = q.shape
    n_blk = T // Q_BLOCK
    q = q * (HEAD_DIM ** -0.5)
    specs = []
    for g, (window, dil) in enumerate(ATT_GROUPS):
        n_side = (window // 2) // dil
        offs = dil * np.arange(-n_side, n_side + 1)
        hs = slice(g * HEADS_PER_GROUP, (g + 1) * HEADS_PER_GROUP)
        bias = rel_bias[_t5_buckets(offs)][:, hs].T.astype(jnp.float32)
        specs.append((offs, bias, q[:, :, hs], k[:, :, hs], v[:, :, hs]))

    def block(i):
        pos = i * Q_BLOCK + jnp.arange(Q_BLOCK)
        outs, lses = [], []
        for offs, bias, qg, kg, vg in specs:
            idx = pos[:, None] + offs[None, :]
            valid = (idx >= 0) & (idx < T)
            idx = jnp.clip(idx, 0, T - 1)
            qb = lax.dynamic_slice_in_dim(qg, i * Q_BLOCK, Q_BLOCK, axis=1)
            kb = jnp.take(kg, idx, axis=1)
            vb = jnp.take(vg, idx, axis=1)
            s = jnp.einsum('bqhd,bqjhd->bhqj', qb, kb).astype(jnp.float32) + bias[None, :, None, :]
            s = jnp.where(valid[None, None], s, NEG_INF)
            lse = jax.nn.logsumexp(s, axis=-1)
            p = jnp.exp(s - lse[..., None])
            outs.append(jnp.einsum('bhqj,bqjhd->bqhd', p, vb.astype(jnp.float32)))
            lses.append(lse)
        wts = jnp.transpose(jax.nn.softmax(jnp.stack(lses, axis=-1), axis=-1), (0, 2, 1, 3))
        o = jnp.sum(jnp.stack(outs, axis=-1) * wts[..., None, :], axis=-1)
        return o.reshape(B, Q_BLOCK, ATT_OUT).astype(q.dtype)

    y = lax.map(block, jnp.arange(n_blk))
    return jnp.transpose(y, (1, 0, 2, 3)).reshape(B, T, ATT_OUT)


def _mixer(h, w_in, conv_w, conv_b, lru_wa, lru_ba, lru_wx, lru_bx, lru_L,
           lam_re, lam_im, log_dt, b_re, b_im, c_re, c_im, s5_d, glu_w, glu_b,
           w_br_lru, w_br_s5, w_br_att, w_out, rel_bias):
    B, T, _ = h.shape
    z = h @ w_in
    cuts = np.cumsum([LRU_WIDTH, LRU_WIDTH, S5_WIDTH, ATT_WIDTH, ATT_WIDTH, ATT_WIDTH, D_MODEL, D_MODEL])
    xl, gl, u, q, k, v, ga, gb, gc = jnp.split(z, [int(c) for c in cuts], axis=-1)
    xc = _centred_dwconv(xl, conv_w, conv_b).astype(jnp.float32)
    y_lru = (_rglru_direction(xc, lru_wa[0], lru_ba[0], lru_wx[0], lru_bx[0], lru_L[0], False)
             + _rglru_direction(xc, lru_wa[1], lru_ba[1], lru_wx[1], lru_bx[1], lru_L[1], True))
    y_lru = y_lru.astype(h.dtype) * jax.nn.gelu(gl)
    u32 = u.astype(jnp.float32)
    y = (_s5_direction(u32, lam_re[0], lam_im[0], log_dt[0], b_re[0], b_im[0], c_re[0], c_im[0], False)
         + _s5_direction(u32, lam_re[1], lam_im[1], log_dt[1], b_re[1], b_im[1], c_re[1], c_im[1], True)
         + s5_d * u32)
    y1 = jax.nn.gelu(y)
    y_s5 = (y1 * jax.nn.sigmoid(y1 @ glu_w + glu_b)).astype(h.dtype)
    y_att = _dilated_attention(q.reshape(B, T, ATT_HEADS, HEAD_DIM), k.reshape(B, T, ATT_HEADS, HEAD_DIM),
                               v.reshape(B, T, ATT_HEADS, HEAD_DIM), rel_bias)
    m = (jax.nn.sigmoid(ga) * (y_lru @ w_br_lru)
         + jax.nn.sigmoid(gb) * (y_s5 @ w_br_s5)
         + jax.nn.sigmoid(gc) * (y_att @ w_br_att))
    return m @ w_out


def _trunk(x, params):
    (norm_g, w_in, lru_conv_w, lru_conv_b, lru_wa, lru_ba, lru_wx, lru_bx, lru_L,
     s5_lam_re, s5_lam_im, s5_log_dt, s5_b_re, s5_b_im, s5_c_re, s5_c_im, s5_d, s5_glu_w, s5_glu_b,
     rel_bias, w_br_lru, w_br_s5, w_br_att, w_out, ffn_w1, ffn_w3, ffn_w2) = params
    for l in range(DEPTH):
        g = norm_g[l]
        h = _rmsnorm(x, g[0])
        x = x + 0.5 * _rmsnorm(_swiglu(h, ffn_w1[l, 0], ffn_w3[l, 0], ffn_w2[l, 0]), g[1])
        h = _rmsnorm(x, g[2])
        mix = _mixer(h, w_in[l], lru_conv_w[l], lru_conv_b[l], lru_wa[l], lru_ba[l], lru_wx[l], lru_bx[l], lru_L[l],
                     s5_lam_re[l], s5_lam_im[l], s5_log_dt[l], s5_b_re[l], s5_b_im[l], s5_c_re[l], s5_c_im[l],
                     s5_d[l], s5_glu_w[l], s5_glu_b[l], w_br_lru[l], w_br_s5[l], w_br_att[l], w_out[l], rel_bias)
        x = x + _rmsnorm(mix, g[3])
        h = _rmsnorm(x, g[4])
        x = x + 0.5 * _rmsnorm(_swiglu(h, ffn_w1[l, 1], ffn_w3[l, 1], ffn_w2[l, 1]), g[5])
    return x


def setup_inputs(seed: int = 0) -> dict:
    key = jax.random.key(seed)
    ks = jax.random.split(key, 32)
    f32 = jnp.float32
    L, D = DEPTH, D_MODEL

    def nrm(k, shape, scale):
        return jax.random.normal(k, shape, f32) * scale

    x_prompt = nrm(ks[0], (BATCH, SEQ, D), 1.0)
    x_sample = nrm(ks[1], (DEC_BATCH, DEC_SEQ, D), 1.0)
    norm_g = 1.0 + nrm(ks[2], (L, N_NORMS, D), 0.02)
    w_in = nrm(ks[3], (L, D, N_IN), D ** -0.5)
    lru_conv_w = nrm(ks[4], (L, LRU_CONV_W, LRU_WIDTH), LRU_CONV_W ** -0.5)
    lru_conv_b = nrm(ks[5], (L, LRU_WIDTH), 0.01)
    lru_wa = nrm(ks[6], (L, 2, LRU_BLOCKS, LRU_BLOCK_W, LRU_BLOCK_W), LRU_BLOCK_W ** -0.5)
    lru_ba = nrm(ks[7], (L, 2, LRU_WIDTH), 0.01)
    lru_wx = nrm(ks[8], (L, 2, LRU_BLOCKS, LRU_BLOCK_W, LRU_BLOCK_W), LRU_BLOCK_W ** -0.5)
    lru_bx = nrm(ks[9], (L, 2, LRU_WIDTH), 0.01)
    a_init = jax.random.uniform(ks[10], (L, 2, LRU_WIDTH), f32, 0.9, 0.999) ** (1.0 / LRU_C)
    lru_L = jnp.log(a_init) - jnp.log1p(-a_init)
    n_idx = jnp.arange(S5_STATE, dtype=f32)
    s5_lam_re = -0.5 + nrm(ks[11], (L, 2, S5_GROUPS, S5_STATE), 0.02)
    s5_lam_im = jnp.pi * n_idx + nrm(ks[12], (L, 2, S5_GROUPS, S5_STATE), 0.02)
    s5_log_dt = jax.random.uniform(ks[13], (L, 2, S5_GROUPS), f32, math.log(1e-3), math.log(1e-1))
    s5_b_re = nrm(ks[14], (L, 2, S5_GROUPS, S5_STATE, S5_GROUP_CH), (2 * S5_GROUP_CH) ** -0.5)
    s5_b_im = nrm(ks[15], (L, 2, S5_GROUPS, S5_STATE, S5_GROUP_CH), (2 * S5_GROUP_CH) ** -0.5)
    s5_c_re = nrm(ks[16], (L, 2, S5_GROUPS, S5_GROUP_CH, S5_STATE), 0.5)
    s5_c_im = nrm(ks[17], (L, 2, S5_GROUPS, S5_GROUP_CH, S5_STATE), 0.5)
    s5_d = nrm(ks[18], (L, S5_WIDTH), 1.0)
    s5_glu_w = nrm(ks[19], (L, S5_WIDTH, S5_WIDTH), S5_WIDTH ** -0.5)
    s5_glu_b = nrm(ks[20], (L, S5_WIDTH), 0.01)
    rel_bias = nrm(ks[21], (REL_BUCKETS, ATT_HEADS), 0.5)
    w_br_lru = nrm(ks[22], (L, LRU_WIDTH, D), LRU_WIDTH ** -0.5)
    w_br_s5 = nrm(ks[23], (L, S5_WIDTH, D), S5_WIDTH ** -0.5)
    w_br_att = nrm(ks[24], (L, ATT_OUT, D), ATT_OUT ** -0.5)
    w_out = nrm(ks[25], (L, D, D), D ** -0.5)
    ffn_w1 = nrm(ks[26], (L, 2, D, D_FF), D ** -0.5)
    ffn_w3 = nrm(ks[27], (L, 2, D, D_FF), D ** -0.5)
    ffn_w2 = nrm(ks[28], (L, 2, D_FF, D), D_FF ** -0.5)
    return {'x_prompt': x_prompt, 'x_sample': x_sample, 'norm_g': norm_g, 'w_in': w_in,
            'lru_conv_w': lru_conv_w, 'lru_conv_b': lru_conv_b, 'lru_wa': lru_wa, 'lru_ba': lru_ba,
            'lru_wx': lru_wx, 'lru_bx': lru_bx, 'lru_L': lru_L,
            's5_lam_re': s5_lam_re, 's5_lam_im': s5_lam_im, 's5_log_dt': s5_log_dt,
            's5_b_re': s5_b_re, 's5_b_im': s5_b_im, 's5_c_re': s5_c_re, 's5_c_im': s5_c_im,
            's5_d': s5_d, 's5_glu_w': s5_glu_w, 's5_glu_b': s5_glu_b, 'rel_bias': rel_bias,
            'w_br_lru': w_br_lru, 'w_br_s5': w_br_s5, 'w_br_att': w_br_att, 'w_out': w_out,
            'ffn_w1': ffn_w1, 'ffn_w3': ffn_w3, 'ffn_w2': ffn_w2}


def reference(x_prompt, x_sample, norm_g, w_in, lru_conv_w, lru_conv_b, lru_wa, lru_ba, lru_wx, lru_bx, lru_L,
              s5_lam_re, s5_lam_im, s5_log_dt, s5_b_re, s5_b_im, s5_c_re, s5_c_im, s5_d, s5_glu_w, s5_glu_b,
              rel_bias, w_br_lru, w_br_s5, w_br_att, w_out, ffn_w1, ffn_w3, ffn_w2):
    params = (norm_g, w_in, lru_conv_w, lru_conv_b, lru_wa, lru_ba, lru_wx, lru_bx, lru_L,
              s5_lam_re, s5_lam_im, s5_log_dt, s5_b_re, s5_b_im, s5_c_re, s5_c_im, s5_d, s5_glu_w, s5_glu_b,
              rel_bias, w_br_lru, w_br_s5, w_br_att, w_out, ffn_w1, ffn_w3, ffn_w2)
    y_prompt = _trunk(x_prompt, params)
    y_sample = _trunk(x_sample, params)
    return (y_prompt, y_sample)
```

```python
import functools
import math

import numpy as np
import jax
import jax.numpy as jnp
from jax import lax
from jax.experimental import pallas as pl
from jax.experimental.pallas import tpu as pltpu

F32 = jnp.float32
BF16 = jnp.bfloat16

D_MODEL = 2048
LRU_WIDTH = 1024
LRU_BLOCKS = 16
LRU_CONV_W = 4
LRU_C = 8.0
S5_WIDTH = 1024
S5_GROUP_CH = 16
S5_GROUPS = 64
S5_STATE = 64
HEAD_DIM = 64
ATT_GROUPS = ((128, 1), (512, 4), (2048, 16))
ATT_WIDTH = 1536
ATT_HEADS = 24
HEADS_PER_GROUP = 8
ATT_OUT = 512
REL_BUCKETS = 32
REL_MAX_DIST = 1024
D_FF = 5632
RMS_EPS = 1e-6
NEG_INF = -1e30

Q_BLOCK = 128
HALF_WIN = 64
S5_CHUNK = 64
E_COLS = 3 * 1024 + 3 * D_MODEL
QKV_COLS = 3 * ATT_WIDTH
VMEM_LIMIT = 48 * 1024 * 1024


def _cparams(sem):
    return pltpu.CompilerParams(dimension_semantics=sem, vmem_limit_bytes=VMEM_LIMIT)


def _rms(v, g):
    return v * lax.rsqrt(jnp.mean(v * v, axis=-1, keepdims=True) + RMS_EPS) * g


def _gelu_tanh(v):
    return 0.5 * v * (1.0 + jnp.tanh(math.sqrt(2.0 / math.pi) * (v + 0.044715 * (v * v * v))))


def _sigmoid(v):
    return 1.0 / (1.0 + jnp.exp(-v))


def _any_eq(idx, values):
    hit = idx == values[0]
    for v in values[1:]:
        hit = jnp.logical_or(hit, idx == v)
    return hit


def _ffn_kernel(x_ref, gpre_ref, gpost_ref, w1_ref, w3_ref, w2_ref, o_ref, h_scr, acc_scr, *, nj):
    j = pl.program_id(1)

    @pl.when(j == 0)
    def _():
        h_scr[...] = _rms(x_ref[...], gpre_ref[...]).astype(BF16)
        acc_scr[...] = jnp.zeros_like(acc_scr)

    h = h_scr[...]
    a = jnp.dot(h, w1_ref[...], preferred_element_type=F32)
    b = jnp.dot(h, w3_ref[...], preferred_element_type=F32)
    g = (a * _sigmoid(a) * b).astype(BF16)
    acc_scr[...] += jnp.dot(g, w2_ref[...], preferred_element_type=F32)

    @pl.when(j == nj - 1)
    def _():
        o_ref[...] = x_ref[...] + 0.5 * _rms(acc_scr[...], gpost_ref[...])


def _ffn(x, g_pre, g_post, w1, w3, w2):
    n, d = x.shape
    dff = w1.shape[1]
    tm = min(512, n)
    tf = 512
    nj = dff // tf
    row = lambda i, j: (i, 0)
    const = lambda i, j: (0, 0)
    return pl.pallas_call(
        functools.partial(_ffn_kernel, nj=nj),
        grid=(n // tm, nj),
        in_specs=[pl.BlockSpec((tm, d), row), pl.BlockSpec((1, d), const), pl.BlockSpec((1, d), const),
                  pl.BlockSpec((d, tf), lambda i, j: (0, j)), pl.BlockSpec((d, tf), lambda i, j: (0, j)),
                  pl.BlockSpec((tf, d), lambda i, j: (j, 0))],
        out_specs=pl.BlockSpec((tm, d), row),
        out_shape=jax.ShapeDtypeStruct((n, d), F32),
        scratch_shapes=[pltpu.VMEM((tm, d), BF16), pltpu.VMEM((tm, d), F32)],
        compiler_params=_cparams(("parallel", "arbitrary")),
        name="ffn",
    )(x, g_pre.reshape(1, d), g_post.reshape(1, d), w1, w3, w2)


def _norm_mm_kernel(x_ref, g_ref, w_ref, o_ref, h_scr):
    @pl.when(pl.program_id(1) == 0)
    def _():
        h_scr[...] = _rms(x_ref[...], g_ref[...]).astype(BF16)

    o_ref[...] = jnp.dot(h_scr[...], w_ref[...], preferred_element_type=F32).astype(o_ref.dtype)


def _norm_mm(x, g, w, out_dtype):
    n, d = x.shape
    nout = w.shape[1]
    tm = min(512, n)
    tn = 1536
    return pl.pallas_call(
        _norm_mm_kernel,
        grid=(n // tm, nout // tn),
        in_specs=[pl.BlockSpec((tm, d), lambda i, j: (i, 0)), pl.BlockSpec((1, d), lambda i, j: (0, 0)),
                  pl.BlockSpec((d, tn), lambda i, j: (0, j))],
        out_specs=pl.BlockSpec((tm, tn), lambda i, j: (i, j)),
        out_shape=jax.ShapeDtypeStruct((n, nout), out_dtype),
        scratch_shapes=[pltpu.VMEM((tm, d), BF16)],
        compiler_params=_cparams(("parallel", "arbitrary")),
        name="mixer_in",
    )(x, g.reshape(1, d), w)


def _lru_kernel(xm_ref, xp_ref, xn_ref, cw_ref, cb_ref, wg_ref, bg_ref, lam_ref, o_ref,
                xpad_scr, a_scr, b_scr, h_scr, *, tc, nt, seg_first, seg_last):
    dirn = pl.program_id(0)
    i = pl.program_id(1)
    ti = i + dirn * (nt - 1 - 2 * i)
    w = LRU_WIDTH
    at_first = _any_eq(ti, seg_first)
    at_last = _any_eq(ti, seg_last)

    xpad_scr[pl.ds(0, 8), :] = xp_ref[...] * jnp.where(at_first, 0.0, 1.0)
    xpad_scr[pl.ds(8, tc), :] = xm_ref[...]
    xpad_scr[pl.ds(8 + tc, 8), :] = xn_ref[...] * jnp.where(at_last, 0.0, 1.0)
    xc = cb_ref[...] + xpad_scr[pl.ds(6, tc), :] * cw_ref[pl.ds(0, 1), :]
    for kk in range(1, LRU_CONV_W):
        xc = xc + xpad_scr[pl.ds(6 + kk, tc), :] * cw_ref[pl.ds(kk, 1), :]

    gates = jnp.dot(xc.astype(BF16), wg_ref[...], preferred_element_type=F32) + bg_ref[...]
    r = _sigmoid(gates[:, :w])
    ig = _sigmoid(gates[:, w:])
    nlam = -lam_ref[...]
    softplus = jnp.maximum(nlam, 0.0) + jnp.log(1.0 + jnp.exp(-jnp.abs(nlam)))
    log_a = -LRU_C * r * softplus
    a = jnp.exp(log_a)
    a_scr[...] = a
    b_scr[...] = jnp.sqrt(1.0 - a * a) * (ig * xc)

    reset = jnp.logical_or(jnp.logical_and(dirn == 0, at_first), jnp.logical_and(dirn == 1, at_last))

    @pl.when(reset)
    def _():
        h_scr[...] = jnp.zeros_like(h_scr)

    def body(t, h):
        row = t + dirn * (tc - 1 - 2 * t)
        h = a_scr[pl.ds(row, 1), :] * h + b_scr[pl.ds(row, 1), :]
        o_ref[pl.ds(row, 1), :] = h
        return h

    h_scr[pl.ds(0, 1), :] = lax.fori_loop(0, tc, body, h_scr[pl.ds(0, 1), :])


def _lru(ze, conv_w, conv_b, wg, bg, lam, seg_lens):
    n = ze.shape[0]
    w = LRU_WIDTH
    tc = 256
    nt = n // tc
    starts = np.cumsum((0,) + tuple(seg_lens))
    seg_first = tuple(int(s) // tc for s in starts[:-1])
    seg_last = tuple(int(s) // tc - 1 for s in starts[1:])
    hb = tc // 8
    nhb = n // 8

    def tile(d, i):
        return i + d * (nt - 1 - 2 * i)

    return pl.pallas_call(
        functools.partial(_lru_kernel, tc=tc, nt=nt, seg_first=seg_first, seg_last=seg_last),
        grid=(2, nt),
        in_specs=[pl.BlockSpec((tc, w), lambda d, i: (tile(d, i), 0)),
                  pl.BlockSpec((8, w), lambda d, i: (jnp.maximum(tile(d, i) * hb - 1, 0), 0)),
                  pl.BlockSpec((8, w), lambda d, i: (jnp.minimum((tile(d, i) + 1) * hb, nhb - 1), 0)),
                  pl.BlockSpec((LRU_CONV_W, w), lambda d, i: (0, 0)),
                  pl.BlockSpec((1, w), lambda d, i: (0, 0)),
                  pl.BlockSpec((None, w, 2 * w), lambda d, i: (d, 0, 0)),
                  pl.BlockSpec((None, 1, 2 * w), lambda d, i: (d, 0, 0)),
                  pl.BlockSpec((None, 1, w), lambda d, i: (d, 0, 0))],
        out_specs=pl.BlockSpec((None, tc, w), lambda d, i: (d, tile(d, i), 0)),
        out_shape=jax.ShapeDtypeStruct((2, n, w), F32),
        scratch_shapes=[pltpu.VMEM((tc + 16, w), F32), pltpu.VMEM((tc, w), F32), pltpu.VMEM((tc, w), F32),
                        pltpu.VMEM((8, w), F32)],
        compiler_params=_cparams(("arbitrary", "arbitrary")),
        name="rglru",
    )(ze, ze, ze, conv_w, conv_b.reshape(1, w), wg, bg, lam)


def _lru_gate_weights(wa, ba, wx, bx):
    def dense(wb):
        eye = jnp.eye(LRU_BLOCKS, dtype=wb.dtype)
        full = wb[:, :, :, None, :] * eye[None, :, None, :, None]
        return full.reshape(2, LRU_WIDTH, LRU_WIDTH)

    wg = jnp.concatenate([dense(wa), dense(wx)], axis=-1).astype(BF16)
    bg = jnp.concatenate([ba, bx], axis=-1).reshape(2, 1, 2 * LRU_WIDTH)
    return wg, bg


def _s5_ktable_kernel(c_ref, z_ref, o_ref):
    o_ref[...] = jnp.dot(c_ref[...], z_ref[...], preferred_element_type=F32, precision=lax.Precision.HIGHEST)


def _s5_prepare(lam_re, lam_im, log_dt, b_re, b_im, c_re, c_im):
    L, G, P, C = S5_CHUNK, S5_GROUPS, S5_STATE, S5_GROUP_CH
    dt = jnp.exp(log_dt)[..., None]
    mag = jnp.exp(lam_re * dt)
    ar = mag * jnp.cos(lam_im * dt)
    ai = mag * jnp.sin(lam_im * dt)
    den = lam_re * lam_re + lam_im * lam_im
    cr = ((ar - 1.0) * lam_re + ai * lam_im) / den
    ci = (ai * lam_re - (ar - 1.0) * lam_im) / den
    bbr = cr[..., None] * b_re - ci[..., None] * b_im
    bbi = cr[..., None] * b_im + ci[..., None] * b_re
    pr, pi = jnp.ones_like(ar)[None], jnp.zeros_like(ai)[None]
    nr, ni = ar, ai
    while pr.shape[0] < L + 1:
        pr, pi = (jnp.concatenate([pr, pr * nr - pi * ni], axis=0),
                  jnp.concatenate([pi, pr * ni + pi * nr], axis=0))
        nr, ni = nr * nr - ni * ni, 2.0 * nr * ni
    pr, pi = pr[:L + 1], pi[:L + 1]
    zr = pr[:L, ..., None] * bbr - pi[:L, ..., None] * bbi
    zi = pr[:L, ..., None] * bbi + pi[:L, ..., None] * bbr

    zmat = jnp.concatenate([zr, zi], axis=3)
    zmat = jnp.transpose(zmat, (1, 2, 3, 0, 4)).reshape(2 * G, 2 * P, L * C)
    cmat = jnp.concatenate([c_re, -c_im], axis=-1).reshape(2 * G, C, 2 * P)
    ktab = pl.pallas_call(
        _s5_ktable_kernel,
        grid=(2 * G,),
        in_specs=[pl.BlockSpec((None, C, 2 * P), lambda g: (g, 0, 0)),
                  pl.BlockSpec((None, 2 * P, L * C), lambda g: (g, 0, 0))],
        out_specs=pl.BlockSpec((None, C, L * C), lambda g: (g, 0, 0)),
        out_shape=jax.ShapeDtypeStruct((2 * G, C, L * C), F32),
        compiler_params=_cparams(("parallel",)),
        name="s5_ktable",
    )(cmat, zmat).reshape(2, G, C, L, C)
    kt = jnp.transpose(ktab, (0, 1, 3, 4, 2))
    kall = jnp.concatenate([kt[1, :, :0:-1], kt[0, :, :1] + kt[1, :, :1], kt[0, :, 1:]], axis=1)
    idx = np.arange(L)[None, :] - np.arange(L)[:, None] + (L - 1)
    mt = kall[:, idx]
    mt = jnp.transpose(mt, (0, 1, 3, 2, 4)).reshape(G, L * C, L * C).astype(BF16)

    def w_in(z, flip):
        zf = z[::-1] if flip else z
        return jnp.transpose(zf, (1, 0, 3, 2)).reshape(G, L * C, P)

    win = jnp.concatenate([w_in(zr[:, 0], True), w_in(zr[:, 1], False),
                           w_in(zi[:, 0], True), w_in(zi[:, 1], False)], axis=-1).astype(BF16)

    def w_out(dirn, powers_r, powers_i):
        ctr = jnp.transpose(c_re[dirn], (0, 2, 1))[:, :, None, :]
        cti = jnp.transpose(c_im[dirn], (0, 2, 1))[:, :, None, :]
        qr = jnp.transpose(powers_r, (1, 2, 0))[..., None]
        qi = jnp.transpose(powers_i, (1, 2, 0))[..., None]
        wr = ctr * qr - cti * qi
        wi = ctr * qi + cti * qr
        return wr.reshape(G, P, L * C), (-wi).reshape(G, P, L * C)

    fr, fi = w_out(0, pr[1:, 0], pi[1:, 0])
    br, bi = w_out(1, pr[:0:-1, 1], pi[:0:-1, 1])
    wout = jnp.concatenate([fr, br, fi, bi], axis=1).astype(BF16)
    al = jnp.concatenate([pr[L, 0], pr[L, 1], pi[L, 0], pi[L, 1]], axis=-1)
    return mt, win, wout, al


def _s5_state_kernel(v_ref, win_ref, o_ref):
    o_ref[...] = jnp.dot(v_ref[...], win_ref[...], preferred_element_type=F32)


def _s5_scan_kernel(s_ref, al_ref, o_ref, *, seg_chunks):
    p2 = 2 * S5_STATE
    alr = al_ref[:, :p2]
    ali = al_ref[:, p2:]
    is_fwd = lax.broadcasted_iota(jnp.int32, alr.shape, 1) < S5_STATE
    zero = jnp.zeros_like(alr)
    start = 0
    for n_chunks in seg_chunks:
        def fwd(k, carry, start=start):
            xr, xi = carry
            c = start + k
            o_ref[c, :, :p2] = xr
            o_ref[c, :, p2:] = xi
            sr = s_ref[c, :, :p2]
            si = s_ref[c, :, p2:]
            return alr * xr - ali * xi + sr, alr * xi + ali * xr + si

        lax.fori_loop(0, n_chunks, fwd, (zero, zero))

        def bwd(k, carry, start=start, n_chunks=n_chunks):
            xr, xi = carry
            c = start + n_chunks - 1 - k
            o_ref[c, :, :p2] = jnp.where(is_fwd, o_ref[c, :, :p2], xr)
            o_ref[c, :, p2:] = jnp.where(is_fwd, o_ref[c, :, p2:], xi)
            sr = s_ref[c, :, :p2]
            si = s_ref[c, :, p2:]
            return alr * xr - ali * xi + sr, alr * xi + ali * xr + si

        lax.fori_loop(0, n_chunks, bwd, (zero, zero))
        start += n_chunks


def _s5_out_kernel(v_ref, mt_ref, x_ref, wout_ref, o_ref):
    o_ref[...] = (jnp.dot(v_ref[...], mt_ref[...], preferred_element_type=F32)
                  + jnp.dot(x_ref[...].astype(BF16), wout_ref[...], preferred_element_type=F32))


def _s5(u, tables, seg_lens):
    mt, win, wout, al = tables
    L, G, P, C = S5_CHUNK, S5_GROUPS, S5_STATE, S5_GROUP_CH
    n = u.shape[0]
    nc = n // L
    lc = L * C
    v = jnp.transpose(u.reshape(nc, L, G, C), (2, 0, 1, 3)).reshape(G, nc, lc).astype(BF16)
    states = pl.pallas_call(
        _s5_state_kernel,
        grid=(G,),
        in_specs=[pl.BlockSpec((None, nc, lc), lambda g: (g, 0, 0)),
                  pl.BlockSpec((None, lc, 4 * P), lambda g: (g, 0, 0))],
        out_specs=pl.BlockSpec((nc, 4 * P), lambda g: (0, g)),
        out_shape=jax.ShapeDtypeStruct((nc, G * 4 * P), F32),
        compiler_params=_cparams(("parallel",)),
        name="s5_chunk_state",
    )(v, win)
    gb = 8
    carried = pl.pallas_call(
        functools.partial(_s5_scan_kernel, seg_chunks=tuple(t // L for t in seg_lens)),
        grid=(G // gb,),
        in_specs=[pl.BlockSpec((nc, gb, 4 * P), lambda g: (0, g, 0)),
                  pl.BlockSpec((gb, 4 * P), lambda g: (g, 0))],
        out_specs=pl.BlockSpec((nc, gb, 4 * P), lambda g: (0, g, 0)),
        out_shape=jax.ShapeDtypeStruct((nc, G, 4 * P), F32),
        compiler_params=_cparams(("parallel",)),
        name="s5_chunk_scan",
    )(states.reshape(nc, G, 4 * P), al)
    y = pl.pallas_call(
        _s5_out_kernel,
        grid=(G,),
        in_specs=[pl.BlockSpec((None, nc, lc), lambda g: (g, 0, 0)),
                  pl.BlockSpec((None, lc, lc), lambda g: (g, 0, 0)),
                  pl.BlockSpec((nc, 4 * P), lambda g: (0, g)),
                  pl.BlockSpec((None, 4 * P, lc), lambda g: (g, 0, 0))],
        out_specs=pl.BlockSpec((None, nc, lc), lambda g: (g, 0, 0)),
        out_shape=jax.ShapeDtypeStruct((G, nc, lc), F32),
        compiler_params=_cparams(("parallel",)),
        name="s5_chunk_out",
    )(v, mt, carried.reshape(nc, G * 4 * P), wout)
    return jnp.transpose(y.reshape(G, nc, L, C), (1, 2, 0, 3)).reshape(n, G * C)


def _s5_post_kernel(y_ref, u_ref, d_ref, w_ref, b_ref, o_ref):
    y1 = _gelu_tanh(y_ref[...] + d_ref[...] * u_ref[...])
    gate = jnp.dot(y1.astype(BF16), w_ref[...], preferred_element_type=F32) + b_ref[...]
    o_ref[...] = (y1 * _sigmoid(gate)).astype(o_ref.dtype)


def _s5_post(y5, ze, s5_d, glu_w, glu_b):
    n, w = y5.shape
    tm = min(512, n)
    return pl.pallas_call(
        _s5_post_kernel,
        grid=(n // tm,),
        in_specs=[pl.BlockSpec((tm, w), lambda i: (i, 0)), pl.BlockSpec((tm, w), lambda i: (i, 2)),
                  pl.BlockSpec((1, w), lambda i: (0, 0)), pl.BlockSpec((w, w), lambda i: (0, 0)),
                  pl.BlockSpec((1, w), lambda i: (0, 0))],
        out_specs=pl.BlockSpec((tm, w), lambda i: (i, 0)),
        out_shape=jax.ShapeDtypeStruct((n, w), BF16),
        compiler_params=_cparams(("parallel",)),
        name="s5_post",
    )(y5, ze, s5_d.reshape(1, w), glu_w, glu_b.reshape(1, w))


def _t5_buckets(rel):
    half = REL_BUCKETS // 2
    max_exact = half // 2
    sign = (rel > 0).astype(np.int32) * half
    n = np.abs(rel)
    large = max_exact + (np.log(np.maximum(n, 1) / max_exact)
                         / np.log(REL_MAX_DIST / max_exact) * (half - max_exact)).astype(np.int32)
    large = np.minimum(large, half - 1)
    return sign + np.where(n < max_exact, n, large)


def _att_bias_tile(rel_bias, group, dil):
    rel = np.arange(Q_BLOCK + 2 * HALF_WIN)[None, :] - HALF_WIN - np.arange(Q_BLOCK)[:, None]
    band = np.abs(rel) <= HALF_WIN
    buckets = _t5_buckets(np.where(band, rel, 0) * dil)
    hs = slice(group * HEADS_PER_GROUP, (group + 1) * HEADS_PER_GROUP)
    tile = jnp.transpose(rel_bias[:, hs][buckets], (2, 0, 1)).astype(F32)
    return jnp.where(band[None], tile, NEG_INF)


def _att_kernel(q_ref, kp_ref, km_ref, kn_ref, vp_ref, vm_ref, vn_ref, bias_ref, o_ref, lse_ref,
                *, blk_first, blk_last):
    nb = pl.program_id(1)
    at_first = _any_eq(nb, blk_first)
    at_last = _any_eq(nb, blk_last)
    ncol = Q_BLOCK + 2 * HALF_WIN
    col = lax.broadcasted_iota(jnp.int32, (1, ncol), 1)
    col_ok = jnp.logical_and(col >= jnp.where(at_first, HALF_WIN, 0),
                             col < jnp.where(at_last, HALF_WIN + Q_BLOCK, ncol))
    q = q_ref[...]
    k = jnp.concatenate([kp_ref[...], km_ref[...], kn_ref[...]], axis=0)
    v = jnp.concatenate([vp_ref[...], vm_ref[...], vn_ref[...]], axis=0)
    scale = HEAD_DIM ** -0.5
    outs, lses = [], []
    for h in range(HEADS_PER_GROUP):
        hs = slice(h * HEAD_DIM, (h + 1) * HEAD_DIM)
        s = lax.dot_general(q[:, hs], k[:, hs], (((1,), (1,)), ((), ())), preferred_element_type=F32)
        s = jnp.where(col_ok, s * scale + bias_ref[h], NEG_INF)
        m = jnp.max(s, axis=-1, keepdims=True)
        p = jnp.exp(s - m)
        l = jnp.sum(p, axis=-1, keepdims=True)
        o = jnp.dot(p.astype(BF16), v[:, hs], preferred_element_type=F32) / l
        outs.append(o)
        lses.append(jnp.broadcast_to(m + jnp.log(l), (Q_BLOCK, HEAD_DIM)))
    o_ref[...] = jnp.concatenate(outs, axis=-1)
    lse_ref[...] = jnp.concatenate(lses, axis=-1)


def _attention_group(qkv, bias_tile, group, dil, seg_lens):
    n = qkv.shape[0]
    nd = n // dil
    nblk = nd // Q_BLOCK
    nhalf = nd // HALF_WIN
    cb = QKV_COLS // ATT_OUT
    starts = np.cumsum((0,) + tuple(seg_lens)) // (dil * Q_BLOCK)
    blk_first = tuple(int(s) for s in starts[:-1])
    blk_last = tuple(int(s) - 1 for s in starts[1:])
    view = qkv.reshape(nd, dil * QKV_COLS)
    ncol = Q_BLOCK + 2 * HALF_WIN

    def main(off):
        return pl.BlockSpec((Q_BLOCK, ATT_OUT), lambda r, b: (b, r * cb + off))

    def prev(off):
        return pl.BlockSpec((HALF_WIN, ATT_OUT), lambda r, b: (jnp.maximum(2 * b - 1, 0), r * cb + off))

    def nxt(off):
        return pl.BlockSpec((HALF_WIN, ATT_OUT), lambda r, b: (jnp.minimum(2 * b + 2, nhalf - 1), r * cb + off))

    ko, vo = 3 + group, 6 + group
    out_spec = pl.BlockSpec((Q_BLOCK, ATT_OUT), lambda r, b: (b, r))
    o, lse = pl.pallas_call(
        functools.partial(_att_kernel, blk_first=blk_first, blk_last=blk_last),
        grid=(dil, nblk),
        in_specs=[main(group), prev(ko), main(ko), nxt(ko), prev(vo), main(vo), nxt(vo),
                  pl.BlockSpec((HEADS_PER_GROUP, Q_BLOCK, ncol), lambda r, b: (0, 0, 0))],
        out_specs=[out_spec, out_spec],
        out_shape=[jax.ShapeDtypeStruct((nd, dil * ATT_OUT), F32)] * 2,
        compiler_params=_cparams(("parallel", "parallel")),
        name=f"attention_d{dil}",
    )(view, view, view, view, view, view, view, bias_tile)
    return o.reshape(n, ATT_OUT), lse.reshape(n, ATT_OUT)


def _merge_kernel(hf_ref, hb_ref, gl_ref, ys_ref, o0_ref, o1_ref, o2_ref, l0_ref, l1_ref, l2_ref,
                  ga_ref, gb_ref, gc_ref, wl_ref, ws_ref, wa_ref, m_ref, ylru_scr, yatt_scr):
    @pl.when(pl.program_id(1) == 0)
    def _():
        ylru_scr[...] = ((hf_ref[...] + hb_ref[...]) * _gelu_tanh(gl_ref[...])).astype(BF16)
        l0, l1, l2 = l0_ref[...], l1_ref[...], l2_ref[...]
        mx = jnp.maximum(jnp.maximum(l0, l1), l2)
        e0, e1, e2 = jnp.exp(l0 - mx), jnp.exp(l1 - mx), jnp.exp(l2 - mx)
        att = (o0_ref[...] * e0 + o1_ref[...] * e1 + o2_ref[...] * e2) / (e0 + e1 + e2)
        yatt_scr[...] = att.astype(BF16)

    m = (_sigmoid(ga_ref[...]) * jnp.dot(ylru_scr[...], wl_ref[...], preferred_element_type=F32)
         + _sigmoid(gb_ref[...]) * jnp.dot(ys_ref[...], ws_ref[...], preferred_element_type=F32)
         + _sigmoid(gc_ref[...]) * jnp.dot(yatt_scr[...], wa_ref[...], preferred_element_type=F32))
    m_ref[...] = m.astype(m_ref.dtype)


def _merge(hs, ze, ys5, att, w_br_lru, w_br_s5, w_br_att):
    n = ze.shape[0]
    d = D_MODEL
    tm = min(512, n)
    tn = 512
    ncb = d // tn
    wl, wa = LRU_WIDTH, ATT_OUT
    row = lambda i, j: (i, 0)
    (o0, l0), (o1, l1), (o2, l2) = att
    att_spec = pl.BlockSpec((tm, wa), row)
    return pl.pallas_call(
        _merge_kernel,
        grid=(n // tm, ncb),
        in_specs=[pl.BlockSpec((None, tm, wl), lambda i, j: (0, i, 0)),
                  pl.BlockSpec((None, tm, wl), lambda i, j: (1, i, 0)),
                  pl.BlockSpec((tm, wl), lambda i, j: (i, 1)),
                  pl.BlockSpec((tm, S5_WIDTH), row),
                  att_spec, att_spec, att_spec, att_spec, att_spec, att_spec,
                  pl.BlockSpec((tm, tn), lambda i, j: (i, 3 * 1024 // tn + j)),
                  pl.BlockSpec((tm, tn), lambda i, j: (i, 3 * 1024 // tn + ncb + j)),
                  pl.BlockSpec((tm, tn), lambda i, j: (i, 3 * 1024 // tn + 2 * ncb + j)),
                  pl.BlockSpec((wl, tn), lambda i, j: (0, j)),
                  pl.BlockSpec((S5_WIDTH, tn), lambda i, j: (0, j)),
                  pl.BlockSpec((wa, tn), lambda i, j: (0, j))],
        out_specs=pl.BlockSpec((tm, tn), lambda i, j: (i, j)),
        out_shape=jax.ShapeDtypeStruct((n, d), BF16),
        scratch_shapes=[pltpu.VMEM((tm, wl), BF16), pltpu.VMEM((tm, wa), BF16)],
        compiler_params=_cparams(("parallel", "arbitrary")),
        name="merge",
    )(hs, hs, ze, ys5, o0, o1, o2, l0, l1, l2, ze, ze, ze, w_br_lru, w_br_s5, w_br_att)


def _out_proj_kernel(x_ref, m_ref, w_ref, g_ref, o_ref):
    mix = jnp.dot(m_ref[...], w_ref[...], preferred_element_type=F32)
    o_ref[...] = x_ref[...] + _rms(mix, g_ref[...])


def _out_proj(x, m, w_out, g):
    n, d = x.shape
    tm = min(512, n)
    return pl.pallas_call(
        _out_proj_kernel,
        grid=(n // tm,),
        in_specs=[pl.BlockSpec((tm, d), lambda i: (i, 0)), pl.BlockSpec((tm, d), lambda i: (i, 0)),
                  pl.BlockSpec((d, d), lambda i: (0, 0)), pl.BlockSpec((1, d), lambda i: (0, 0))],
        out_specs=pl.BlockSpec((tm, d), lambda i: (i, 0)),
        out_shape=jax.ShapeDtypeStruct((n, d), F32),
        compiler_params=_cparams(("parallel",)),
        name="mixer_out",
    )(x, m, w_out, g.reshape(1, d))


def _mixer(x, g_pre, g_post, seg_lens, w_in, conv_w, conv_b, lru_wa, lru_ba, lru_wx, lru_bx, lru_L,
           lam_re, lam_im, log_dt, b_re, b_im, c_re, c_im, s5_d, glu_w, glu_b,
           w_br_lru, w_br_s5, w_br_att, w_out, bias_tiles):
    w_e = jnp.concatenate([w_in[:, :3 * 1024], w_in[:, 3 * 1024 + QKV_COLS:]], axis=1).astype(BF16)
    w_qkv = w_in[:, 3 * 1024:3 * 1024 + QKV_COLS].astype(BF16)
    ze = _norm_mm(x, g_pre, w_e, F32)
    qkv = _norm_mm(x, g_pre, w_qkv, BF16)

    wg, bg = _lru_gate_weights(lru_wa, lru_ba, lru_wx, lru_bx)
    hs = _lru(ze, conv_w, conv_b, wg, bg, lru_L.reshape(2, 1, LRU_WIDTH), seg_lens)

    tables = _s5_prepare(lam_re, lam_im, log_dt, b_re, b_im, c_re, c_im)
    y5 = _s5(ze[:, 2 * 1024:3 * 1024], tables, seg_lens)
    ys5 = _s5_post(y5, ze, s5_d, glu_w.astype(BF16), glu_b)

    att = [_attention_group(qkv, bias_tiles[g], g, dil, seg_lens) for g, (_, dil) in enumerate(ATT_GROUPS)]

    m = _merge(hs, ze, ys5, att, w_br_lru.astype(BF16), w_br_s5.astype(BF16), w_br_att.astype(BF16))
    return _out_proj(x, m, w_out.astype(BF16), g_post)


def _trunk(x, seg_lens, norm_g, w_in, lru_conv_w, lru_conv_b, lru_wa, lru_ba, lru_wx, lru_bx, lru_L,
           s5_lam_re, s5_lam_im, s5_log_dt, s5_b_re, s5_b_im, s5_c_re, s5_c_im, s5_d, s5_glu_w, s5_glu_b,
           rel_bias, w_br_lru, w_br_s5, w_br_att, w_out, ffn_w1, ffn_w3, ffn_w2):
    bias_tiles = [_att_bias_tile(rel_bias, g, dil) for g, (_, dil) in enumerate(ATT_GROUPS)]
    for l in range(norm_g.shape[0]):
        g = norm_g[l]
        x = _ffn(x, g[0], g[1], ffn_w1[l, 0].astype(BF16), ffn_w3[l, 0].astype(BF16), ffn_w2[l, 0].astype(BF16))
        x = _mixer(x, g[2], g[3], seg_lens, w_in[l], lru_conv_w[l], lru_conv_b[l], lru_wa[l], lru_ba[l],
                   lru_wx[l], lru_bx[l], lru_L[l], s5_lam_re[l], s5_lam_im[l], s5_log_dt[l], s5_b_re[l],
                   s5_b_im[l], s5_c_re[l], s5_c_im[l], s5_d[l], s5_glu_w[l], s5_glu_b[l],
                   w_br_lru[l], w_br_s5[l], w_br_att[l], w_out[l], bias_tiles)
        x = _ffn(x, g[4], g[5], ffn_w1[l, 1].astype(BF16), ffn_w3[l, 1].astype(BF16), ffn_w2[l, 1].astype(BF16))
    return x


def kernel(x_prompt, x_sample, norm_g, w_in, lru_conv_w, lru_conv_b, lru_wa, lru_ba, lru_wx, lru_bx, lru_L,
           s5_lam_re, s5_lam_im, s5_log_dt, s5_b_re, s5_b_im, s5_c_re, s5_c_im, s5_d, s5_glu_w, s5_glu_b,
           rel_bias, w_br_lru, w_br_s5, w_br_att, w_out, ffn_w1, ffn_w3, ffn_w2):
    bp, tp, d = x_prompt.shape
    bs, ts, _ = x_sample.shape
    seg_lens = (tp,) * bp + (ts,) * bs
    x = jnp.concatenate([x_prompt.reshape(bp * tp, d), x_sample.reshape(bs * ts, d)], axis=0)
    y = _trunk(x, seg_lens, norm_g, w_in, lru_conv_w, lru_conv_b, lru_wa, lru_ba, lru_wx, lru_bx, lru_L,
               s5_lam_re, s5_lam_im, s5_log_dt, s5_b_re, s5_b_im, s5_c_re, s5_c_im, s5_d, s5_glu_w, s5_glu_b,
               rel_bias, w_br_lru, w_br_s5, w_br_att, w_out, ffn_w1, ffn_w3, ffn_w2)
    return (y[:bp * tp].reshape(bp, tp, d), y[bp * tp:].reshape(bs, ts, d))
```

```python
import functools
import math

import numpy as np
import jax
import jax.numpy as jnp
from jax import lax
from jax.experimental import pallas as pl
from jax.experimental.pallas import tpu as pltpu

F32 = jnp.float32
BF16 = jnp.bfloat16

D_MODEL = 2048
LRU_WIDTH = 1024
LRU_BLOCKS = 16
LRU_CONV_W = 4
LRU_C = 8.0
S5_WIDTH = 1024
S5_GROUP_CH = 16
S5_GROUPS = 64
S5_STATE = 64
HEAD_DIM = 64
ATT_GROUPS = ((128, 1), (512, 4), (2048, 16))
ATT_WIDTH = 1536
HEADS_PER_GROUP = 8
ATT_OUT = 512
REL_BUCKETS = 32
REL_MAX_DIST = 1024
RMS_EPS = 1e-6
NEG_INF = -1e30

Q_BLOCK = 128
HALF_WIN = 64
S5_CHUNK = 64
LANES = 128
SUBLANES = 8
COL_QKV = 3 * 1024
E_TILE = 1536
E_XL, E_GL, E_U = 6, 7, 8
VMEM_LIMIT = 48 * 1024 * 1024


def _cparams(sem, vmem_limit=VMEM_LIMIT):
    return pltpu.CompilerParams(dimension_semantics=sem, vmem_limit_bytes=vmem_limit)


def _rms(v, g):
    return v * lax.rsqrt(jnp.mean(v * v, axis=-1, keepdims=True) + RMS_EPS) * g


def _gelu_tanh(v):
    return 0.5 * v * (1.0 + jnp.tanh(math.sqrt(2.0 / math.pi) * (v + 0.044715 * (v * v * v))))


def _sigmoid(v):
    return 1.0 / (1.0 + jnp.exp(-v))


def _any_eq(idx, values):
    hit = idx == values[0]
    for v in values[1:]:
        hit = jnp.logical_or(hit, idx == v)
    return hit


def _resident(shape, index_map):
    return pl.BlockSpec(shape, index_map, pipeline_mode=pl.Buffered(1))


def _ffn_kernel(*refs, nj, in_tiles, out_tiles):
    n_in, n_out = len(in_tiles), len(out_tiles)
    x_refs = refs[:n_in]
    gpre_ref, gpost_ref, w1_ref, w3_ref, w2_ref = refs[n_in:n_in + 5]
    o_refs = refs[n_in + 5:n_in + 5 + n_out]
    h_scr, acc_scr = refs[n_in + 5 + n_out:]
    i = pl.program_id(0)
    j = pl.program_id(1)

    def active(bounds, k):
        lo = sum(bounds[:k])
        return jnp.logical_and(i >= lo, i < lo + bounds[k])

    for k in range(n_in):
        @pl.when(jnp.logical_and(j == 0, active(in_tiles, k)))
        def _(k=k):
            h_scr[...] = _rms(x_refs[k][...], gpre_ref[...]).astype(BF16)
            acc_scr[...] = jnp.zeros_like(acc_scr)

    h = h_scr[...]
    a = jnp.dot(h, w1_ref[...], preferred_element_type=F32)
    b = jnp.dot(h, w3_ref[...], preferred_element_type=F32)
    g = (a * _sigmoid(a) * b).astype(BF16)
    acc_scr[...] += jnp.dot(g, w2_ref[...], preferred_element_type=F32)

    for ki in range(n_in):
        for ko in range(n_out):
            @pl.when(jnp.logical_and(j == nj - 1, jnp.logical_and(active(in_tiles, ki), active(out_tiles, ko))))
            def _(ki=ki, ko=ko):
                o_refs[ko][...] = x_refs[ki][...] + 0.5 * _rms(acc_scr[...], gpost_ref[...])


def _ffn(xs, g_pre, g_post, w1, w3, w2, layer, which, out_rows):
    d = xs[0].shape[1]
    dff = w1.shape[-1]
    tm, tf = 512, 512
    nj = dff // tf
    in_tiles = tuple(x.shape[0] // tm for x in xs)
    out_tiles = tuple(r // tm for r in out_rows)

    def piece(bounds, k):
        lo = sum(bounds[:k])
        return lambda i, j: (jnp.clip(i - lo, 0, bounds[k] - 1), 0)

    const = lambda i, j: (0, 0)
    wcol = pl.BlockSpec((None, None, d, tf), lambda i, j: (layer, which, 0, j))
    outs = pl.pallas_call(
        functools.partial(_ffn_kernel, nj=nj, in_tiles=in_tiles, out_tiles=out_tiles),
        grid=(sum(in_tiles), nj),
        in_specs=[pl.BlockSpec((tm, d), piece(in_tiles, k)) for k in range(len(xs))]
        + [pl.BlockSpec((1, d), const), pl.BlockSpec((1, d), const), wcol, wcol,
           pl.BlockSpec((None, None, tf, d), lambda i, j: (layer, which, j, 0))],
        out_specs=[pl.BlockSpec((tm, d), piece(out_tiles, k)) for k in range(len(out_rows))],
        out_shape=[jax.ShapeDtypeStruct((r, d), F32) for r in out_rows],
        scratch_shapes=[pltpu.VMEM((tm, d), BF16), pltpu.VMEM((tm, d), F32)],
        compiler_params=_cparams(("parallel", "arbitrary"),
                                 VMEM_LIMIT + (len(xs) + len(out_rows) - 2) * 2 * tm * d * 4),
        name="ffn",
    )(*xs, g_pre.reshape(1, d), g_post.reshape(1, d), w1, w3, w2)
    return list(outs)


def _norm_mm_kernel(x_ref, g_ref, w_ref, o_ref, h_scr):
    @pl.when(pl.program_id(1) == 0)
    def _():
        h_scr[...] = _rms(x_ref[...], g_ref[...]).astype(BF16)

    o_ref[...] = jnp.dot(h_scr[...], w_ref[...], preferred_element_type=F32).astype(o_ref.dtype)


def _proj_elementwise(x, g, w_in, layer):
    n, d = x.shape
    tm, tn = 512, E_TILE
    lead = COL_QKV // tn
    skip = 3 * ATT_WIDTH // tn
    ncol = (w_in.shape[-1] - 3 * ATT_WIDTH) // tn
    ngate = ncol - lead
    return pl.pallas_call(
        _norm_mm_kernel,
        grid=(n // tm, ncol),
        in_specs=[pl.BlockSpec((tm, d), lambda i, j: (i, 0)), pl.BlockSpec((1, d), lambda i, j: (0, 0)),
                  pl.BlockSpec((None, d, tn),
                               lambda i, j: (layer, 0, jnp.where(j < ngate, j + lead + skip, j - ngate)))],
        out_specs=pl.BlockSpec((tm, tn), lambda i, j: (i, j)),
        out_shape=jax.ShapeDtypeStruct((n, ncol * tn), F32),
        scratch_shapes=[pltpu.VMEM((tm, d), BF16)],
        compiler_params=_cparams(("parallel", "arbitrary")),
        name="mixer_in",
    )(x, g.reshape(1, d), w_in)


def _qkv_kernel(x_ref, g_ref, wq_ref, wk_ref, wv_ref, o0_ref, o1_ref, o2_ref, h_scr, res_scr, *, tm):
    grp = pl.program_id(1)
    o_refs = (o0_ref, o1_ref, o2_ref)

    @pl.when(grp == 0)
    def _():
        h_scr[...] = _rms(x_ref[...], g_ref[...]).astype(BF16)

    h = h_scr[...]
    q = jnp.dot(h, wq_ref[...], preferred_element_type=F32) * (HEAD_DIM ** -0.5)
    k = jnp.dot(h, wk_ref[...], preferred_element_type=F32)
    v = jnp.dot(h, wv_ref[...], preferred_element_type=F32)
    res = jnp.concatenate([q, k, v], axis=-1)
    ntile = res.shape[-1] // LANES
    for gi, (_, dil) in enumerate(ATT_GROUPS):
        @pl.when(grp == gi)
        def _(gi=gi, dil=dil):
            if dil == 1:
                o_refs[gi][0] = res.astype(BF16)
                return
            rows = tm // dil
            for c in range(ntile):
                res_scr[c] = res[:, c * LANES:(c + 1) * LANES]
            for r in range(dil):
                o_refs[gi][r] = jnp.concatenate(
                    [res_scr[c, pl.ds(r, rows, stride=dil), :] for c in range(ntile)], axis=-1).astype(BF16)


def _proj_qkv(x, g, w_in, layer):
    n, d = x.shape
    tm = 512
    qb = COL_QKV // ATT_OUT

    def wspec(off):
        return pl.BlockSpec((None, d, ATT_OUT), lambda i, gq: (layer, 0, qb + off + gq))

    ng = len(ATT_GROUPS)
    return pl.pallas_call(
        functools.partial(_qkv_kernel, tm=tm),
        grid=(n // tm, ng),
        in_specs=[pl.BlockSpec((tm, d), lambda i, gq: (i, 0)), pl.BlockSpec((1, d), lambda i, gq: (0, 0)),
                  wspec(0), wspec(ng), wspec(2 * ng)],
        out_specs=[pl.BlockSpec((dil, tm // dil, 3 * ATT_OUT), lambda i, gq: (0, i, 0)) for _, dil in ATT_GROUPS],
        out_shape=[jax.ShapeDtypeStruct((dil, n // dil, 3 * ATT_OUT), BF16) for _, dil in ATT_GROUPS],
        scratch_shapes=[pltpu.VMEM((tm, d), BF16), pltpu.VMEM((3 * ATT_OUT // LANES, tm, LANES), F32)],
        compiler_params=_cparams(("parallel", "arbitrary")),
        name="mixer_qkv",
    )(x, g.reshape(1, d), w_in, w_in, w_in)


def _lru_kernel(xm_ref, xp_ref, xn_ref, cw_ref, cb_ref, wg_ref, bg_ref, lam_ref, o_ref,
                xpad_scr, a_scr, b_scr, hl_scr, p_scr, h_scr, *, tc, nt, seg_first, seg_last):
    dirn = pl.program_id(0)
    i = pl.program_id(1)
    ti = i + dirn * (nt - 1 - 2 * i)
    w = LRU_WIDTH
    at_first = _any_eq(ti, seg_first)
    at_last = _any_eq(ti, seg_last)

    xpad_scr[pl.ds(0, SUBLANES), :] = xp_ref[...] * jnp.where(at_first, 0.0, 1.0)
    xpad_scr[pl.ds(SUBLANES, tc), :] = xm_ref[...]
    xpad_scr[pl.ds(SUBLANES + tc, SUBLANES), :] = xn_ref[...] * jnp.where(at_last, 0.0, 1.0)
    left = LRU_CONV_W // 2
    xc = cb_ref[...] + xpad_scr[pl.ds(SUBLANES - left, tc), :] * cw_ref[pl.ds(0, 1), :]
    for kk in range(1, LRU_CONV_W):
        xc = xc + xpad_scr[pl.ds(SUBLANES - left + kk, tc), :] * cw_ref[pl.ds(kk, 1), :]

    gates = jnp.dot(xc.astype(BF16), wg_ref[...], preferred_element_type=F32) + bg_ref[...]
    r = _sigmoid(gates[:, :w])
    ig = _sigmoid(gates[:, w:])
    nlam = -lam_ref[...]
    softplus = jnp.maximum(nlam, 0.0) + jnp.log(1.0 + jnp.exp(-jnp.abs(nlam)))
    a = jnp.exp(-LRU_C * r * softplus)
    bb = jnp.sqrt(1.0 - a * a) * (ig * xc)

    sub = tc // SUBLANES
    pitch = sub + SUBLANES
    ntile = w // LANES
    for c in range(ntile):
        for jj in range(SUBLANES):
            a_scr[c, pl.ds(jj * pitch, sub), :] = a[jj * sub:(jj + 1) * sub, c * LANES:(c + 1) * LANES]
            b_scr[c, pl.ds(jj * pitch, sub), :] = bb[jj * sub:(jj + 1) * sub, c * LANES:(c + 1) * LANES]

    def strided_rows(ref, k):
        return jnp.concatenate([ref[c, pl.ds(k, SUBLANES, stride=pitch), :] for c in range(ntile)], axis=-1)

    reset = jnp.logical_or(jnp.logical_and(dirn == 0, at_first), jnp.logical_and(dirn == 1, at_last))

    @pl.when(reset)
    def _():
        h_scr[...] = jnp.zeros_like(h_scr)

    sub_id = lax.broadcasted_iota(jnp.int32, (SUBLANES, w), 0)
    for d in (0, 1):
        @pl.when(dirn == d)
        def _(d=d):
            steps = range(sub) if d == 0 else range(sub - 1, -1, -1)
            hl = jnp.zeros((SUBLANES, w), F32)
            pp = jnp.ones((SUBLANES, w), F32)
            for k in steps:
                av = strided_rows(a_scr, k)
                bv = strided_rows(b_scr, k)
                hl = av * hl + bv
                pp = av * pp
                hl_scr[pl.ds(k * SUBLANES, SUBLANES), :] = hl
                p_scr[pl.ds(k * SUBLANES, SUBLANES), :] = pp
            cur = h_scr[pl.ds(0, 1), :]
            carry = jnp.zeros((SUBLANES, w), F32)
            for jj in (range(SUBLANES) if d == 0 else range(SUBLANES - 1, -1, -1)):
                carry = jnp.where(sub_id == jj, cur, carry)
                cur = hl[jj:jj + 1, :] + pp[jj:jj + 1, :] * cur
            h_scr[pl.ds(0, 1), :] = cur
            for k in range(sub):
                rows = pl.ds(k * SUBLANES, SUBLANES)
                hv = hl_scr[rows, :] + p_scr[rows, :] * carry
                for c in range(ntile):
                    a_scr[c, pl.ds(k, SUBLANES, stride=pitch), :] = hv[:, c * LANES:(c + 1) * LANES]
            for jj in range(SUBLANES):
                o_ref[pl.ds(jj * sub, sub), :] = jnp.concatenate(
                    [a_scr[c, pl.ds(jj * pitch, sub), :] for c in range(ntile)], axis=-1)


def _lru(ze, conv_w, conv_b, wg, bg, lam, seg_lens):
    n = ze.shape[0]
    w = LRU_WIDTH
    tc = 256
    nt = n // tc
    starts = np.cumsum((0,) + tuple(seg_lens))
    seg_first = tuple(int(s) // tc for s in starts[:-1])
    seg_last = tuple(int(s) // tc - 1 for s in starts[1:])
    hb = tc // SUBLANES
    nhb = n // SUBLANES
    pitched = SUBLANES * (tc // SUBLANES + SUBLANES)

    def tile(d, i):
        return i + d * (nt - 1 - 2 * i)

    return pl.pallas_call(
        functools.partial(_lru_kernel, tc=tc, nt=nt, seg_first=seg_first, seg_last=seg_last),
        grid=(2, nt),
        in_specs=[pl.BlockSpec((tc, w), lambda d, i: (tile(d, i), E_XL)),
                  pl.BlockSpec((SUBLANES, w), lambda d, i: (jnp.maximum(tile(d, i) * hb - 1, 0), E_XL)),
                  pl.BlockSpec((SUBLANES, w), lambda d, i: (jnp.minimum((tile(d, i) + 1) * hb, nhb - 1), E_XL)),
                  pl.BlockSpec((LRU_CONV_W, w), lambda d, i: (0, 0)),
                  pl.BlockSpec((1, w), lambda d, i: (0, 0)),
                  pl.BlockSpec((None, w, 2 * w), lambda d, i: (d, 0, 0)),
                  pl.BlockSpec((None, 1, 2 * w), lambda d, i: (d, 0, 0)),
                  pl.BlockSpec((None, 1, w), lambda d, i: (d, 0, 0))],
        out_specs=pl.BlockSpec((None, tc, w), lambda d, i: (d, tile(d, i), 0)),
        out_shape=jax.ShapeDtypeStruct((2, n, w), F32),
        scratch_shapes=[pltpu.VMEM((tc + 2 * SUBLANES, w), F32), pltpu.VMEM((w // LANES, pitched, LANES), F32),
                        pltpu.VMEM((w // LANES, pitched, LANES), F32), pltpu.VMEM((tc, w), F32),
                        pltpu.VMEM((tc, w), F32),
                        pltpu.VMEM((SUBLANES, w), F32)],
        compiler_params=_cparams(("arbitrary", "arbitrary")),
        name="rglru",
    )(ze, ze, ze, conv_w, conv_b.reshape(1, w), wg, bg, lam)


def _lru_gate_weights(wa, ba, wx, bx):
    def dense(wb):
        eye = jnp.eye(LRU_BLOCKS, dtype=wb.dtype)
        full = wb[:, :, :, None, :] * eye[None, :, None, :, None]
        return full.reshape(2, LRU_WIDTH, LRU_WIDTH)

    wg = jnp.concatenate([dense(wa), dense(wx)], axis=-1).astype(BF16)
    bg = jnp.concatenate([ba, bx], axis=-1).reshape(2, 1, 2 * LRU_WIDTH)
    return wg, bg


def _s5_ktable_kernel(b_ref, w_ref, o_ref):
    o_ref[...] = jnp.dot(b_ref[...], w_ref[...], preferred_element_type=F32, precision=lax.Precision.HIGHEST)


def _s5_prepare(lam_re, lam_im, log_dt, b_re, b_im, c_re, c_im):
    L, G, P, C = S5_CHUNK, S5_GROUPS, S5_STATE, S5_GROUP_CH
    dt = jnp.exp(log_dt)[..., None]
    mag = jnp.exp(lam_re * dt)
    ar = mag * jnp.cos(lam_im * dt)
    ai = mag * jnp.sin(lam_im * dt)
    den = lam_re * lam_re + lam_im * lam_im
    cr = ((ar - 1.0) * lam_re + ai * lam_im) / den
    ci = (ai * lam_re - (ar - 1.0) * lam_im) / den
    bbr = cr[..., None] * b_re - ci[..., None] * b_im
    bbi = cr[..., None] * b_im + ci[..., None] * b_re
    pr, pi = jnp.ones_like(ar)[None], jnp.zeros_like(ai)[None]
    nr, ni = ar, ai
    while pr.shape[0] < L + 1:
        pr, pi = (jnp.concatenate([pr, pr * nr - pi * ni], axis=0),
                  jnp.concatenate([pi, pr * ni + pi * nr], axis=0))
        nr, ni = nr * nr - ni * ni, 2.0 * nr * ni
    pr, pi = pr[:L + 1], pi[:L + 1]
    zr = pr[:L, ..., None] * bbr - pi[:L, ..., None] * bbi
    zi = pr[:L, ..., None] * bbi + pi[:L, ..., None] * bbr

    def c_pow(powers_r, powers_i):
        ctr = jnp.transpose(c_re, (0, 1, 3, 2))[:, :, :, None, :]
        cti = jnp.transpose(c_im, (0, 1, 3, 2))[:, :, :, None, :]
        qr = jnp.transpose(powers_r, (1, 2, 3, 0))[..., None]
        qi = jnp.transpose(powers_i, (1, 2, 3, 0))[..., None]
        return ctr * qr - cti * qi, ctr * qi + cti * qr

    wr, wi = c_pow(pr[:L], pi[:L])
    wmat = jnp.concatenate([wr, wi], axis=2).reshape(2 * G, 2 * P, L * C)
    bmat = jnp.concatenate([jnp.transpose(bbr, (0, 1, 3, 2)), -jnp.transpose(bbi, (0, 1, 3, 2))],
                           axis=-1).reshape(2 * G, C, 2 * P)
    kt = pl.pallas_call(
        _s5_ktable_kernel,
        grid=(2 * G,),
        in_specs=[pl.BlockSpec((None, C, 2 * P), lambda g: (g, 0, 0)),
                  pl.BlockSpec((None, 2 * P, L * C), lambda g: (g, 0, 0))],
        out_specs=pl.BlockSpec((None, C, L * C), lambda g: (g, 0, 0)),
        out_shape=jax.ShapeDtypeStruct((2 * G, C, L * C), F32),
        compiler_params=_cparams(("parallel",)),
        name="s5_ktable",
    )(bmat, wmat).reshape(2, G, C, L, C)
    kk = jnp.concatenate([kt[1, :, :, :0:-1], kt[0, :, :, :1] + kt[1, :, :, :1], kt[0, :, :, 1:]], axis=2)
    kk = kk.reshape(G, C, (2 * L - 1) * C)
    kk = jnp.pad(kk, ((0, 0), (0, 0), (0, 2 * L * C - kk.shape[-1])))

    def w_in(z, flip):
        zf = z[::-1] if flip else z
        return jnp.transpose(zf, (1, 0, 3, 2)).reshape(G, L * C, P)

    win = jnp.concatenate([w_in(zr[:, 0], True), w_in(zr[:, 1], False),
                           w_in(zi[:, 0], True), w_in(zi[:, 1], False)], axis=-1).astype(BF16)

    fr, fi = c_pow(pr[1:], pi[1:])
    br, bi = c_pow(pr[:0:-1], pi[:0:-1])
    wout = jnp.concatenate([fr[0], br[1], -fi[0], -bi[1]], axis=1).reshape(G, 4 * P, L * C).astype(BF16)
    al = jnp.concatenate([pr[L, 0], pr[L, 1], pi[L, 0], pi[L, 1]], axis=-1)
    return kk, win, wout, al


def _s5_state_kernel(v_ref, win_ref, o_ref):
    o_ref[...] = jnp.dot(v_ref[...], win_ref[...], preferred_element_type=F32)


def _s5_scan_kernel(s_ref, al_ref, o_ref, *, seg_chunks):
    p2 = 2 * S5_STATE
    alr = al_ref[:, :p2]
    ali = al_ref[:, p2:]
    is_fwd = lax.broadcasted_iota(jnp.int32, alr.shape, 1) < S5_STATE
    zero = jnp.zeros_like(alr)
    start = 0
    for n_chunks in seg_chunks:
        def fwd(k, carry, start=start):
            xr, xi = carry
            c = start + k
            o_ref[c, :, :p2] = xr
            o_ref[c, :, p2:] = xi
            sr = s_ref[c, :, :p2]
            si = s_ref[c, :, p2:]
            return alr * xr - ali * xi + sr, alr * xi + ali * xr + si

        lax.fori_loop(0, n_chunks, fwd, (zero, zero))

        def bwd(k, carry, start=start, n_chunks=n_chunks):
            xr, xi = carry
            c = start + n_chunks - 1 - k
            o_ref[c, :, :p2] = jnp.where(is_fwd, o_ref[c, :, :p2], xr)
            o_ref[c, :, p2:] = jnp.where(is_fwd, o_ref[c, :, p2:], xi)
            sr = s_ref[c, :, :p2]
            si = s_ref[c, :, p2:]
            return alr * xr - ali * xi + sr, alr * xi + ali * xr + si

        lax.fori_loop(0, n_chunks, bwd, (zero, zero))
        start += n_chunks


def _s5_out_kernel(v_ref, kk_ref, x_ref, wout_ref, o_ref, mt_scr):
    L, C = S5_CHUNK, S5_GROUP_CH
    kk = kk_ref[...]
    width = kk.shape[-1]
    per_tile = LANES // C
    for rot in range(per_tile):
        shifted = kk if rot == 0 else pltpu.roll(kk, width - rot * C, axis=1)
        shifted = shifted.astype(BF16)
        for s in range(L):
            lag0 = L - 1 - s
            if lag0 % per_tile == rot:
                col = (lag0 // per_tile) * LANES
                mt_scr[pl.ds(s * C, C), :] = shifted[:, col:col + L * C]
    o_ref[...] = (jnp.dot(v_ref[...], mt_scr[...], preferred_element_type=F32)
                  + jnp.dot(x_ref[...].astype(BF16), wout_ref[...], preferred_element_type=F32))


def _s5(u, tables, seg_lens):
    kk, win, wout, al = tables
    L, G, P, C = S5_CHUNK, S5_GROUPS, S5_STATE, S5_GROUP_CH
    n = u.shape[0]
    nc = n // L
    lc = L * C
    v = jnp.transpose(u.reshape(nc, L, G, C), (2, 0, 1, 3)).reshape(G, nc, lc).astype(BF16)
    states = pl.pallas_call(
        _s5_state_kernel,
        grid=(G,),
        in_specs=[pl.BlockSpec((None, nc, lc), lambda g: (g, 0, 0)),
                  pl.BlockSpec((None, lc, 4 * P), lambda g: (g, 0, 0))],
        out_specs=pl.BlockSpec((nc, 4 * P), lambda g: (0, g)),
        out_shape=jax.ShapeDtypeStruct((nc, G * 4 * P), F32),
        compiler_params=_cparams(("parallel",)),
        name="s5_chunk_state",
    )(v, win)
    gb = SUBLANES
    carried = pl.pallas_call(
        functools.partial(_s5_scan_kernel, seg_chunks=tuple(t // L for t in seg_lens)),
        grid=(G // gb,),
        in_specs=[pl.BlockSpec((nc, gb, 4 * P), lambda g: (0, g, 0)),
                  pl.BlockSpec((gb, 4 * P), lambda g: (g, 0))],
        out_specs=pl.BlockSpec((nc, gb, 4 * P), lambda g: (0, g, 0)),
        out_shape=jax.ShapeDtypeStruct((nc, G, 4 * P), F32),
        compiler_params=_cparams(("parallel",)),
        name="s5_chunk_scan",
    )(states.reshape(nc, G, 4 * P), al)
    y = pl.pallas_call(
        _s5_out_kernel,
        grid=(G,),
        in_specs=[pl.BlockSpec((None, nc, lc), lambda g: (g, 0, 0)),
                  pl.BlockSpec((None, C, 2 * lc), lambda g: (g, 0, 0)),
                  pl.BlockSpec((nc, 4 * P), lambda g: (0, g)),
                  pl.BlockSpec((None, 4 * P, lc), lambda g: (g, 0, 0))],
        out_specs=pl.BlockSpec((None, nc, lc), lambda g: (g, 0, 0)),
        out_shape=jax.ShapeDtypeStruct((G, nc, lc), F32),
        scratch_shapes=[pltpu.VMEM((lc, lc), BF16)],
        compiler_params=_cparams(("parallel",)),
        name="s5_chunk_out",
    )(v, kk, carried.reshape(nc, G * 4 * P), wout)
    return jnp.transpose(y.reshape(G, nc, L, C), (1, 2, 0, 3)).reshape(n, G * C)


def _s5_post_kernel(y_ref, u_ref, d_ref, w_ref, b_ref, o_ref):
    y1 = _gelu_tanh(y_ref[...] + d_ref[...] * u_ref[...])
    gate = jnp.dot(y1.astype(BF16), w_ref[...], preferred_element_type=F32) + b_ref[...]
    o_ref[...] = (y1 * _sigmoid(gate)).astype(o_ref.dtype)


def _s5_post(y5, ze, s5_d, glu_w, glu_b, layer):
    n, w = y5.shape
    tm = 512
    return pl.pallas_call(
        _s5_post_kernel,
        grid=(n // tm,),
        in_specs=[pl.BlockSpec((tm, w), lambda i: (i, 0)), pl.BlockSpec((tm, w), lambda i: (i, E_U)),
                  pl.BlockSpec((1, w), lambda i: (0, 0)), pl.BlockSpec((None, w, w), lambda i: (layer, 0, 0)),
                  pl.BlockSpec((1, w), lambda i: (0, 0))],
        out_specs=pl.BlockSpec((tm, w), lambda i: (i, 0)),
        out_shape=jax.ShapeDtypeStruct((n, w), BF16),
        compiler_params=_cparams(("parallel",)),
        name="s5_post",
    )(y5, ze, s5_d.reshape(1, w), glu_w, glu_b.reshape(1, w))


def _t5_buckets(rel):
    half = REL_BUCKETS // 2
    max_exact = half // 2
    sign = (rel > 0).astype(np.int32) * half
    n = np.abs(rel)
    large = max_exact + (np.log(np.maximum(n, 1) / max_exact)
                         / np.log(REL_MAX_DIST / max_exact) * (half - max_exact)).astype(np.int32)
    large = np.minimum(large, half - 1)
    return sign + np.where(n < max_exact, n, large)


def _att_bias_tile(rel_bias, group, dil):
    rel = np.arange(Q_BLOCK + 2 * HALF_WIN)[None, :] - HALF_WIN - np.arange(Q_BLOCK)[:, None]
    band = np.abs(rel) <= HALF_WIN
    buckets = _t5_buckets(np.where(band, rel, 0) * dil)
    hs = slice(group * HEADS_PER_GROUP, (group + 1) * HEADS_PER_GROUP)
    tile = jnp.transpose(rel_bias[:, hs][buckets], (2, 0, 1)).astype(F32)
    return jnp.where(band[None], tile, NEG_INF)


def _att_kernel(q_ref, kp_ref, km_ref, kn_ref, vp_ref, vm_ref, vn_ref, bias_ref, o_ref, lse_ref,
                *, blk_first, blk_last):
    nb = pl.program_id(1)
    at_first = _any_eq(nb, blk_first)
    at_last = _any_eq(nb, blk_last)
    ncol = Q_BLOCK + 2 * HALF_WIN
    col = lax.broadcasted_iota(jnp.int32, (1, ncol), 1)
    col_ok = jnp.logical_and(col >= jnp.where(at_first, HALF_WIN, 0),
                             col < jnp.where(at_last, HALF_WIN + Q_BLOCK, ncol))
    q = q_ref[...]
    k = jnp.concatenate([kp_ref[...], km_ref[...], kn_ref[...]], axis=0)
    v = jnp.concatenate([vp_ref[...], vm_ref[...], vn_ref[...]], axis=0)
    outs, lses = [], []
    for h in range(HEADS_PER_GROUP):
        hs = slice(h * HEAD_DIM, (h + 1) * HEAD_DIM)
        s = lax.dot_general(q[:, hs], k[:, hs], (((1,), (1,)), ((), ())), preferred_element_type=F32)
        s = jnp.where(col_ok, s + bias_ref[h], NEG_INF)
        m = jnp.max(s, axis=-1, keepdims=True)
        p = jnp.exp(s - m)
        l = jnp.sum(p, axis=-1, keepdims=True)
        o = jnp.dot(p.astype(BF16), v[:, hs], preferred_element_type=F32) / l
        outs.append(o)
        lses.append(jnp.broadcast_to(m + jnp.log(l), (Q_BLOCK, HEAD_DIM)))
    o_ref[...] = jnp.concatenate(outs, axis=-1)
    lse_ref[...] = jnp.concatenate(lses, axis=-1)


def _attention_group(qkv, bias_tile, dil, seg_lens):
    nd = qkv.shape[1]
    nblk = nd // Q_BLOCK
    nhalf = nd // HALF_WIN
    starts = np.cumsum((0,) + tuple(seg_lens)) // (dil * Q_BLOCK)
    blk_first = tuple(int(s) for s in starts[:-1])
    blk_last = tuple(int(s) - 1 for s in starts[1:])
    ncol = Q_BLOCK + 2 * HALF_WIN

    def main(cblk):
        return pl.BlockSpec((None, Q_BLOCK, ATT_OUT), lambda r, b: (r, b, cblk))

    def prev(cblk):
        return pl.BlockSpec((None, HALF_WIN, ATT_OUT), lambda r, b: (r, jnp.maximum(2 * b - 1, 0), cblk))

    def nxt(cblk):
        return pl.BlockSpec((None, HALF_WIN, ATT_OUT), lambda r, b: (r, jnp.minimum(2 * b + 2, nhalf - 1), cblk))

    out_spec = pl.BlockSpec((None, Q_BLOCK, ATT_OUT), lambda r, b: (r, b, 0))
    return pl.pallas_call(
        functools.partial(_att_kernel, blk_first=blk_first, blk_last=blk_last),
        grid=(dil, nblk),
        in_specs=[main(0), prev(1), main(1), nxt(1), prev(2), main(2), nxt(2),
                  _resident((HEADS_PER_GROUP, Q_BLOCK, ncol), lambda r, b: (0, 0, 0))],
        out_specs=[out_spec, out_spec],
        out_shape=[jax.ShapeDtypeStruct((dil, nd, ATT_OUT), F32)] * 2,
        compiler_params=_cparams(("parallel", "parallel")),
        name=f"attention_d{dil}",
    )(qkv, qkv, qkv, qkv, qkv, qkv, qkv, bias_tile)


def _merge_kernel(hf_ref, hb_ref, gl_ref, ys_ref, o0_ref, l0_ref, o1_ref, l1_ref, o2_ref, l2_ref,
                  ga_ref, gb_ref, gc_ref, wl_ref, ws_ref, wa_ref, m_ref, o1_scr, l1_scr, o2_scr, l2_scr, *, tm):
    ntile = ATT_OUT // LANES

    def sequence_order(src, dst, dil):
        rows = tm // dil
        for r in range(dil):
            blk = src[r]
            for c in range(ntile):
                dst[c, pl.ds(r, rows, stride=dil), :] = blk[:, c * LANES:(c + 1) * LANES]
        return jnp.concatenate([dst[c] for c in range(ntile)], axis=-1)

    o1 = sequence_order(o1_ref, o1_scr, ATT_GROUPS[1][1])
    l1 = sequence_order(l1_ref, l1_scr, ATT_GROUPS[1][1])
    o2 = sequence_order(o2_ref, o2_scr, ATT_GROUPS[2][1])
    l2 = sequence_order(l2_ref, l2_scr, ATT_GROUPS[2][1])
    l0 = l0_ref[...]
    mx = jnp.maximum(jnp.maximum(l0, l1), l2)
    e0, e1, e2 = jnp.exp(l0 - mx), jnp.exp(l1 - mx), jnp.exp(l2 - mx)
    yatt = ((o0_ref[...] * e0 + o1 * e1 + o2 * e2) / (e0 + e1 + e2)).astype(BF16)
    ylru = ((hf_ref[...] + hb_ref[...]) * _gelu_tanh(gl_ref[...])).astype(BF16)
    m = (_sigmoid(ga_ref[...]) * jnp.dot(ylru, wl_ref[...], preferred_element_type=F32)
         + _sigmoid(gb_ref[...]) * jnp.dot(ys_ref[...], ws_ref[...], preferred_element_type=F32)
         + _sigmoid(gc_ref[...]) * jnp.dot(yatt, wa_ref[...], preferred_element_type=F32))
    m_ref[...] = m.astype(m_ref.dtype)


def _merge(hs, ze, ys5, att, w_br_lru, w_br_s5, w_br_att, layer):
    n = ze.shape[0]
    d = D_MODEL
    tm = 256
    wl, wa = LRU_WIDTH, ATT_OUT
    gspec = [pl.BlockSpec((tm, d), lambda i, c=c: (i, c)) for c in range(3)]
    att_specs, att_args = [], []
    for (_, dil), (o, l) in zip(ATT_GROUPS, att):
        blk = (None, tm, wa) if dil == 1 else (dil, tm // dil, wa)
        att_specs += [pl.BlockSpec(blk, lambda i: (0, i, 0))] * 2
        att_args += [o, l]
    return pl.pallas_call(
        functools.partial(_merge_kernel, tm=tm),
        grid=(n // tm,),
        in_specs=[pl.BlockSpec((None, tm, wl), lambda i: (0, i, 0)),
                  pl.BlockSpec((None, tm, wl), lambda i: (1, i, 0)),
                  pl.BlockSpec((tm, wl), lambda i: (i, E_GL)),
                  pl.BlockSpec((tm, S5_WIDTH), lambda i: (i, 0))]
        + att_specs + gspec
        + [_resident((None, wl, d), lambda i: (layer, 0, 0)), _resident((None, S5_WIDTH, d), lambda i: (layer, 0, 0)),
           _resident((None, wa, d), lambda i: (layer, 0, 0))],
        out_specs=pl.BlockSpec((tm, d), lambda i: (i, 0)),
        out_shape=jax.ShapeDtypeStruct((n, d), BF16),
        scratch_shapes=[pltpu.VMEM((wa // LANES, tm, LANES), F32)] * 4,
        compiler_params=_cparams(("parallel",)),
        name="merge",
    )(hs, hs, ze, ys5, *att_args, ze, ze, ze, w_br_lru, w_br_s5, w_br_att)


def _out_proj_kernel(x_ref, m_ref, w_ref, g_ref, o_ref):
    mix = jnp.dot(m_ref[...], w_ref[...], preferred_element_type=F32)
    o_ref[...] = x_ref[...] + _rms(mix, g_ref[...])


def _out_proj(x, m, w_out, g, layer):
    n, d = x.shape
    tm = 512
    return pl.pallas_call(
        _out_proj_kernel,
        grid=(n // tm,),
        in_specs=[pl.BlockSpec((tm, d), lambda i: (i, 0)), pl.BlockSpec((tm, d), lambda i: (i, 0)),
                  _resident((None, d, d), lambda i: (layer, 0, 0)), pl.BlockSpec((1, d), lambda i: (0, 0))],
        out_specs=pl.BlockSpec((tm, d), lambda i: (i, 0)),
        out_shape=jax.ShapeDtypeStruct((n, d), F32),
        compiler_params=_cparams(("parallel",)),
        name="mixer_out",
    )(x, m, w_out, g.reshape(1, d))


def _mixer(x, g_pre, g_post, seg_lens, layer, w_in, conv_w, conv_b, lru_wa, lru_ba, lru_wx, lru_bx, lru_L,
           lam_re, lam_im, log_dt, b_re, b_im, c_re, c_im, s5_d, glu_w, glu_b,
           w_br_lru, w_br_s5, w_br_att, w_out, bias_tiles):
    ze = _proj_elementwise(x, g_pre, w_in, layer)
    qkv = _proj_qkv(x, g_pre, w_in, layer)

    wg, bg = _lru_gate_weights(lru_wa, lru_ba, lru_wx, lru_bx)
    hs = _lru(ze, conv_w, conv_b, wg, bg, lru_L.reshape(2, 1, LRU_WIDTH), seg_lens)

    tables = _s5_prepare(lam_re, lam_im, log_dt, b_re, b_im, c_re, c_im)
    y5 = _s5(ze[:, E_U * S5_WIDTH:(E_U + 1) * S5_WIDTH], tables, seg_lens)
    ys5 = _s5_post(y5, ze, s5_d, glu_w, glu_b, layer)

    att = [_attention_group(qkv[g], bias_tiles[g], dil, seg_lens) for g, (_, dil) in enumerate(ATT_GROUPS)]

    m = _merge(hs, ze, ys5, att, w_br_lru, w_br_s5, w_br_att, layer)
    return _out_proj(x, m, w_out, g_post, layer)


def kernel(x_prompt, x_sample, norm_g, w_in, lru_conv_w, lru_conv_b, lru_wa, lru_ba, lru_wx, lru_bx, lru_L,
           s5_lam_re, s5_lam_im, s5_log_dt, s5_b_re, s5_b_im, s5_c_re, s5_c_im, s5_d, s5_glu_w, s5_glu_b,
           rel_bias, w_br_lru, w_br_s5, w_br_att, w_out, ffn_w1, ffn_w3, ffn_w2):
    bp, tp, d = x_prompt.shape
    bs, ts, _ = x_sample.shape
    seg_lens = (tp,) * bp + (ts,) * bs
    rows = (bp * tp, bs * ts)
    n = sum(rows)
    depth = norm_g.shape[0]
    w1, w3, w2 = ffn_w1.astype(BF16), ffn_w3.astype(BF16), ffn_w2.astype(BF16)
    w_in_b, glu_b16, w_out_b = w_in.astype(BF16), s5_glu_w.astype(BF16), w_out.astype(BF16)
    wbl, wbs, wba = w_br_lru.astype(BF16), w_br_s5.astype(BF16), w_br_att.astype(BF16)
    bias_tiles = [_att_bias_tile(rel_bias, g, dil) for g, (_, dil) in enumerate(ATT_GROUPS)]

    xs = [x_prompt.reshape(rows[0], d), x_sample.reshape(rows[1], d)]
    for l in range(depth):
        g = norm_g[l]
        (x,) = _ffn(xs, g[0], g[1], w1, w3, w2, l, 0, (n,))
        x = _mixer(x, g[2], g[3], seg_lens, l, w_in_b, lru_conv_w[l], lru_conv_b[l], lru_wa[l], lru_ba[l],
                   lru_wx[l], lru_bx[l], lru_L[l], s5_lam_re[l], s5_lam_im[l], s5_log_dt[l], s5_b_re[l],
                   s5_b_im[l], s5_c_re[l], s5_c_im[l], s5_d[l], glu_b16, s5_glu_b[l], wbl, wbs, wba, w_out_b,
                   bias_tiles)
        xs = _ffn([x], g[4], g[5], w1, w3, w2, l, 1, rows if l == depth - 1 else (n,))
    return (xs[0].reshape(bp, tp, d), xs[1].reshape(bs, ts, d))
```

```python
import functools
import math

import numpy as np
import jax
import jax.numpy as jnp
from jax import lax
from jax.experimental import pallas as pl
from jax.experimental.pallas import tpu as pltpu

F32 = jnp.float32
BF16 = jnp.bfloat16

D_MODEL = 2048
LRU_WIDTH = 1024
LRU_BLOCKS = 16
LRU_CONV_W = 4
LRU_C = 8.0
S5_WIDTH = 1024
S5_GROUP_CH = 16
S5_GROUPS = 64
S5_STATE = 64
HEAD_DIM = 64
ATT_GROUPS = ((128, 1), (512, 4), (2048, 16))
ATT_WIDTH = 1536
HEADS_PER_GROUP = 8
ATT_OUT = 512
REL_BUCKETS = 32
REL_MAX_DIST = 1024
RMS_EPS = 1e-6
NEG_INF = -1e30

Q_BLOCK = 128
HALF_WIN = 64
S5_CHUNK = 64
LANES = 128
SUBLANES = 8
COL_QKV = 3 * 1024
E_TILE = 1536
E_XL, E_GL, E_U = 0, 1, 2
VMEM_LIMIT = 48 * 1024 * 1024


def _cparams(sem, vmem_limit=VMEM_LIMIT):
    return pltpu.CompilerParams(dimension_semantics=sem, vmem_limit_bytes=vmem_limit)


def _rms(v, g):
    return v * lax.rsqrt(jnp.mean(v * v, axis=-1, keepdims=True) + RMS_EPS) * g


def _gelu_tanh(v):
    return 0.5 * v * (1.0 + jnp.tanh(math.sqrt(2.0 / math.pi) * (v + 0.044715 * (v * v * v))))


def _sigmoid(v):
    return 0.5 * jnp.tanh(0.5 * v) + 0.5


def _any_eq(idx, values):
    hit = idx == values[0]
    for v in values[1:]:
        hit = jnp.logical_or(hit, idx == v)
    return hit


def _resident(shape, index_map):
    return pl.BlockSpec(shape, index_map, pipeline_mode=pl.Buffered(1))


def _ffn_kernel(*refs, nj, in_tiles, out_tiles):
    n_in, n_out = len(in_tiles), len(out_tiles)
    x_refs = refs[:n_in]
    gpre_ref, gpost_ref, w1_ref, w3_ref, w2_ref = refs[n_in:n_in + 5]
    o_refs = refs[n_in + 5:n_in + 5 + n_out]
    h_scr, acc_scr = refs[n_in + 5 + n_out:]
    i = pl.program_id(0)
    j = pl.program_id(1)

    def active(bounds, k):
        lo = sum(bounds[:k])
        return jnp.logical_and(i >= lo, i < lo + bounds[k])

    for k in range(n_in):
        @pl.when(jnp.logical_and(j == 0, active(in_tiles, k)))
        def _(k=k):
            h_scr[...] = _rms(x_refs[k][...], gpre_ref[...]).astype(BF16)
            acc_scr[...] = jnp.zeros_like(acc_scr)

    h = h_scr[...]
    a = jnp.dot(h, w1_ref[...], preferred_element_type=F32)
    b = jnp.dot(h, w3_ref[...], preferred_element_type=F32)
    g = (a * _sigmoid(a) * b).astype(BF16)
    acc_scr[...] += jnp.dot(g, w2_ref[...], preferred_element_type=F32)

    for ki in range(n_in):
        for ko in range(n_out):
            @pl.when(jnp.logical_and(j == nj - 1, jnp.logical_and(active(in_tiles, ki), active(out_tiles, ko))))
            def _(ki=ki, ko=ko):
                o_refs[ko][...] = x_refs[ki][...] + 0.5 * _rms(acc_scr[...], gpost_ref[...])


def _ffn(xs, g_pre, g_post, w1, w3, w2, layer, which, out_rows):
    d = xs[0].shape[1]
    dff = w1.shape[-1]
    tm, tf = 512, 512
    nj = dff // tf
    in_tiles = tuple(x.shape[0] // tm for x in xs)
    out_tiles = tuple(r // tm for r in out_rows)

    def piece(bounds, k):
        lo = sum(bounds[:k])
        return lambda i, j: (jnp.clip(i - lo, 0, bounds[k] - 1), 0)

    const = lambda i, j: (0, 0)
    wcol = pl.BlockSpec((None, None, d, tf), lambda i, j: (layer, which, 0, j))
    outs = pl.pallas_call(
        functools.partial(_ffn_kernel, nj=nj, in_tiles=in_tiles, out_tiles=out_tiles),
        grid=(sum(in_tiles), nj),
        in_specs=[pl.BlockSpec((tm, d), piece(in_tiles, k)) for k in range(len(xs))]
        + [pl.BlockSpec((1, d), const), pl.BlockSpec((1, d), const), wcol, wcol,
           pl.BlockSpec((None, None, tf, d), lambda i, j: (layer, which, j, 0))],
        out_specs=[pl.BlockSpec((tm, d), piece(out_tiles, k)) for k in range(len(out_rows))],
        out_shape=[jax.ShapeDtypeStruct((r, d), F32) for r in out_rows],
        scratch_shapes=[pltpu.VMEM((tm, d), BF16), pltpu.VMEM((tm, d), F32)],
        compiler_params=_cparams(("parallel", "arbitrary"),
                                 VMEM_LIMIT + (len(xs) + len(out_rows) - 2) * 2 * tm * d * 4),
        name="ffn",
    )(*xs, g_pre.reshape(1, d), g_post.reshape(1, d), w1, w3, w2)
    return list(outs)


def _norm_mm_kernel(x_ref, g_ref, w_ref, gates_ref, e_ref, h_scr, *, ngate):
    j = pl.program_id(1)

    @pl.when(j == 0)
    def _():
        h_scr[...] = _rms(x_ref[...], g_ref[...]).astype(BF16)

    z = jnp.dot(h_scr[...], w_ref[...], preferred_element_type=F32)

    @pl.when(j < ngate)
    def _():
        gates_ref[...] = z.astype(gates_ref.dtype)

    @pl.when(j >= ngate)
    def _():
        e_ref[...] = z


def _proj_elementwise(x, g, w_in, layer):
    n, d = x.shape
    tm, tn = 512, E_TILE
    lead = COL_QKV // tn
    skip = 3 * ATT_WIDTH // tn
    ncol = (w_in.shape[-1] - 3 * ATT_WIDTH) // tn
    ngate = ncol - lead
    return pl.pallas_call(
        functools.partial(_norm_mm_kernel, ngate=ngate),
        grid=(n // tm, ncol),
        in_specs=[pl.BlockSpec((tm, d), lambda i, j: (i, 0)), pl.BlockSpec((1, d), lambda i, j: (0, 0)),
                  pl.BlockSpec((None, d, tn),
                               lambda i, j: (layer, 0, jnp.where(j < ngate, j + lead + skip, j - ngate)))],
        out_specs=[pl.BlockSpec((tm, tn), lambda i, j: (i, jnp.minimum(j, ngate - 1))),
                   pl.BlockSpec((tm, tn), lambda i, j: (i, jnp.maximum(j - ngate, 0)))],
        out_shape=[jax.ShapeDtypeStruct((n, ngate * tn), BF16), jax.ShapeDtypeStruct((n, lead * tn), F32)],
        scratch_shapes=[pltpu.VMEM((tm, d), BF16)],
        compiler_params=_cparams(("parallel", "arbitrary")),
        name="mixer_in",
    )(x, g.reshape(1, d), w_in)


def _qkv_kernel(x_ref, g_ref, wq_ref, wk_ref, wv_ref, o0_ref, o1_ref, o2_ref, h_scr, res_scr, *, tm):
    grp = pl.program_id(1)
    o_refs = (o0_ref, o1_ref, o2_ref)

    @pl.when(grp == 0)
    def _():
        h_scr[...] = _rms(x_ref[...], g_ref[...]).astype(BF16)

    h = h_scr[...]
    q = jnp.dot(h, wq_ref[...], preferred_element_type=F32) * (HEAD_DIM ** -0.5)
    k = jnp.dot(h, wk_ref[...], preferred_element_type=F32)
    v = jnp.dot(h, wv_ref[...], preferred_element_type=F32)
    res = jnp.concatenate([q, k, v], axis=-1)
    ntile = res.shape[-1] // LANES
    for gi, (_, dil) in enumerate(ATT_GROUPS):
        @pl.when(grp == gi)
        def _(gi=gi, dil=dil):
            if dil == 1:
                o_refs[gi][0] = res.astype(BF16)
                return
            rows = tm // dil
            for c in range(ntile):
                res_scr[c] = res[:, c * LANES:(c + 1) * LANES]
            for r in range(dil):
                o_refs[gi][r] = jnp.concatenate(
                    [res_scr[c, pl.ds(r, rows, stride=dil), :] for c in range(ntile)], axis=-1).astype(BF16)


def _proj_qkv(x, g, w_in, layer):
    n, d = x.shape
    tm = 512
    qb = COL_QKV // ATT_OUT

    def wspec(off):
        return pl.BlockSpec((None, d, ATT_OUT), lambda i, gq: (layer, 0, qb + off + gq))

    ng = len(ATT_GROUPS)
    return pl.pallas_call(
        functools.partial(_qkv_kernel, tm=tm),
        grid=(n // tm, ng),
        in_specs=[pl.BlockSpec((tm, d), lambda i, gq: (i, 0)), pl.BlockSpec((1, d), lambda i, gq: (0, 0)),
                  wspec(0), wspec(ng), wspec(2 * ng)],
        out_specs=[pl.BlockSpec((dil, tm // dil, 3 * ATT_OUT), lambda i, gq: (0, i, 0)) for _, dil in ATT_GROUPS],
        out_shape=[jax.ShapeDtypeStruct((dil, n // dil, 3 * ATT_OUT), BF16) for _, dil in ATT_GROUPS],
        scratch_shapes=[pltpu.VMEM((tm, d), BF16), pltpu.VMEM((3 * ATT_OUT // LANES, tm, LANES), F32)],
        compiler_params=_cparams(("parallel", "arbitrary")),
        name="mixer_qkv",
    )(x, g.reshape(1, d), w_in, w_in, w_in)


def _lru_kernel(*refs, tc, nt, seg_first, seg_last, d):
    xm_ref, xp_ref, xn_ref, cw_ref, cb_ref, wg_ref, bg_ref, lam_ref = refs[:8]
    if d == 0:
        o_ref, xpad_scr, a_scr, b_scr, hl_scr, p_scr, h_scr = refs[8:]
    else:
        hf_ref, gl_ref, o_ref, xpad_scr, a_scr, b_scr, hl_scr, p_scr, h_scr = refs[8:]
    i = pl.program_id(0)
    ti = i if d == 0 else nt - 1 - i
    w = LRU_WIDTH
    at_first = _any_eq(ti, seg_first)
    at_last = _any_eq(ti, seg_last)

    xpad_scr[pl.ds(0, SUBLANES), :] = xp_ref[...] * jnp.where(at_first, 0.0, 1.0)
    xpad_scr[pl.ds(SUBLANES, tc), :] = xm_ref[...]
    xpad_scr[pl.ds(SUBLANES + tc, SUBLANES), :] = xn_ref[...] * jnp.where(at_last, 0.0, 1.0)
    left = LRU_CONV_W // 2
    xc = cb_ref[...] + xpad_scr[pl.ds(SUBLANES - left, tc), :] * cw_ref[pl.ds(0, 1), :]
    for kk in range(1, LRU_CONV_W):
        xc = xc + xpad_scr[pl.ds(SUBLANES - left + kk, tc), :] * cw_ref[pl.ds(kk, 1), :]

    gates = jnp.dot(xc.astype(BF16), wg_ref[...], preferred_element_type=F32) + bg_ref[...]
    r = _sigmoid(gates[:, :w])
    ig = _sigmoid(gates[:, w:])
    nlam = -lam_ref[...]
    softplus = jnp.maximum(nlam, 0.0) + jnp.log(1.0 + jnp.exp(-jnp.abs(nlam)))
    a = jnp.exp(-LRU_C * r * softplus)
    bb = jnp.sqrt(1.0 - a * a) * (ig * xc)

    sub = tc // SUBLANES
    pitch = sub + SUBLANES
    ntile = w // LANES
    for c in range(ntile):
        for jj in range(SUBLANES):
            a_scr[c, pl.ds(jj * pitch, sub), :] = a[jj * sub:(jj + 1) * sub, c * LANES:(c + 1) * LANES]
            b_scr[c, pl.ds(jj * pitch, sub), :] = bb[jj * sub:(jj + 1) * sub, c * LANES:(c + 1) * LANES]

    def strided_rows(ref, k):
        return jnp.concatenate([ref[c, pl.ds(k, SUBLANES, stride=pitch), :] for c in range(ntile)], axis=-1)

    @pl.when(at_first if d == 0 else at_last)
    def _():
        h_scr[...] = jnp.zeros_like(h_scr)

    sub_id = lax.broadcasted_iota(jnp.int32, (SUBLANES, w), 0)
    hl = jnp.zeros((SUBLANES, w), F32)
    pp = jnp.ones((SUBLANES, w), F32)
    for k in (range(sub) if d == 0 else range(sub - 1, -1, -1)):
        av = strided_rows(a_scr, k)
        bv = strided_rows(b_scr, k)
        hl = av * hl + bv
        pp = av * pp
        hl_scr[pl.ds(k * SUBLANES, SUBLANES), :] = hl
        p_scr[pl.ds(k * SUBLANES, SUBLANES), :] = pp
    cur = h_scr[pl.ds(0, 1), :]
    carry = jnp.zeros((SUBLANES, w), F32)
    for jj in (range(SUBLANES) if d == 0 else range(SUBLANES - 1, -1, -1)):
        carry = jnp.where(sub_id == jj, cur, carry)
        cur = hl[jj:jj + 1, :] + pp[jj:jj + 1, :] * cur
    h_scr[pl.ds(0, 1), :] = cur
    for k in range(sub):
        rows = pl.ds(k * SUBLANES, SUBLANES)
        hv = hl_scr[rows, :] + p_scr[rows, :] * carry
        for c in range(ntile):
            a_scr[c, pl.ds(k, SUBLANES, stride=pitch), :] = hv[:, c * LANES:(c + 1) * LANES]
    for jj in range(SUBLANES):
        rows = pl.ds(jj * sub, sub)
        hv = jnp.concatenate([a_scr[c, pl.ds(jj * pitch, sub), :] for c in range(ntile)], axis=-1)
        if d == 0:
            o_ref[rows, :] = hv
        else:
            o_ref[rows, :] = ((hf_ref[rows, :] + hv) * _gelu_tanh(gl_ref[rows, :])).astype(o_ref.dtype)


def _lru(ze, conv_w, conv_b, wg, bg, lam, seg_lens):
    n = ze.shape[0]
    w = LRU_WIDTH
    tc = 256
    nt = n // tc
    starts = np.cumsum((0,) + tuple(seg_lens))
    seg_first = tuple(int(s) // tc for s in starts[:-1])
    seg_last = tuple(int(s) // tc - 1 for s in starts[1:])
    hb = tc // SUBLANES
    nhb = n // SUBLANES
    pitched = SUBLANES * (tc // SUBLANES + SUBLANES)

    def one_direction(d, extra_specs, extra_args, out_dtype):
        tile = (lambda i: i) if d == 0 else (lambda i: nt - 1 - i)
        return pl.pallas_call(
            functools.partial(_lru_kernel, tc=tc, nt=nt, seg_first=seg_first, seg_last=seg_last, d=d),
            grid=(nt,),
            in_specs=[pl.BlockSpec((tc, w), lambda i: (tile(i), E_XL)),
                      pl.BlockSpec((SUBLANES, w), lambda i: (jnp.maximum(tile(i) * hb - 1, 0), E_XL)),
                      pl.BlockSpec((SUBLANES, w), lambda i: (jnp.minimum((tile(i) + 1) * hb, nhb - 1), E_XL)),
                      pl.BlockSpec((LRU_CONV_W, w), lambda i: (0, 0)),
                      pl.BlockSpec((1, w), lambda i: (0, 0)),
                      _resident((None, w, 2 * w), lambda i: (d, 0, 0)),
                      pl.BlockSpec((None, 1, 2 * w), lambda i: (d, 0, 0)),
                      pl.BlockSpec((None, 1, w), lambda i: (d, 0, 0))] + [spec(tile) for spec in extra_specs],
            out_specs=pl.BlockSpec((tc, w), lambda i: (tile(i), 0)),
            out_shape=jax.ShapeDtypeStruct((n, w), out_dtype),
            scratch_shapes=[pltpu.VMEM((tc + 2 * SUBLANES, w), F32), pltpu.VMEM((w // LANES, pitched, LANES), F32),
                            pltpu.VMEM((w // LANES, pitched, LANES), F32), pltpu.VMEM((tc, w), F32),
                            pltpu.VMEM((tc, w), F32), pltpu.VMEM((SUBLANES, w), F32)],
            compiler_params=_cparams(("arbitrary",)),
            name="rglru_fwd" if d == 0 else "rglru_bwd",
        )(ze, ze, ze, conv_w, conv_b.reshape(1, w), wg, bg, lam, *extra_args)

    hf = one_direction(0, [], [], F32)
    return one_direction(1, [lambda tile: pl.BlockSpec((tc, w), lambda i: (tile(i), 0)),
                             lambda tile: pl.BlockSpec((tc, w), lambda i: (tile(i), E_GL))], [hf, ze], BF16)


def _lru_gate_weights(wa, ba, wx, bx):
    def dense(wb):
        eye = jnp.eye(LRU_BLOCKS, dtype=wb.dtype)
        full = wb[:, :, :, None, :] * eye[None, :, None, :, None]
        return full.reshape(2, LRU_WIDTH, LRU_WIDTH)

    wg = jnp.concatenate([dense(wa), dense(wx)], axis=-1).astype(BF16)
    bg = jnp.concatenate([ba, bx], axis=-1).reshape(2, 1, 2 * LRU_WIDTH)
    return wg, bg


def _s5_ktable_kernel(b_ref, w_ref, o_ref):
    o_ref[...] = jnp.dot(b_ref[...], w_ref[...], preferred_element_type=F32, precision=lax.Precision.HIGHEST)


def _s5_prepare(lam_re, lam_im, log_dt, b_re, b_im, c_re, c_im):
    L, G, P, C = S5_CHUNK, S5_GROUPS, S5_STATE, S5_GROUP_CH
    dt = jnp.exp(log_dt)[..., None]
    mag = jnp.exp(lam_re * dt)
    ar = mag * jnp.cos(lam_im * dt)
    ai = mag * jnp.sin(lam_im * dt)
    den = lam_re * lam_re + lam_im * lam_im
    cr = ((ar - 1.0) * lam_re + ai * lam_im) / den
    ci = (ai * lam_re - (ar - 1.0) * lam_im) / den
    bbr = cr[..., None] * b_re - ci[..., None] * b_im
    bbi = cr[..., None] * b_im + ci[..., None] * b_re
    pr, pi = jnp.ones_like(ar)[None], jnp.zeros_like(ai)[None]
    nr, ni = ar, ai
    while pr.shape[0] < L + 1:
        pr, pi = (jnp.concatenate([pr, pr * nr - pi * ni], axis=0),
                  jnp.concatenate([pi, pr * ni + pi * nr], axis=0))
        nr, ni = nr * nr - ni * ni, 2.0 * nr * ni
    pr, pi = pr[:L + 1], pi[:L + 1]
    zr = pr[:L, ..., None] * bbr - pi[:L, ..., None] * bbi
    zi = pr[:L, ..., None] * bbi + pi[:L, ..., None] * bbr

    def c_pow(powers_r, powers_i):
        ctr = jnp.transpose(c_re, (0, 1, 3, 2))[:, :, :, None, :]
        cti = jnp.transpose(c_im, (0, 1, 3, 2))[:, :, :, None, :]
        qr = jnp.transpose(powers_r, (1, 2, 3, 0))[..., None]
        qi = jnp.transpose(powers_i, (1, 2, 3, 0))[..., None]
        return ctr * qr - cti * qi, ctr * qi + cti * qr

    wr, wi = c_pow(pr[:L], pi[:L])
    wmat = jnp.concatenate([wr, wi], axis=2).reshape(2 * G, 2 * P, L * C)
    bmat = jnp.concatenate([jnp.transpose(bbr, (0, 1, 3, 2)), -jnp.transpose(bbi, (0, 1, 3, 2))],
                           axis=-1).reshape(2 * G, C, 2 * P)
    kt = pl.pallas_call(
        _s5_ktable_kernel,
        grid=(2 * G,),
        in_specs=[pl.BlockSpec((None, C, 2 * P), lambda g: (g, 0, 0)),
                  pl.BlockSpec((None, 2 * P, L * C), lambda g: (g, 0, 0))],
        out_specs=pl.BlockSpec((None, C, L * C), lambda g: (g, 0, 0)),
        out_shape=jax.ShapeDtypeStruct((2 * G, C, L * C), F32),
        compiler_params=_cparams(("parallel",)),
        name="s5_ktable",
    )(bmat, wmat).reshape(2, G, C, L, C)
    kk = jnp.concatenate([kt[1, :, :, :0:-1], kt[0, :, :, :1] + kt[1, :, :, :1], kt[0, :, :, 1:]], axis=2)
    kk = kk.reshape(G, C, (2 * L - 1) * C)
    kk = jnp.pad(kk, ((0, 0), (0, 0), (0, 2 * L * C - kk.shape[-1])))

    def w_in(z, flip):
        zf = z[::-1] if flip else z
        return jnp.transpose(zf, (1, 0, 3, 2)).reshape(G, L * C, P)

    win = jnp.concatenate([w_in(zr[:, 0], True), w_in(zr[:, 1], False),
                           w_in(zi[:, 0], True), w_in(zi[:, 1], False)], axis=-1).astype(BF16)

    fr, fi = c_pow(pr[1:], pi[1:])
    br, bi = c_pow(pr[:0:-1], pi[:0:-1])
    wout = jnp.concatenate([fr[0], br[1], -fi[0], -bi[1]], axis=1).reshape(G, 4 * P, L * C).astype(BF16)
    al = jnp.concatenate([pr[L, 0], pr[L, 1], pi[L, 0], pi[L, 1]], axis=-1)
    return kk, win, wout, al


def _s5_state_kernel(v_ref, win_ref, o_ref):
    o_ref[...] = jnp.dot(v_ref[...], win_ref[...], preferred_element_type=F32)


def _s5_scan_kernel(s_ref, al_ref, o_ref, *, seg_chunks):
    p2 = 2 * S5_STATE
    alr = al_ref[:, :p2]
    ali = al_ref[:, p2:]
    is_fwd = lax.broadcasted_iota(jnp.int32, alr.shape, 1) < S5_STATE
    zero = jnp.zeros_like(alr)
    start = 0
    for n_chunks in seg_chunks:
        def fwd(k, carry, start=start):
            xr, xi = carry
            c = start + k
            o_ref[c, :, :p2] = xr
            o_ref[c, :, p2:] = xi
            sr = s_ref[c, :, :p2]
            si = s_ref[c, :, p2:]
            return alr * xr - ali * xi + sr, alr * xi + ali * xr + si

        lax.fori_loop(0, n_chunks, fwd, (zero, zero))

        def bwd(k, carry, start=start, n_chunks=n_chunks):
            xr, xi = carry
            c = start + n_chunks - 1 - k
            o_ref[c, :, :p2] = jnp.where(is_fwd, o_ref[c, :, :p2], xr)
            o_ref[c, :, p2:] = jnp.where(is_fwd, o_ref[c, :, p2:], xi)
            sr = s_ref[c, :, :p2]
            si = s_ref[c, :, p2:]
            return alr * xr - ali * xi + sr, alr * xi + ali * xr + si

        lax.fori_loop(0, n_chunks, bwd, (zero, zero))
        start += n_chunks


def _s5_out_kernel(v_ref, kk_ref, x_ref, wout_ref, o_ref, mt_scr):
    L, C = S5_CHUNK, S5_GROUP_CH
    kk = kk_ref[...]
    width = kk.shape[-1]
    per_tile = LANES // C
    for rot in range(per_tile):
        shifted = kk if rot == 0 else pltpu.roll(kk, width - rot * C, axis=1)
        shifted = shifted.astype(BF16)
        for s in range(L):
            lag0 = L - 1 - s
            if lag0 % per_tile == rot:
                col = (lag0 // per_tile) * LANES
                mt_scr[pl.ds(s * C, C), :] = shifted[:, col:col + L * C]
    o_ref[...] = (jnp.dot(v_ref[...], mt_scr[...], preferred_element_type=F32)
                  + jnp.dot(x_ref[...].astype(BF16), wout_ref[...], preferred_element_type=F32))


def _s5(u, tables, seg_lens):
    kk, win, wout, al = tables
    L, G, P, C = S5_CHUNK, S5_GROUPS, S5_STATE, S5_GROUP_CH
    n = u.shape[0]
    nc = n // L
    lc = L * C
    v = jnp.transpose(u.reshape(nc, L, G, C), (2, 0, 1, 3)).reshape(G, nc, lc).astype(BF16)
    states = pl.pallas_call(
        _s5_state_kernel,
        grid=(G,),
        in_specs=[pl.BlockSpec((None, nc, lc), lambda g: (g, 0, 0)),
                  pl.BlockSpec((None, lc, 4 * P), lambda g: (g, 0, 0))],
        out_specs=pl.BlockSpec((nc, 4 * P), lambda g: (0, g)),
        out_shape=jax.ShapeDtypeStruct((nc, G * 4 * P), F32),
        compiler_params=_cparams(("parallel",)),
        name="s5_chunk_state",
    )(v, win)
    gb = SUBLANES
    carried = pl.pallas_call(
        functools.partial(_s5_scan_kernel, seg_chunks=tuple(t // L for t in seg_lens)),
        grid=(G // gb,),
        in_specs=[pl.BlockSpec((nc, gb, 4 * P), lambda g: (0, g, 0)),
                  pl.BlockSpec((gb, 4 * P), lambda g: (g, 0))],
        out_specs=pl.BlockSpec((nc, gb, 4 * P), lambda g: (0, g, 0)),
        out_shape=jax.ShapeDtypeStruct((nc, G, 4 * P), F32),
        compiler_params=_cparams(("parallel",)),
        name="s5_chunk_scan",
    )(states.reshape(nc, G, 4 * P), al)
    y = pl.pallas_call(
        _s5_out_kernel,
        grid=(G,),
        in_specs=[pl.BlockSpec((None, nc, lc), lambda g: (g, 0, 0)),
                  pl.BlockSpec((None, C, 2 * lc), lambda g: (g, 0, 0)),
                  pl.BlockSpec((nc, 4 * P), lambda g: (0, g)),
                  pl.BlockSpec((None, 4 * P, lc), lambda g: (g, 0, 0))],
        out_specs=pl.BlockSpec((None, nc, lc), lambda g: (g, 0, 0)),
        out_shape=jax.ShapeDtypeStruct((G, nc, lc), F32),
        scratch_shapes=[pltpu.VMEM((lc, lc), BF16)],
        compiler_params=_cparams(("parallel",)),
        name="s5_chunk_out",
    )(v, kk, carried.reshape(nc, G * 4 * P), wout)
    return jnp.transpose(y.reshape(G, nc, L, C), (1, 2, 0, 3)).reshape(n, G * C)


def _s5_post_kernel(y_ref, u_ref, d_ref, w_ref, b_ref, o_ref):
    y1 = _gelu_tanh(y_ref[...] + d_ref[...] * u_ref[...])
    gate = jnp.dot(y1.astype(BF16), w_ref[...], preferred_element_type=F32) + b_ref[...]
    o_ref[...] = (y1 * _sigmoid(gate)).astype(o_ref.dtype)


def _s5_post(y5, ze, s5_d, glu_w, glu_b, layer):
    n, w = y5.shape
    tm = 512
    return pl.pallas_call(
        _s5_post_kernel,
        grid=(n // tm,),
        in_specs=[pl.BlockSpec((tm, w), lambda i: (i, 0)), pl.BlockSpec((tm, w), lambda i: (i, E_U)),
                  pl.BlockSpec((1, w), lambda i: (0, 0)), pl.BlockSpec((None, w, w), lambda i: (layer, 0, 0)),
                  pl.BlockSpec((1, w), lambda i: (0, 0))],
        out_specs=pl.BlockSpec((tm, w), lambda i: (i, 0)),
        out_shape=jax.ShapeDtypeStruct((n, w), BF16),
        compiler_params=_cparams(("parallel",)),
        name="s5_post",
    )(y5, ze, s5_d.reshape(1, w), glu_w, glu_b.reshape(1, w))


def _t5_buckets(rel):
    half = REL_BUCKETS // 2
    max_exact = half // 2
    sign = (rel > 0).astype(np.int32) * half
    n = np.abs(rel)
    large = max_exact + (np.log(np.maximum(n, 1) / max_exact)
                         / np.log(REL_MAX_DIST / max_exact) * (half - max_exact)).astype(np.int32)
    large = np.minimum(large, half - 1)
    return sign + np.where(n < max_exact, n, large)


def _att_bias_tile(rel_bias, group, dil):
    ncol = Q_BLOCK + 2 * HALF_WIN
    hs = slice(group * HEADS_PER_GROUP, (group + 1) * HEADS_PER_GROUP)
    offs = np.arange(-HALF_WIN, HALF_WIN + 1)
    vals = jnp.transpose(rel_bias[:, hs][_t5_buckets(offs * dil)]).astype(F32)
    width = 2 * ncol
    pad_lo = Q_BLOCK
    line = jnp.pad(vals, ((0, 0), (pad_lo, width - pad_lo - vals.shape[1])), constant_values=NEG_INF)
    rows = jnp.broadcast_to(line[:, None, :], (HEADS_PER_GROUP, Q_BLOCK, width)).reshape(HEADS_PER_GROUP, -1)
    skew = rows[:, :Q_BLOCK * (width - 1)].reshape(HEADS_PER_GROUP, Q_BLOCK, width - 1)
    tile = skew[:, :, pad_lo:pad_lo + ncol]
    col = np.arange(ncol)[None, None, :]
    before = col < HALF_WIN
    after = col >= HALF_WIN + Q_BLOCK
    return jnp.stack([tile, jnp.where(before, NEG_INF, tile), jnp.where(after, NEG_INF, tile),
                      jnp.where(before | after, NEG_INF, tile)])


def _att_kernel(q_ref, kp_ref, km_ref, kn_ref, vp_ref, vm_ref, vn_ref, bias_ref, o_ref, lse_ref):
    ncol = Q_BLOCK + 2 * HALF_WIN
    q = q_ref[...]
    k = jnp.concatenate([kp_ref[...], km_ref[...], kn_ref[...]], axis=0)
    v = jnp.concatenate([vp_ref[...], vm_ref[...], vn_ref[...]], axis=0)
    low = lax.broadcasted_iota(jnp.int32, (Q_BLOCK, LANES), 1) < HEAD_DIM
    ones = jnp.ones((ncol, LANES), BF16)
    zero = jnp.zeros((Q_BLOCK, LANES), BF16)
    heads = range(HEADS_PER_GROUP)
    pair_cols = [slice((h // 2) * LANES, (h // 2 + 1) * LANES) for h in heads]
    scores = []
    for h in heads:
        q2 = q[:, pair_cols[h]]
        qh = jnp.where(low, q2, zero) if h % 2 == 0 else jnp.where(low, zero, q2)
        s = lax.dot_general(qh, k[:, pair_cols[h]], (((1,), (1,)), ((), ())), preferred_element_type=F32)
        scores.append(s + bias_ref[h])
    maxes = [jnp.max(s, axis=-1, keepdims=True) for s in scores]
    probs = [jnp.exp(s - m).astype(BF16) for s, m in zip(scores, maxes)]
    outs = [jnp.dot(p, v[:, pair_cols[h]], preferred_element_type=F32) for h, p in zip(heads, probs)]
    sums = [jnp.dot(p, ones, preferred_element_type=F32) for p in probs]
    for pair in range(HEADS_PER_GROUP // 2):
        a, b = 2 * pair, 2 * pair + 1
        l = jnp.where(low, sums[a], sums[b])
        o_ref[:, pair_cols[a]] = (jnp.where(low, outs[a], outs[b]) / l).astype(o_ref.dtype)
        lse_ref[:, pair_cols[a]] = jnp.where(low, maxes[a], maxes[b]) + jnp.log(l)


def _attention_group(qkv, bias_tile, dil, seg_lens):
    nd = qkv.shape[1]
    nblk = nd // Q_BLOCK
    nhalf = nd // HALF_WIN
    starts = np.cumsum((0,) + tuple(seg_lens)) // (dil * Q_BLOCK)
    blk_first = tuple(int(s) for s in starts[:-1])
    blk_last = tuple(int(s) - 1 for s in starts[1:])
    ncol = Q_BLOCK + 2 * HALF_WIN

    def variant(b):
        return _any_eq(b, blk_first).astype(jnp.int32) + 2 * _any_eq(b, blk_last).astype(jnp.int32)

    def main(cblk):
        return pl.BlockSpec((None, Q_BLOCK, ATT_OUT), lambda r, b: (r, b, cblk))

    def prev(cblk):
        return pl.BlockSpec((None, HALF_WIN, ATT_OUT), lambda r, b: (r, jnp.maximum(2 * b - 1, 0), cblk))

    def nxt(cblk):
        return pl.BlockSpec((None, HALF_WIN, ATT_OUT), lambda r, b: (r, jnp.minimum(2 * b + 2, nhalf - 1), cblk))

    out_spec = pl.BlockSpec((None, Q_BLOCK, ATT_OUT), lambda r, b: (r, b, 0))
    return pl.pallas_call(
        _att_kernel,
        grid=(dil, nblk),
        in_specs=[main(0), prev(1), main(1), nxt(1), prev(2), main(2), nxt(2),
                  pl.BlockSpec((None, HEADS_PER_GROUP, Q_BLOCK, ncol), lambda r, b: (variant(b), 0, 0, 0))],
        out_specs=[out_spec, out_spec],
        out_shape=[jax.ShapeDtypeStruct((dil, nd, ATT_OUT), BF16), jax.ShapeDtypeStruct((dil, nd, ATT_OUT), F32)],
        compiler_params=_cparams(("parallel", "parallel")),
        name=f"attention_d{dil}",
    )(qkv, qkv, qkv, qkv, qkv, qkv, qkv, bias_tile)


def _merge_kernel(yl_ref, ys_ref, o0_ref, l0_ref, o1_ref, l1_ref, o2_ref, l2_ref,
                  ga_ref, gb_ref, gc_ref, wl_ref, ws_ref, wa_ref, m_ref, o1_scr, l1_scr, o2_scr, l2_scr, *, tm):
    ntile = ATT_OUT // LANES

    def sequence_order(src, dst, dil):
        rows = tm // dil
        for r in range(dil):
            blk = src[r].astype(F32)
            for c in range(ntile):
                dst[c, pl.ds(r, rows, stride=dil), :] = blk[:, c * LANES:(c + 1) * LANES]
        return jnp.concatenate([dst[c] for c in range(ntile)], axis=-1)

    o1 = sequence_order(o1_ref, o1_scr, ATT_GROUPS[1][1])
    l1 = sequence_order(l1_ref, l1_scr, ATT_GROUPS[1][1])
    o2 = sequence_order(o2_ref, o2_scr, ATT_GROUPS[2][1])
    l2 = sequence_order(l2_ref, l2_scr, ATT_GROUPS[2][1])
    l0 = l0_ref[...]
    mx = jnp.maximum(jnp.maximum(l0, l1), l2)
    e0, e1, e2 = jnp.exp(l0 - mx), jnp.exp(l1 - mx), jnp.exp(l2 - mx)
    yatt = ((o0_ref[...].astype(F32) * e0 + o1 * e1 + o2 * e2) / (e0 + e1 + e2)).astype(BF16)

    def gate(ref):
        return _sigmoid(ref[...].astype(F32))

    m = (gate(ga_ref) * jnp.dot(yl_ref[...], wl_ref[...], preferred_element_type=F32)
         + gate(gb_ref) * jnp.dot(ys_ref[...], ws_ref[...], preferred_element_type=F32)
         + gate(gc_ref) * jnp.dot(yatt, wa_ref[...], preferred_element_type=F32))
    m_ref[...] = m.astype(m_ref.dtype)


def _merge(ylru, gates, ys5, att, w_br_lru, w_br_s5, w_br_att, layer):
    n = gates.shape[0]
    d = D_MODEL
    tm = 256
    wl, wa = LRU_WIDTH, ATT_OUT
    gspec = [pl.BlockSpec((tm, d), lambda i, c=c: (i, c)) for c in range(3)]
    att_specs, att_args = [], []
    for (_, dil), (o, l) in zip(ATT_GROUPS, att):
        blk = (None, tm, wa) if dil == 1 else (dil, tm // dil, wa)
        att_specs += [pl.BlockSpec(blk, lambda i: (0, i, 0))] * 2
        att_args += [o, l]
    return pl.pallas_call(
        functools.partial(_merge_kernel, tm=tm),
        grid=(n // tm,),
        in_specs=[pl.BlockSpec((tm, wl), lambda i: (i, 0)), pl.BlockSpec((tm, S5_WIDTH), lambda i: (i, 0))]
        + att_specs + gspec
        + [_resident((None, wl, d), lambda i: (layer, 0, 0)), _resident((None, S5_WIDTH, d), lambda i: (layer, 0, 0)),
           _resident((None, wa, d), lambda i: (layer, 0, 0))],
        out_specs=pl.BlockSpec((tm, d), lambda i: (i, 0)),
        out_shape=jax.ShapeDtypeStruct((n, d), BF16),
        scratch_shapes=[pltpu.VMEM((wa // LANES, tm, LANES), F32)] * 4,
        compiler_params=_cparams(("parallel",)),
        name="merge",
    )(ylru, ys5, *att_args, gates, gates, gates, w_br_lru, w_br_s5, w_br_att)


def _out_proj_kernel(x_ref, m_ref, w_ref, g_ref, o_ref):
    mix = jnp.dot(m_ref[...], w_ref[...], preferred_element_type=F32)
    o_ref[...] = x_ref[...] + _rms(mix, g_ref[...])


def _out_proj(x, m, w_out, g, layer):
    n, d = x.shape
    tm = 512
    return pl.pallas_call(
        _out_proj_kernel,
        grid=(n // tm,),
        in_specs=[pl.BlockSpec((tm, d), lambda i: (i, 0)), pl.BlockSpec((tm, d), lambda i: (i, 0)),
                  _resident((None, d, d), lambda i: (layer, 0, 0)), pl.BlockSpec((1, d), lambda i: (0, 0))],
        out_specs=pl.BlockSpec((tm, d), lambda i: (i, 0)),
        out_shape=jax.ShapeDtypeStruct((n, d), F32),
        compiler_params=_cparams(("parallel",)),
        name="mixer_out",
    )(x, m, w_out, g.reshape(1, d))


def _mixer(x, g_pre, g_post, seg_lens, layer, w_in, conv_w, conv_b, lru_wa, lru_ba, lru_wx, lru_bx, lru_L,
           lam_re, lam_im, log_dt, b_re, b_im, c_re, c_im, s5_d, glu_w, glu_b,
           w_br_lru, w_br_s5, w_br_att, w_out, bias_tiles):
    gates, ze = _proj_elementwise(x, g_pre, w_in, layer)
    qkv = _proj_qkv(x, g_pre, w_in, layer)

    wg, bg = _lru_gate_weights(lru_wa, lru_ba, lru_wx, lru_bx)
    ylru = _lru(ze, conv_w, conv_b, wg, bg, lru_L.reshape(2, 1, LRU_WIDTH), seg_lens)

    tables = _s5_prepare(lam_re, lam_im, log_dt, b_re, b_im, c_re, c_im)
    y5 = _s5(ze[:, E_U * S5_WIDTH:(E_U + 1) * S5_WIDTH], tables, seg_lens)
    ys5 = _s5_post(y5, ze, s5_d, glu_w, glu_b, layer)

    att = [_attention_group(qkv[g], bias_tiles[g], dil, seg_lens) for g, (_, dil) in enumerate(ATT_GROUPS)]

    m = _merge(ylru, gates, ys5, att, w_br_lru, w_br_s5, w_br_att, layer)
    return _out_proj(x, m, w_out, g_post, layer)


def kernel(x_prompt, x_sample, norm_g, w_in, lru_conv_w, lru_conv_b, lru_wa, lru_ba, lru_wx, lru_bx, lru_L,
           s5_lam_re, s5_lam_im, s5_log_dt, s5_b_re, s5_b_im, s5_c_re, s5_c_im, s5_d, s5_glu_w, s5_glu_b,
           rel_bias, w_br_lru, w_br_s5, w_br_att, w_out, ffn_w1, ffn_w3, ffn_w2):
    bp, tp, d = x_prompt.shape
    bs, ts, _ = x_sample.shape
    seg_lens = (tp,) * bp + (ts,) * bs
    rows = (bp * tp, bs * ts)
    n = sum(rows)
    depth = norm_g.shape[0]
    w1, w3, w2 = ffn_w1.astype(BF16), ffn_w3.astype(BF16), ffn_w2.astype(BF16)
    w_in_b, glu_b16, w_out_b = w_in.astype(BF16), s5_glu_w.astype(BF16), w_out.astype(BF16)
    wbl, wbs, wba = w_br_lru.astype(BF16), w_br_s5.astype(BF16), w_br_att.astype(BF16)
    bias_tiles = [_att_bias_tile(rel_bias, g, dil) for g, (_, dil) in enumerate(ATT_GROUPS)]

    xs = [x_prompt.reshape(rows[0], d), x_sample.reshape(rows[1], d)]
    for l in range(depth):
        g = norm_g[l]
        (x,) = _ffn(xs, g[0], g[1], w1, w3, w2, l, 0, (n,))
        x = _mixer(x, g[2], g[3], seg_lens, l, w_in_b, lru_conv_w[l], lru_conv_b[l], lru_wa[l], lru_ba[l],
                   lru_wx[l], lru_bx[l], lru_L[l], s5_lam_re[l], s5_lam_im[l], s5_log_dt[l], s5_b_re[l],
                   s5_b_im[l], s5_c_re[l], s5_c_im[l], s5_d[l], glu_b16, s5_glu_b[l], wbl, wbs, wba, w_out_b,
                   bias_tiles)
        xs = _ffn([x], g[4], g[5], w1, w3, w2, l, 1, rows if l == depth - 1 else (n,))
    return (xs[0].reshape(bp, tp, d), xs[1].reshape(bs, ts, d))
```

```python
import functools
import math

import numpy as np
import jax
import jax.numpy as jnp
from jax import lax
from jax.experimental import pallas as pl
from jax.experimental.pallas import tpu as pltpu

F32 = jnp.float32
BF16 = jnp.bfloat16

D_MODEL = 2048
LRU_WIDTH = 1024
LRU_BLOCKS = 16
LRU_CONV_W = 4
LRU_C = 8.0
S5_WIDTH = 1024
S5_GROUP_CH = 16
S5_GROUPS = 64
S5_STATE = 64
HEAD_DIM = 64
ATT_GROUPS = ((128, 1), (512, 4), (2048, 16))
ATT_WIDTH = 1536
HEADS_PER_GROUP = 8
ATT_OUT = 512
REL_BUCKETS = 32
REL_MAX_DIST = 1024
RMS_EPS = 1e-6
NEG_INF = -1e30

Q_BLOCK = 128
HALF_WIN = 64
ATT_STEP_BLOCKS = 4
S5_CHUNK = 64
LANES = 128
SUBLANES = 8
COL_QKV = 3 * 1024
E_TILE = 1536
E_XL, E_GL, E_U = 0, 1, 2
VMEM_LIMIT = 48 * 1024 * 1024


def _cparams(sem, vmem_limit=VMEM_LIMIT):
    return pltpu.CompilerParams(dimension_semantics=sem, vmem_limit_bytes=vmem_limit)


def _rms(v, g):
    return v * lax.rsqrt(jnp.mean(v * v, axis=-1, keepdims=True) + RMS_EPS) * g


def _gelu_tanh(v):
    return 0.5 * v * (1.0 + jnp.tanh(math.sqrt(2.0 / math.pi) * (v + 0.044715 * (v * v * v))))


def _sigmoid(v):
    return 0.5 * jnp.tanh(0.5 * v) + 0.5


def _any_eq(idx, values):
    hit = idx == values[0]
    for v in values[1:]:
        hit = jnp.logical_or(hit, idx == v)
    return hit


def _resident(shape, index_map):
    return pl.BlockSpec(shape, index_map, pipeline_mode=pl.Buffered(1))


def _ffn_kernel(*refs, nj, in_tiles, out_tiles):
    n_in, n_out = len(in_tiles), len(out_tiles)
    x_refs = refs[:n_in]
    gpre_ref, gpost_ref, w1_ref, w3_ref, w2_ref = refs[n_in:n_in + 5]
    o_refs = refs[n_in + 5:n_in + 5 + n_out]
    h_scr, acc_scr = refs[n_in + 5 + n_out:]
    i = pl.program_id(0)
    j = pl.program_id(1)

    def active(bounds, k):
        lo = sum(bounds[:k])
        return jnp.logical_and(i >= lo, i < lo + bounds[k])

    for k in range(n_in):
        @pl.when(jnp.logical_and(j == 0, active(in_tiles, k)))
        def _(k=k):
            h_scr[...] = _rms(x_refs[k][...], gpre_ref[...]).astype(BF16)
            acc_scr[...] = jnp.zeros_like(acc_scr)

    h = h_scr[...]
    a = jnp.dot(h, w1_ref[...], preferred_element_type=F32)
    b = jnp.dot(h, w3_ref[...], preferred_element_type=F32)
    g = (a * _sigmoid(a) * b).astype(BF16)
    acc_scr[...] += jnp.dot(g, w2_ref[...], preferred_element_type=F32)

    for ki in range(n_in):
        for ko in range(n_out):
            @pl.when(jnp.logical_and(j == nj - 1, jnp.logical_and(active(in_tiles, ki), active(out_tiles, ko))))
            def _(ki=ki, ko=ko):
                o_refs[ko][...] = x_refs[ki][...] + 0.5 * _rms(acc_scr[...], gpost_ref[...])


def _ffn(xs, g_pre, g_post, w1, w3, w2, layer, which, out_rows):
    d = xs[0].shape[1]
    dff = w1.shape[-1]
    tm, tf = 512, 512
    nj = dff // tf
    in_tiles = tuple(x.shape[0] // tm for x in xs)
    out_tiles = tuple(r // tm for r in out_rows)

    def piece(bounds, k):
        lo = sum(bounds[:k])
        return lambda i, j: (jnp.clip(i - lo, 0, bounds[k] - 1), 0)

    const = lambda i, j: (0, 0)
    wcol = pl.BlockSpec((None, None, d, tf), lambda i, j: (layer, which, 0, j))
    outs = pl.pallas_call(
        functools.partial(_ffn_kernel, nj=nj, in_tiles=in_tiles, out_tiles=out_tiles),
        grid=(sum(in_tiles), nj),
        in_specs=[pl.BlockSpec((tm, d), piece(in_tiles, k)) for k in range(len(xs))]
        + [pl.BlockSpec((1, d), const), pl.BlockSpec((1, d), const), wcol, wcol,
           pl.BlockSpec((None, None, tf, d), lambda i, j: (layer, which, j, 0))],
        out_specs=[pl.BlockSpec((tm, d), piece(out_tiles, k)) for k in range(len(out_rows))],
        out_shape=[jax.ShapeDtypeStruct((r, d), F32) for r in out_rows],
        scratch_shapes=[pltpu.VMEM((tm, d), BF16), pltpu.VMEM((tm, d), F32)],
        compiler_params=_cparams(("parallel", "arbitrary"),
                                 VMEM_LIMIT + (len(xs) + len(out_rows) - 2) * 2 * tm * d * 4),
        name="ffn",
    )(*xs, g_pre.reshape(1, d), g_post.reshape(1, d), w1, w3, w2)
    return list(outs)


def _norm_mm_kernel(x_ref, g_ref, w_ref, gates_ref, e_ref, h_scr, *, ngate):
    j = pl.program_id(1)

    @pl.when(j == 0)
    def _():
        h_scr[...] = _rms(x_ref[...], g_ref[...]).astype(BF16)

    z = jnp.dot(h_scr[...], w_ref[...], preferred_element_type=F32)

    @pl.when(j < ngate)
    def _():
        gates_ref[...] = z.astype(gates_ref.dtype)

    @pl.when(j >= ngate)
    def _():
        e_ref[...] = z


def _proj_elementwise(x, g, w_in, layer):
    n, d = x.shape
    tm, tn = 512, E_TILE
    lead = COL_QKV // tn
    skip = 3 * ATT_WIDTH // tn
    ncol = (w_in.shape[-1] - 3 * ATT_WIDTH) // tn
    ngate = ncol - lead
    return pl.pallas_call(
        functools.partial(_norm_mm_kernel, ngate=ngate),
        grid=(n // tm, ncol),
        in_specs=[pl.BlockSpec((tm, d), lambda i, j: (i, 0)), pl.BlockSpec((1, d), lambda i, j: (0, 0)),
                  pl.BlockSpec((None, d, tn),
                               lambda i, j: (layer, 0, jnp.where(j < ngate, j + lead + skip, j - ngate)))],
        out_specs=[pl.BlockSpec((tm, tn), lambda i, j: (i, jnp.minimum(j, ngate - 1))),
                   pl.BlockSpec((tm, tn), lambda i, j: (i, jnp.maximum(j - ngate, 0)))],
        out_shape=[jax.ShapeDtypeStruct((n, ngate * tn), BF16), jax.ShapeDtypeStruct((n, lead * tn), F32)],
        scratch_shapes=[pltpu.VMEM((tm, d), BF16)],
        compiler_params=_cparams(("parallel", "arbitrary")),
        name="mixer_in",
    )(x, g.reshape(1, d), w_in)


def _qkv_kernel(x_ref, g_ref, wq_ref, wk_ref, wv_ref, o0_ref, o1_ref, o2_ref, h_scr, res_scr, *, tm):
    grp = pl.program_id(1)
    o_refs = (o0_ref, o1_ref, o2_ref)

    @pl.when(grp == 0)
    def _():
        h_scr[...] = _rms(x_ref[...], g_ref[...]).astype(BF16)

    h = h_scr[...]
    q = jnp.dot(h, wq_ref[...], preferred_element_type=F32) * (HEAD_DIM ** -0.5)
    k = jnp.dot(h, wk_ref[...], preferred_element_type=F32)
    v = jnp.dot(h, wv_ref[...], preferred_element_type=F32)
    res = jnp.concatenate([q, k, v], axis=-1)
    ntile = res.shape[-1] // LANES
    for gi, (_, dil) in enumerate(ATT_GROUPS):
        @pl.when(grp == gi)
        def _(gi=gi, dil=dil):
            if dil == 1:
                o_refs[gi][0] = res.astype(BF16)
                return
            rows = tm // dil
            for c in range(ntile):
                res_scr[c] = res[:, c * LANES:(c + 1) * LANES]
            for r in range(dil):
                o_refs[gi][r] = jnp.concatenate(
                    [res_scr[c, pl.ds(r, rows, stride=dil), :] for c in range(ntile)], axis=-1).astype(BF16)


def _proj_qkv(x, g, w_in, layer):
    n, d = x.shape
    tm = 512
    qb = COL_QKV // ATT_OUT

    def wspec(off):
        return pl.BlockSpec((None, d, ATT_OUT), lambda i, gq: (layer, 0, qb + off + gq))

    ng = len(ATT_GROUPS)
    return pl.pallas_call(
        functools.partial(_qkv_kernel, tm=tm),
        grid=(n // tm, ng),
        in_specs=[pl.BlockSpec((tm, d), lambda i, gq: (i, 0)), pl.BlockSpec((1, d), lambda i, gq: (0, 0)),
                  wspec(0), wspec(ng), wspec(2 * ng)],
        out_specs=[pl.BlockSpec((dil, tm // dil, 3 * ATT_OUT), lambda i, gq: (0, i, 0)) for _, dil in ATT_GROUPS],
        out_shape=[jax.ShapeDtypeStruct((dil, n // dil, 3 * ATT_OUT), BF16) for _, dil in ATT_GROUPS],
        scratch_shapes=[pltpu.VMEM((tm, d), BF16), pltpu.VMEM((3 * ATT_OUT // LANES, tm, LANES), F32)],
        compiler_params=_cparams(("parallel", "arbitrary")),
        name="mixer_qkv",
    )(x, g.reshape(1, d), w_in, w_in, w_in)


def _lru_kernel(*refs, tc, nt, seg_first, seg_last, d):
    xm_ref, xp_ref, xn_ref, cw_ref, cb_ref, wg_ref, bg_ref, lam_ref = refs[:8]
    if d == 0:
        o_ref, xpad_scr, a_scr, b_scr, hl_scr, p_scr, h_scr = refs[8:]
    else:
        hf_ref, gl_ref, o_ref, xpad_scr, a_scr, b_scr, hl_scr, p_scr, h_scr = refs[8:]
    i = pl.program_id(0)
    ti = i if d == 0 else nt - 1 - i
    w = LRU_WIDTH
    at_first = _any_eq(ti, seg_first)
    at_last = _any_eq(ti, seg_last)

    xpad_scr[pl.ds(0, SUBLANES), :] = xp_ref[...] * jnp.where(at_first, 0.0, 1.0)
    xpad_scr[pl.ds(SUBLANES, tc), :] = xm_ref[...]
    xpad_scr[pl.ds(SUBLANES + tc, SUBLANES), :] = xn_ref[...] * jnp.where(at_last, 0.0, 1.0)
    left = LRU_CONV_W // 2
    xc = cb_ref[...] + xpad_scr[pl.ds(SUBLANES - left, tc), :] * cw_ref[pl.ds(0, 1), :]
    for kk in range(1, LRU_CONV_W):
        xc = xc + xpad_scr[pl.ds(SUBLANES - left + kk, tc), :] * cw_ref[pl.ds(kk, 1), :]

    xcb = xc.astype(BF16)
    parts = [jnp.dot(xcb[:, p * LANES:(p + 1) * LANES], wg_ref[p], preferred_element_type=F32)
             for p in range(w // LANES)]
    r = _sigmoid(jnp.concatenate([g[:, :LANES] for g in parts], axis=-1) + bg_ref[:, :w])
    ig = _sigmoid(jnp.concatenate([g[:, LANES:] for g in parts], axis=-1) + bg_ref[:, w:])
    nlam = -lam_ref[...]
    softplus = jnp.maximum(nlam, 0.0) + jnp.log(1.0 + jnp.exp(-jnp.abs(nlam)))
    a = jnp.exp(-LRU_C * r * softplus)
    bb = jnp.sqrt(1.0 - a * a) * (ig * xc)

    sub = tc // SUBLANES
    pitch = sub + SUBLANES
    ntile = w // LANES
    for c in range(ntile):
        for jj in range(SUBLANES):
            a_scr[c, pl.ds(jj * pitch, sub), :] = a[jj * sub:(jj + 1) * sub, c * LANES:(c + 1) * LANES]
            b_scr[c, pl.ds(jj * pitch, sub), :] = bb[jj * sub:(jj + 1) * sub, c * LANES:(c + 1) * LANES]

    def strided_rows(ref, k):
        return jnp.concatenate([ref[c, pl.ds(k, SUBLANES, stride=pitch), :] for c in range(ntile)], axis=-1)

    @pl.when(at_first if d == 0 else at_last)
    def _():
        h_scr[...] = jnp.zeros_like(h_scr)

    sub_id = lax.broadcasted_iota(jnp.int32, (SUBLANES, w), 0)
    hl = jnp.zeros((SUBLANES, w), F32)
    pp = jnp.ones((SUBLANES, w), F32)
    for k in (range(sub) if d == 0 else range(sub - 1, -1, -1)):
        av = strided_rows(a_scr, k)
        bv = strided_rows(b_scr, k)
        hl = av * hl + bv
        pp = av * pp
        hl_scr[pl.ds(k * SUBLANES, SUBLANES), :] = hl
        p_scr[pl.ds(k * SUBLANES, SUBLANES), :] = pp
    cur = h_scr[pl.ds(0, 1), :]
    carry = jnp.zeros((SUBLANES, w), F32)
    for jj in (range(SUBLANES) if d == 0 else range(SUBLANES - 1, -1, -1)):
        carry = jnp.where(sub_id == jj, cur, carry)
        cur = hl[jj:jj + 1, :] + pp[jj:jj + 1, :] * cur
    h_scr[pl.ds(0, 1), :] = cur
    for k in range(sub):
        rows = pl.ds(k * SUBLANES, SUBLANES)
        hv = hl_scr[rows, :] + p_scr[rows, :] * carry
        for c in range(ntile):
            a_scr[c, pl.ds(k, SUBLANES, stride=pitch), :] = hv[:, c * LANES:(c + 1) * LANES]
    for jj in range(SUBLANES):
        rows = pl.ds(jj * sub, sub)
        hv = jnp.concatenate([a_scr[c, pl.ds(jj * pitch, sub), :] for c in range(ntile)], axis=-1)
        if d == 0:
            o_ref[rows, :] = hv
        else:
            o_ref[rows, :] = ((hf_ref[rows, :] + hv) * _gelu_tanh(gl_ref[rows, :])).astype(o_ref.dtype)


def _lru(ze, conv_w, conv_b, wg, bg, lam, seg_lens):
    n = ze.shape[0]
    w = LRU_WIDTH
    tc = 256
    nt = n // tc
    starts = np.cumsum((0,) + tuple(seg_lens))
    seg_first = tuple(int(s) // tc for s in starts[:-1])
    seg_last = tuple(int(s) // tc - 1 for s in starts[1:])
    hb = tc // SUBLANES
    nhb = n // SUBLANES
    pitched = SUBLANES * (tc // SUBLANES + SUBLANES)

    def one_direction(d, extra_specs, extra_args, out_dtype):
        tile = (lambda i: i) if d == 0 else (lambda i: nt - 1 - i)
        return pl.pallas_call(
            functools.partial(_lru_kernel, tc=tc, nt=nt, seg_first=seg_first, seg_last=seg_last, d=d),
            grid=(nt,),
            in_specs=[pl.BlockSpec((tc, w), lambda i: (tile(i), E_XL)),
                      pl.BlockSpec((SUBLANES, w), lambda i: (jnp.maximum(tile(i) * hb - 1, 0), E_XL)),
                      pl.BlockSpec((SUBLANES, w), lambda i: (jnp.minimum((tile(i) + 1) * hb, nhb - 1), E_XL)),
                      pl.BlockSpec((LRU_CONV_W, w), lambda i: (0, 0)),
                      pl.BlockSpec((1, w), lambda i: (0, 0)),
                      _resident((None, w // LANES, LANES, 2 * LANES), lambda i: (d, 0, 0, 0)),
                      pl.BlockSpec((None, 1, 2 * w), lambda i: (d, 0, 0)),
                      pl.BlockSpec((None, 1, w), lambda i: (d, 0, 0))] + [spec(tile) for spec in extra_specs],
            out_specs=pl.BlockSpec((tc, w), lambda i: (tile(i), 0)),
            out_shape=jax.ShapeDtypeStruct((n, w), out_dtype),
            scratch_shapes=[pltpu.VMEM((tc + 2 * SUBLANES, w), F32), pltpu.VMEM((w // LANES, pitched, LANES), F32),
                            pltpu.VMEM((w // LANES, pitched, LANES), F32), pltpu.VMEM((tc, w), F32),
                            pltpu.VMEM((tc, w), F32), pltpu.VMEM((SUBLANES, w), F32)],
            compiler_params=_cparams(("arbitrary",)),
            name="rglru_fwd" if d == 0 else "rglru_bwd",
        )(ze, ze, ze, conv_w, conv_b.reshape(1, w), wg, bg, lam, *extra_args)

    hf = one_direction(0, [], [], F32)
    return one_direction(1, [lambda tile: pl.BlockSpec((tc, w), lambda i: (tile(i), 0)),
                             lambda tile: pl.BlockSpec((tc, w), lambda i: (tile(i), E_GL))], [hf, ze], BF16)


def _lru_gate_weights(wa, ba, wx, bx):
    per_tile = LANES * LRU_BLOCKS // LRU_WIDTH

    def tiles(wb):
        bw = wb.shape[-1]
        eye = jnp.eye(per_tile, dtype=wb.dtype)
        grouped = wb.reshape(2, LRU_BLOCKS // per_tile, per_tile, bw, bw)
        full = grouped[:, :, :, :, None, :] * eye[None, None, :, None, :, None]
        return full.reshape(2, LRU_BLOCKS // per_tile, LANES, LANES)

    wg = jnp.concatenate([tiles(wa), tiles(wx)], axis=-1).astype(BF16)
    bg = jnp.concatenate([ba, bx], axis=-1).reshape(2, 1, 2 * LRU_WIDTH)
    return wg, bg


def _s5_ktable_kernel(b_ref, w_ref, o_ref):
    o_ref[...] = jnp.dot(b_ref[...], w_ref[...], preferred_element_type=F32, precision=lax.Precision.HIGHEST)


def _s5_prepare(lam_re, lam_im, log_dt, b_re, b_im, c_re, c_im):
    L, G, P, C = S5_CHUNK, S5_GROUPS, S5_STATE, S5_GROUP_CH
    dt = jnp.exp(log_dt)[..., None]
    mag = jnp.exp(lam_re * dt)
    ar = mag * jnp.cos(lam_im * dt)
    ai = mag * jnp.sin(lam_im * dt)
    den = lam_re * lam_re + lam_im * lam_im
    cr = ((ar - 1.0) * lam_re + ai * lam_im) / den
    ci = (ai * lam_re - (ar - 1.0) * lam_im) / den
    bbr = cr[..., None] * b_re - ci[..., None] * b_im
    bbi = cr[..., None] * b_im + ci[..., None] * b_re
    pr, pi = jnp.ones_like(ar)[None], jnp.zeros_like(ai)[None]
    nr, ni = ar, ai
    while pr.shape[0] < L + 1:
        pr, pi = (jnp.concatenate([pr, pr * nr - pi * ni], axis=0),
                  jnp.concatenate([pi, pr * ni + pi * nr], axis=0))
        nr, ni = nr * nr - ni * ni, 2.0 * nr * ni
    pr, pi = pr[:L + 1], pi[:L + 1]
    zr = pr[:L, ..., None] * bbr - pi[:L, ..., None] * bbi
    zi = pr[:L, ..., None] * bbi + pi[:L, ..., None] * bbr

    def c_pow(powers_r, powers_i):
        ctr = jnp.transpose(c_re, (0, 1, 3, 2))[:, :, :, None, :]
        cti = jnp.transpose(c_im, (0, 1, 3, 2))[:, :, :, None, :]
        qr = jnp.transpose(powers_r, (1, 2, 3, 0))[..., None]
        qi = jnp.transpose(powers_i, (1, 2, 3, 0))[..., None]
        return ctr * qr - cti * qi, ctr * qi + cti * qr

    wr, wi = c_pow(pr[:L], pi[:L])
    wmat = jnp.concatenate([wr, wi], axis=2).reshape(2 * G, 2 * P, L * C)
    bmat = jnp.concatenate([jnp.transpose(bbr, (0, 1, 3, 2)), -jnp.transpose(bbi, (0, 1, 3, 2))],
                           axis=-1).reshape(2 * G, C, 2 * P)
    kt = pl.pallas_call(
        _s5_ktable_kernel,
        grid=(2 * G,),
        in_specs=[pl.BlockSpec((None, C, 2 * P), lambda g: (g, 0, 0)),
                  pl.BlockSpec((None, 2 * P, L * C), lambda g: (g, 0, 0))],
        out_specs=pl.BlockSpec((None, C, L * C), lambda g: (g, 0, 0)),
        out_shape=jax.ShapeDtypeStruct((2 * G, C, L * C), F32),
        compiler_params=_cparams(("parallel",)),
        name="s5_ktable",
    )(bmat, wmat).reshape(2, G, C, L, C)
    kk = jnp.concatenate([kt[1, :, :, :0:-1], kt[0, :, :, :1] + kt[1, :, :, :1], kt[0, :, :, 1:]], axis=2)
    kk = kk.reshape(G, C, (2 * L - 1) * C)
    kk = jnp.pad(kk, ((0, 0), (0, 0), (0, 2 * L * C - kk.shape[-1])))

    def w_in(z, flip):
        zf = z[::-1] if flip else z
        return jnp.transpose(zf, (1, 0, 3, 2)).reshape(G, L * C, P)

    win = jnp.concatenate([w_in(zr[:, 0], True), w_in(zr[:, 1], False),
                           w_in(zi[:, 0], True), w_in(zi[:, 1], False)], axis=-1).astype(BF16)

    fr, fi = c_pow(pr[1:], pi[1:])
    br, bi = c_pow(pr[:0:-1], pi[:0:-1])
    wout = jnp.concatenate([fr[0], br[1], -fi[0], -bi[1]], axis=1).reshape(G, 4 * P, L * C).astype(BF16)
    al = jnp.concatenate([pr[L, 0], pr[L, 1], pi[L, 0], pi[L, 1]], axis=-1)
    return kk, win, wout, al


def _s5_state_kernel(v_ref, win_ref, o_ref):
    o_ref[...] = jnp.dot(v_ref[...], win_ref[...], preferred_element_type=F32)


def _s5_scan_kernel(s_ref, al_ref, o_ref, *, seg_chunks):
    p2 = 2 * S5_STATE
    alr = al_ref[:, :p2]
    ali = al_ref[:, p2:]
    is_fwd = lax.broadcasted_iota(jnp.int32, alr.shape, 1) < S5_STATE
    zero = jnp.zeros_like(alr)
    start = 0
    for n_chunks in seg_chunks:
        def fwd(k, carry, start=start):
            xr, xi = carry
            c = start + k
            o_ref[c, :, :p2] = xr
            o_ref[c, :, p2:] = xi
            sr = s_ref[c, :, :p2]
            si = s_ref[c, :, p2:]
            return alr * xr - ali * xi + sr, alr * xi + ali * xr + si

        lax.fori_loop(0, n_chunks, fwd, (zero, zero))

        def bwd(k, carry, start=start, n_chunks=n_chunks):
            xr, xi = carry
            c = start + n_chunks - 1 - k
            o_ref[c, :, :p2] = jnp.where(is_fwd, o_ref[c, :, :p2], xr)
            o_ref[c, :, p2:] = jnp.where(is_fwd, o_ref[c, :, p2:], xi)
            sr = s_ref[c, :, :p2]
            si = s_ref[c, :, p2:]
            return alr * xr - ali * xi + sr, alr * xi + ali * xr + si

        lax.fori_loop(0, n_chunks, bwd, (zero, zero))
        start += n_chunks


def _s5_out_kernel(v_ref, kk_ref, x_ref, wout_ref, o_ref, mt_scr):
    L, C = S5_CHUNK, S5_GROUP_CH
    kk = kk_ref[...]
    width = kk.shape[-1]
    per_tile = LANES // C
    for rot in range(per_tile):
        shifted = kk if rot == 0 else pltpu.roll(kk, width - rot * C, axis=1)
        shifted = shifted.astype(BF16)
        for s in range(L):
            lag0 = L - 1 - s
            if lag0 % per_tile == rot:
                col = (lag0 // per_tile) * LANES
                mt_scr[pl.ds(s * C, C), :] = shifted[:, col:col + L * C]
    o_ref[...] = (jnp.dot(v_ref[...], mt_scr[...], preferred_element_type=F32)
                  + jnp.dot(x_ref[...].astype(BF16), wout_ref[...], preferred_element_type=F32)).astype(o_ref.dtype)


def _s5(u, tables, seg_lens):
    kk, win, wout, al = tables
    L, G, P, C = S5_CHUNK, S5_GROUPS, S5_STATE, S5_GROUP_CH
    n = u.shape[0]
    nc = n // L
    lc = L * C
    v = jnp.transpose(u.reshape(nc, L, G, C), (2, 0, 1, 3)).reshape(G, nc, lc).astype(BF16)
    states = pl.pallas_call(
        _s5_state_kernel,
        grid=(G,),
        in_specs=[pl.BlockSpec((None, nc, lc), lambda g: (g, 0, 0)),
                  pl.BlockSpec((None, lc, 4 * P), lambda g: (g, 0, 0))],
        out_specs=pl.BlockSpec((nc, 4 * P), lambda g: (0, g)),
        out_shape=jax.ShapeDtypeStruct((nc, G * 4 * P), F32),
        compiler_params=_cparams(("parallel",)),
        name="s5_chunk_state",
    )(v, win)
    gb = SUBLANES
    carried = pl.pallas_call(
        functools.partial(_s5_scan_kernel, seg_chunks=tuple(t // L for t in seg_lens)),
        grid=(G // gb,),
        in_specs=[pl.BlockSpec((nc, gb, 4 * P), lambda g: (0, g, 0)),
                  pl.BlockSpec((gb, 4 * P), lambda g: (g, 0))],
        out_specs=pl.BlockSpec((nc, gb, 4 * P), lambda g: (0, g, 0)),
        out_shape=jax.ShapeDtypeStruct((nc, G, 4 * P), F32),
        compiler_params=_cparams(("parallel",)),
        name="s5_chunk_scan",
    )(states.reshape(nc, G, 4 * P), al)
    y = pl.pallas_call(
        _s5_out_kernel,
        grid=(G,),
        in_specs=[pl.BlockSpec((None, nc, lc), lambda g: (g, 0, 0)),
                  pl.BlockSpec((None, C, 2 * lc), lambda g: (g, 0, 0)),
                  pl.BlockSpec((nc, 4 * P), lambda g: (0, g)),
                  pl.BlockSpec((None, 4 * P, lc), lambda g: (g, 0, 0))],
        out_specs=pl.BlockSpec((None, nc, lc), lambda g: (g, 0, 0)),
        out_shape=jax.ShapeDtypeStruct((G, nc, lc), BF16),
        scratch_shapes=[pltpu.VMEM((lc, lc), BF16)],
        compiler_params=_cparams(("parallel",)),
        name="s5_chunk_out",
    )(v, kk, carried.reshape(nc, G * 4 * P), wout)
    return jnp.transpose(y.reshape(G, nc, L, C), (1, 2, 0, 3)).reshape(n, G * C)


def _s5_post_kernel(y_ref, u_ref, d_ref, w_ref, b_ref, o_ref):
    y1 = _gelu_tanh(y_ref[...].astype(F32) + d_ref[...] * u_ref[...])
    gate = jnp.dot(y1.astype(BF16), w_ref[...], preferred_element_type=F32) + b_ref[...]
    o_ref[...] = (y1 * _sigmoid(gate)).astype(o_ref.dtype)


def _s5_post(y5, ze, s5_d, glu_w, glu_b, layer):
    n, w = y5.shape
    tm = 512
    return pl.pallas_call(
        _s5_post_kernel,
        grid=(n // tm,),
        in_specs=[pl.BlockSpec((tm, w), lambda i: (i, 0)), pl.BlockSpec((tm, w), lambda i: (i, E_U)),
                  pl.BlockSpec((1, w), lambda i: (0, 0)), pl.BlockSpec((None, w, w), lambda i: (layer, 0, 0)),
                  pl.BlockSpec((1, w), lambda i: (0, 0))],
        out_specs=pl.BlockSpec((tm, w), lambda i: (i, 0)),
        out_shape=jax.ShapeDtypeStruct((n, w), BF16),
        compiler_params=_cparams(("parallel",)),
        name="s5_post",
    )(y5, ze, s5_d.reshape(1, w), glu_w, glu_b.reshape(1, w))


def _t5_buckets(rel):
    half = REL_BUCKETS // 2
    max_exact = half // 2
    sign = (rel > 0).astype(np.int32) * half
    n = np.abs(rel)
    large = max_exact + (np.log(np.maximum(n, 1) / max_exact)
                         / np.log(REL_MAX_DIST / max_exact) * (half - max_exact)).astype(np.int32)
    large = np.minimum(large, half - 1)
    return sign + np.where(n < max_exact, n, large)


def _att_bias_tile(rel_bias, group, dil):
    ncol = Q_BLOCK + 2 * HALF_WIN
    hs = slice(group * HEADS_PER_GROUP, (group + 1) * HEADS_PER_GROUP)
    offs = np.arange(-HALF_WIN, HALF_WIN + 1)
    vals = jnp.transpose(rel_bias[:, hs][_t5_buckets(offs * dil)]).astype(F32)
    width = 2 * ncol
    pad_lo = Q_BLOCK
    line = jnp.pad(vals, ((0, 0), (pad_lo, width - pad_lo - vals.shape[1])), constant_values=NEG_INF)
    rows = jnp.broadcast_to(line[:, None, :], (HEADS_PER_GROUP, Q_BLOCK, width)).reshape(HEADS_PER_GROUP, -1)
    skew = rows[:, :Q_BLOCK * (width - 1)].reshape(HEADS_PER_GROUP, Q_BLOCK, width - 1)
    tile = skew[:, :, pad_lo:pad_lo + ncol]
    col = np.arange(ncol)[None, None, :]
    before = col < HALF_WIN
    after = col >= HALF_WIN + Q_BLOCK
    return jnp.stack([tile, jnp.where(before, NEG_INF, tile), jnp.where(after, NEG_INF, tile),
                      jnp.where(before | after, NEG_INF, tile)])


def _att_kernel(q_ref, kp_ref, km_ref, kn_ref, vp_ref, vm_ref, vn_ref, *rest):
    bias_refs = rest[:ATT_STEP_BLOCKS]
    o_ref, lse_ref = rest[ATT_STEP_BLOCKS:]
    ncol = Q_BLOCK + 2 * HALF_WIN
    main_rows = ATT_STEP_BLOCKS * Q_BLOCK
    low = lax.broadcasted_iota(jnp.int32, (Q_BLOCK, LANES), 1) < HEAD_DIM
    ones = jnp.ones((ncol, LANES), BF16)
    zero = jnp.zeros((Q_BLOCK, LANES), BF16)
    heads = range(HEADS_PER_GROUP)
    pair_cols = [slice((h // 2) * LANES, (h // 2 + 1) * LANES) for h in heads]

    def window(prev_ref, main_ref, next_ref, sb):
        lo = sb * Q_BLOCK - HALF_WIN
        parts = []
        if lo < 0:
            parts.append(prev_ref[...])
        m_lo, m_hi = max(lo, 0), min(lo + ncol, main_rows)
        parts.append(main_ref[pl.ds(m_lo, m_hi - m_lo), :])
        if lo + ncol > main_rows:
            parts.append(next_ref[...])
        return jnp.concatenate(parts, axis=0) if len(parts) > 1 else parts[0]

    for sb in range(ATT_STEP_BLOCKS):
        rows = pl.ds(sb * Q_BLOCK, Q_BLOCK)
        q = q_ref[rows, :]
        k = window(kp_ref, km_ref, kn_ref, sb)
        v = window(vp_ref, vm_ref, vn_ref, sb)
        scores = []
        for h in heads:
            q2 = q[:, pair_cols[h]]
            qh = jnp.where(low, q2, zero) if h % 2 == 0 else jnp.where(low, zero, q2)
            s = lax.dot_general(qh, k[:, pair_cols[h]], (((1,), (1,)), ((), ())), preferred_element_type=F32)
            scores.append(s + bias_refs[sb][h])
        maxes = [jnp.max(s, axis=-1, keepdims=True) for s in scores]
        probs = [jnp.exp(s - m).astype(BF16) for s, m in zip(scores, maxes)]
        outs = [jnp.dot(p, v[:, pair_cols[h]], preferred_element_type=F32) for h, p in zip(heads, probs)]
        sums = [jnp.dot(p, ones, preferred_element_type=F32) for p in probs]
        for pair in range(HEADS_PER_GROUP // 2):
            a, b = 2 * pair, 2 * pair + 1
            l = jnp.where(low, sums[a], sums[b])
            o_ref[rows, pair_cols[a]] = (jnp.where(low, outs[a], outs[b]) / l).astype(o_ref.dtype)
            lse_ref[rows, pair_cols[a]] = jnp.where(low, maxes[a], maxes[b]) + jnp.log(l)


def _attention_group(qkv, bias_tile, dil, seg_lens):
    nd = qkv.shape[1]
    step_rows = ATT_STEP_BLOCKS * Q_BLOCK
    halves_per_step = step_rows // HALF_WIN
    assert nd % step_rows == 0
    nhalf = nd // HALF_WIN
    starts = np.cumsum((0,) + tuple(seg_lens)) // (dil * Q_BLOCK)
    blk_first = tuple(int(s) for s in starts[:-1])
    blk_last = tuple(int(s) - 1 for s in starts[1:])
    ncol = Q_BLOCK + 2 * HALF_WIN

    def variant(blk):
        return _any_eq(blk, blk_first).astype(jnp.int32) + 2 * _any_eq(blk, blk_last).astype(jnp.int32)

    def main(cblk):
        return pl.BlockSpec((None, step_rows, ATT_OUT), lambda r, b: (r, b, cblk))

    def prev(cblk):
        return pl.BlockSpec((None, HALF_WIN, ATT_OUT),
                            lambda r, b: (r, jnp.maximum(halves_per_step * b - 1, 0), cblk))

    def nxt(cblk):
        return pl.BlockSpec((None, HALF_WIN, ATT_OUT),
                            lambda r, b: (r, jnp.minimum(halves_per_step * (b + 1), nhalf - 1), cblk))

    bias_specs = [pl.BlockSpec((None, HEADS_PER_GROUP, Q_BLOCK, ncol),
                               lambda r, b, sb=sb: (variant(ATT_STEP_BLOCKS * b + sb), 0, 0, 0))
                  for sb in range(ATT_STEP_BLOCKS)]
    out_spec = pl.BlockSpec((None, step_rows, ATT_OUT), lambda r, b: (r, b, 0))
    return pl.pallas_call(
        _att_kernel,
        grid=(dil, nd // step_rows),
        in_specs=[main(0), prev(1), main(1), nxt(1), prev(2), main(2), nxt(2)] + bias_specs,
        out_specs=[out_spec, out_spec],
        out_shape=[jax.ShapeDtypeStruct((dil, nd, ATT_OUT), BF16), jax.ShapeDtypeStruct((dil, nd, ATT_OUT), F32)],
        compiler_params=_cparams(("parallel", "parallel")),
        name=f"attention_d{dil}",
    )(qkv, qkv, qkv, qkv, qkv, qkv, qkv, *([bias_tile] * ATT_STEP_BLOCKS))


def _merge_kernel(yl_ref, ys_ref, o0_ref, l0_ref, o1_ref, l1_ref, o2_ref, l2_ref,
                  ga_ref, gb_ref, gc_ref, wl_ref, ws_ref, wa_ref, m_ref, o1_scr, l1_scr, o2_scr, l2_scr, *, tm):
    ntile = ATT_OUT // LANES

    def sequence_order(src, dst, dil):
        rows = tm // dil
        for r in range(dil):
            blk = src[r].astype(F32)
            for c in range(ntile):
                dst[c, pl.ds(r, rows, stride=dil), :] = blk[:, c * LANES:(c + 1) * LANES]
        return jnp.concatenate([dst[c] for c in range(ntile)], axis=-1)

    o1 = sequence_order(o1_ref, o1_scr, ATT_GROUPS[1][1])
    l1 = sequence_order(l1_ref, l1_scr, ATT_GROUPS[1][1])
    o2 = sequence_order(o2_ref, o2_scr, ATT_GROUPS[2][1])
    l2 = sequence_order(l2_ref, l2_scr, ATT_GROUPS[2][1])
    l0 = l0_ref[...]
    mx = jnp.maximum(jnp.maximum(l0, l1), l2)
    e0, e1, e2 = jnp.exp(l0 - mx), jnp.exp(l1 - mx), jnp.exp(l2 - mx)
    yatt = ((o0_ref[...].astype(F32) * e0 + o1 * e1 + o2 * e2) / (e0 + e1 + e2)).astype(BF16)

    def gate(ref):
        return _sigmoid(ref[...].astype(F32))

    m = (gate(ga_ref) * jnp.dot(yl_ref[...], wl_ref[...], preferred_element_type=F32)
         + gate(gb_ref) * jnp.dot(ys_ref[...], ws_ref[...], preferred_element_type=F32)
         + gate(gc_ref) * jnp.dot(yatt, wa_ref[...], preferred_element_type=F32))
    m_ref[...] = m.astype(m_ref.dtype)


def _merge(ylru, gates, ys5, att, w_br_lru, w_br_s5, w_br_att, layer):
    n = gates.shape[0]
    d = D_MODEL
    tm = 256
    wl, wa = LRU_WIDTH, ATT_OUT
    gspec = [pl.BlockSpec((tm, d), lambda i, c=c: (i, c)) for c in range(3)]
    att_specs, att_args = [], []
    for (_, dil), (o, l) in zip(ATT_GROUPS, att):
        blk = (None, tm, wa) if dil == 1 else (dil, tm // dil, wa)
        att_specs += [pl.BlockSpec(blk, lambda i: (0, i, 0))] * 2
        att_args += [o, l]
    return pl.pallas_call(
        functools.partial(_merge_kernel, tm=tm),
        grid=(n // tm,),
        in_specs=[pl.BlockSpec((tm, wl), lambda i: (i, 0)), pl.BlockSpec((tm, S5_WIDTH), lambda i: (i, 0))]
        + att_specs + gspec
        + [_resident((None, wl, d), lambda i: (layer, 0, 0)), _resident((None, S5_WIDTH, d), lambda i: (layer, 0, 0)),
           _resident((None, wa, d), lambda i: (layer, 0, 0))],
        out_specs=pl.BlockSpec((tm, d), lambda i: (i, 0)),
        out_shape=jax.ShapeDtypeStruct((n, d), BF16),
        scratch_shapes=[pltpu.VMEM((wa // LANES, tm, LANES), F32)] * 4,
        compiler_params=_cparams(("parallel",)),
        name="merge",
    )(ylru, ys5, *att_args, gates, gates, gates, w_br_lru, w_br_s5, w_br_att)


def _out_proj_kernel(x_ref, m_ref, w_ref, g_ref, o_ref):
    mix = jnp.dot(m_ref[...], w_ref[...], preferred_element_type=F32)
    o_ref[...] = x_ref[...] + _rms(mix, g_ref[...])


def _out_proj(x, m, w_out, g, layer):
    n, d = x.shape
    tm = 512
    return pl.pallas_call(
        _out_proj_kernel,
        grid=(n // tm,),
        in_specs=[pl.BlockSpec((tm, d), lambda i: (i, 0)), pl.BlockSpec((tm, d), lambda i: (i, 0)),
                  _resident((None, d, d), lambda i: (layer, 0, 0)), pl.BlockSpec((1, d), lambda i: (0, 0))],
        out_specs=pl.BlockSpec((tm, d), lambda i: (i, 0)),
        out_shape=jax.ShapeDtypeStruct((n, d), F32),
        compiler_params=_cparams(("parallel",)),
        name="mixer_out",
    )(x, m, w_out, g.reshape(1, d))


def _mixer(x, g_pre, g_post, seg_lens, layer, w_in, conv_w, conv_b, lru_wa, lru_ba, lru_wx, lru_bx, lru_L,
           lam_re, lam_im, log_dt, b_re, b_im, c_re, c_im, s5_d, glu_w, glu_b,
           w_br_lru, w_br_s5, w_br_att, w_out, bias_tiles):
    gates, ze = _proj_elementwise(x, g_pre, w_in, layer)
    qkv = _proj_qkv(x, g_pre, w_in, layer)

    wg, bg = _lru_gate_weights(lru_wa, lru_ba, lru_wx, lru_bx)
    ylru = _lru(ze, conv_w, conv_b, wg, bg, lru_L.reshape(2, 1, LRU_WIDTH), seg_lens)

    tables = _s5_prepare(lam_re, lam_im, log_dt, b_re, b_im, c_re, c_im)
    y5 = _s5(ze[:, E_U * S5_WIDTH:(E_U + 1) * S5_WIDTH], tables, seg_lens)
    ys5 = _s5_post(y5, ze, s5_d, glu_w, glu_b, layer)

    att = [_attention_group(qkv[g], bias_tiles[g], dil, seg_lens) for g, (_, dil) in enumerate(ATT_GROUPS)]

    m = _merge(ylru, gates, ys5, att, w_br_lru, w_br_s5, w_br_att, layer)
    return _out_proj(x, m, w_out, g_post, layer)


def kernel(x_prompt, x_sample, norm_g, w_in, lru_conv_w, lru_conv_b, lru_wa, lru_ba, lru_wx, lru_bx, lru_L,
           s5_lam_re, s5_lam_im, s5_log_dt, s5_b_re, s5_b_im, s5_c_re, s5_c_im, s5_d, s5_glu_w, s5_glu_b,
           rel_bias, w_br_lru, w_br_s5, w_br_att, w_out, ffn_w1, ffn_w3, ffn_w2):
    bp, tp, d = x_prompt.shape
    bs, ts, _ = x_sample.shape
    seg_lens = (tp,) * bp + (ts,) * bs
    rows = (bp * tp, bs * ts)
    n = sum(rows)
    depth = norm_g.shape[0]
    w1, w3, w2 = ffn_w1.astype(BF16), ffn_w3.astype(BF16), ffn_w2.astype(BF16)
    w_in_b, glu_b16, w_out_b = w_in.astype(BF16), s5_glu_w.astype(BF16), w_out.astype(BF16)
    wbl, wbs, wba = w_br_lru.astype(BF16), w_br_s5.astype(BF16), w_br_att.astype(BF16)
    bias_tiles = [_att_bias_tile(rel_bias, g, dil) for g, (_, dil) in enumerate(ATT_GROUPS)]

    xs = [x_prompt.reshape(rows[0], d), x_sample.reshape(rows[1], d)]
    for l in range(depth):
        g = norm_g[l]
        (x,) = _ffn(xs, g[0], g[1], w1, w3, w2, l, 0, (n,))
        x = _mixer(x, g[2], g[3], seg_lens, l, w_in_b, lru_conv_w[l], lru_conv_b[l], lru_wa[l], lru_ba[l],
                   lru_wx[l], lru_bx[l], lru_L[l], s5_lam_re[l], s5_lam_im[l], s5_log_dt[l], s5_b_re[l],
                   s5_b_im[l], s5_c_re[l], s5_c_im[l], s5_d[l], glu_b16, s5_glu_b[l], wbl, wbs, wba, w_out_b,
                   bias_tiles)
        xs = _ffn([x], g[4], g[5], w1, w3, w2, l, 1, rows if l == depth - 1 else (n,))
    return (xs[0].reshape(bp, tp, d), xs[1].reshape(bs, ts, d))
```

```python
import functools
import math

import numpy as np
import jax
import jax.numpy as jnp
from jax import lax
from jax.experimental import pallas as pl
from jax.experimental.pallas import tpu as pltpu

F32 = jnp.float32
BF16 = jnp.bfloat16

D_MODEL = 2048
LRU_WIDTH = 1024
LRU_BLOCKS = 16
LRU_CONV_W = 4
LRU_C = 8.0
S5_WIDTH = 1024
S5_GROUP_CH = 16
S5_GROUPS = 64
S5_STATE = 64
HEAD_DIM = 64
ATT_GROUPS = ((128, 1), (512, 4), (2048, 16))
ATT_WIDTH = 1536
HEADS_PER_GROUP = 8
ATT_OUT = 512
REL_BUCKETS = 32
REL_MAX_DIST = 1024
RMS_EPS = 1e-6
NEG_INF = -1e30

Q_BLOCK = 128
HALF_WIN = 64
ATT_STEP_BLOCKS = 4
S5_CHUNK = 64
LANES = 128
SUBLANES = 8
COL_QKV = 3 * 1024
E_TILE = 1536
E_XL, E_GL, E_U = 0, 1, 2
VMEM_LIMIT = 48 * 1024 * 1024


def _cparams(sem, vmem_limit=VMEM_LIMIT):
    return pltpu.CompilerParams(dimension_semantics=sem, vmem_limit_bytes=vmem_limit)


def _rms(v, g):
    width = v.shape[-1]
    sq = v * v
    part = sq[:, :LANES]
    for c in range(1, width // LANES):
        part = part + sq[:, c * LANES:(c + 1) * LANES]
    ms = jnp.sum(part, axis=-1, keepdims=True) * (1.0 / width)
    return v * lax.rsqrt(ms + RMS_EPS) * g


ROW_BLOCK = 16


def _for_row_blocks(n_rows, body):
    def step(t, carry):
        body(pl.ds(pl.multiple_of(t * ROW_BLOCK, ROW_BLOCK), ROW_BLOCK))
        return carry

    lax.fori_loop(0, n_rows // ROW_BLOCK, step, 0, unroll=16)


def _norm_rows_to(h_scr, x_ref, g_ref):
    g = g_ref[...]

    def body(rows):
        h_scr[rows, :] = _rms(x_ref[rows, :], g).astype(h_scr.dtype)

    _for_row_blocks(x_ref.shape[0], body)


def _gelu_tanh(v):
    return 0.5 * v * (1.0 + jnp.tanh(math.sqrt(2.0 / math.pi) * (v + 0.044715 * (v * v * v))))


def _sigmoid(v):
    return 0.5 * jnp.tanh(0.5 * v) + 0.5


def _any_eq(idx, values):
    hit = idx == values[0]
    for v in values[1:]:
        hit = jnp.logical_or(hit, idx == v)
    return hit


def _resident(shape, index_map):
    return pl.BlockSpec(shape, index_map, pipeline_mode=pl.Buffered(1))


def _ffn_kernel(*refs, nj, in_tiles, out_tiles):
    n_in, n_out = len(in_tiles), len(out_tiles)
    x_refs = refs[:n_in]
    gpre_ref, gpost_ref, w1_ref, w3_ref, w2_ref = refs[n_in:n_in + 5]
    o_refs = refs[n_in + 5:n_in + 5 + n_out]
    h_scr, acc_scr = refs[n_in + 5 + n_out:]
    i = pl.program_id(0)
    j = pl.program_id(1)

    def active(bounds, k):
        lo = sum(bounds[:k])
        return jnp.logical_and(i >= lo, i < lo + bounds[k])

    for k in range(n_in):
        @pl.when(jnp.logical_and(j == 0, active(in_tiles, k)))
        def _(k=k):
            _norm_rows_to(h_scr, x_refs[k], gpre_ref)
            acc_scr[...] = jnp.zeros_like(acc_scr)

    h = h_scr[...]
    a = jnp.dot(h, w1_ref[...], preferred_element_type=F32)
    b = jnp.dot(h, w3_ref[...], preferred_element_type=F32)
    g = (a * _sigmoid(a) * b).astype(BF16)
    acc_scr[...] += jnp.dot(g, w2_ref[...], preferred_element_type=F32)

    for ki in range(n_in):
        for ko in range(n_out):
            @pl.when(jnp.logical_and(j == nj - 1, jnp.logical_and(active(in_tiles, ki), active(out_tiles, ko))))
            def _(ki=ki, ko=ko):
                g_half = 0.5 * gpost_ref[...]

                def finish(rows):
                    o_refs[ko][rows, :] = x_refs[ki][rows, :] + _rms(acc_scr[rows, :], g_half)

                _for_row_blocks(acc_scr.shape[0], finish)


def _ffn(xs, g_pre, g_post, w1, w3, w2, layer, which, out_rows):
    d = xs[0].shape[1]
    dff = w1.shape[-1]
    tm, tf = 512, 512
    nj = dff // tf
    in_tiles = tuple(x.shape[0] // tm for x in xs)
    out_tiles = tuple(r // tm for r in out_rows)

    def piece(bounds, k):
        lo = sum(bounds[:k])
        return lambda i, j: (jnp.clip(i - lo, 0, bounds[k] - 1), 0)

    const = lambda i, j: (0, 0)
    wcol = pl.BlockSpec((None, None, d, tf), lambda i, j: (layer, which, 0, j))
    outs = pl.pallas_call(
        functools.partial(_ffn_kernel, nj=nj, in_tiles=in_tiles, out_tiles=out_tiles),
        grid=(sum(in_tiles), nj),
        in_specs=[pl.BlockSpec((tm, d), piece(in_tiles, k)) for k in range(len(xs))]
        + [pl.BlockSpec((1, d), const), pl.BlockSpec((1, d), const), wcol, wcol,
           pl.BlockSpec((None, None, tf, d), lambda i, j: (layer, which, j, 0))],
        out_specs=[pl.BlockSpec((tm, d), piece(out_tiles, k)) for k in range(len(out_rows))],
        out_shape=[jax.ShapeDtypeStruct((r, d), F32) for r in out_rows],
        scratch_shapes=[pltpu.VMEM((tm, d), BF16), pltpu.VMEM((tm, d), F32)],
        compiler_params=_cparams(("parallel", "arbitrary"),
                                 VMEM_LIMIT + (len(xs) + len(out_rows) - 2) * 2 * tm * d * 4),
        name="ffn",
    )(*xs, g_pre.reshape(1, d), g_post.reshape(1, d), w1, w3, w2)
    return list(outs)


def _norm_mm_kernel(x_ref, g_ref, w_ref, gates_ref, e_ref, h_scr, *, ngate):
    j = pl.program_id(1)

    @pl.when(j == 0)
    def _():
        _norm_rows_to(h_scr, x_ref, g_ref)

    z = jnp.dot(h_scr[...], w_ref[...], preferred_element_type=F32)

    @pl.when(j < ngate)
    def _():
        gates_ref[...] = z.astype(gates_ref.dtype)

    @pl.when(j >= ngate)
    def _():
        e_ref[...] = z


def _proj_elementwise(x, g, w_in, layer):
    n, d = x.shape
    tm, tn = 512, E_TILE
    lead = COL_QKV // tn
    skip = 3 * ATT_WIDTH // tn
    ncol = (w_in.shape[-1] - 3 * ATT_WIDTH) // tn
    ngate = ncol - lead
    return pl.pallas_call(
        functools.partial(_norm_mm_kernel, ngate=ngate),
        grid=(n // tm, ncol),
        in_specs=[pl.BlockSpec((tm, d), lambda i, j: (i, 0)), pl.BlockSpec((1, d), lambda i, j: (0, 0)),
                  pl.BlockSpec((None, d, tn),
                               lambda i, j: (layer, 0, jnp.where(j < ngate, j + lead + skip, j - ngate)))],
        out_specs=[pl.BlockSpec((tm, tn), lambda i, j: (i, jnp.minimum(j, ngate - 1))),
                   pl.BlockSpec((tm, tn), lambda i, j: (i, jnp.maximum(j - ngate, 0)))],
        out_shape=[jax.ShapeDtypeStruct((n, ngate * tn), BF16), jax.ShapeDtypeStruct((n, lead * tn), F32)],
        scratch_shapes=[pltpu.VMEM((tm, d), BF16)],
        compiler_params=_cparams(("parallel", "arbitrary")),
        name="mixer_in",
    )(x, g.reshape(1, d), w_in)


def _qkv_kernel(x_ref, g_ref, wq_ref, wk_ref, wv_ref, o0_ref, o1_ref, o2_ref, h_scr, res_scr, *, tm):
    step = pl.program_id(1)
    o_refs = (o0_ref, o1_ref, o2_ref)
    order = tuple(reversed(range(len(ATT_GROUPS))))
    ntile = 3 * ATT_OUT // LANES

    @pl.when(step == 0)
    def _():
        _norm_rows_to(h_scr, x_ref, g_ref)

    def project():
        h = h_scr[...]
        q = jnp.dot(h, wq_ref[...], preferred_element_type=F32) * (HEAD_DIM ** -0.5)
        k = jnp.dot(h, wk_ref[...], preferred_element_type=F32)
        v = jnp.dot(h, wv_ref[...], preferred_element_type=F32)
        return jnp.concatenate([q, k, v], axis=-1)

    def write_residue_major(slot, gi):
        dil = ATT_GROUPS[gi][1]
        rows = tm // dil
        for r in range(dil):
            o_refs[gi][r] = jnp.concatenate(
                [res_scr[slot, c, pl.ds(r, rows, stride=dil), :] for c in range(ntile)], axis=-1).astype(BF16)

    for s, gi in enumerate(order):
        @pl.when(step == s)
        def _(s=s, gi=gi):
            res = project()
            if s > 0 and ATT_GROUPS[order[s - 1]][1] > 1:
                write_residue_major((s - 1) % 2, order[s - 1])
            if ATT_GROUPS[gi][1] == 1:
                o_refs[gi][0] = res.astype(BF16)
            else:
                for c in range(ntile):
                    res_scr[s % 2, c] = res[:, c * LANES:(c + 1) * LANES]


def _proj_qkv(x, g, w_in, layer):
    n, d = x.shape
    tm = 512
    qb = COL_QKV // ATT_OUT

    ng = len(ATT_GROUPS)
    assert ATT_GROUPS[0][1] == 1

    def wspec(off):
        return pl.BlockSpec((None, d, ATT_OUT), lambda i, s: (layer, 0, qb + off + ng - 1 - s))

    return pl.pallas_call(
        functools.partial(_qkv_kernel, tm=tm),
        grid=(n // tm, ng),
        in_specs=[pl.BlockSpec((tm, d), lambda i, gq: (i, 0)), pl.BlockSpec((1, d), lambda i, gq: (0, 0)),
                  wspec(0), wspec(ng), wspec(2 * ng)],
        out_specs=[pl.BlockSpec((dil, tm // dil, 3 * ATT_OUT), lambda i, gq: (0, i, 0)) for _, dil in ATT_GROUPS],
        out_shape=[jax.ShapeDtypeStruct((dil, n // dil, 3 * ATT_OUT), BF16) for _, dil in ATT_GROUPS],
        scratch_shapes=[pltpu.VMEM((tm, d), BF16), pltpu.VMEM((2, 3 * ATT_OUT // LANES, tm, LANES), F32)],
        compiler_params=_cparams(("parallel", "arbitrary")),
        name="mixer_qkv",
    )(x, g.reshape(1, d), w_in, w_in, w_in)


def _lru_kernel(*refs, tc, nt, seg_first, seg_last, d):
    xm_ref, xp_ref, xn_ref, cw_ref, cb_ref, wg_ref, bg_ref, lam_ref = refs[:8]
    if d == 0:
        o_ref, xpad_scr, a_scr, b_scr, hl_scr, p_scr, h_scr = refs[8:]
    else:
        hf_ref, gl_ref, o_ref, xpad_scr, a_scr, b_scr, hl_scr, p_scr, h_scr = refs[8:]
    i = pl.program_id(0)
    ti = i if d == 0 else nt - 1 - i
    w = LRU_WIDTH
    at_first = _any_eq(ti, seg_first)
    at_last = _any_eq(ti, seg_last)

    xpad_scr[pl.ds(0, SUBLANES), :] = xp_ref[...] * jnp.where(at_first, 0.0, 1.0)
    xpad_scr[pl.ds(SUBLANES, tc), :] = xm_ref[...]
    xpad_scr[pl.ds(SUBLANES + tc, SUBLANES), :] = xn_ref[...] * jnp.where(at_last, 0.0, 1.0)
    left = LRU_CONV_W // 2
    xc = cb_ref[...] + xpad_scr[pl.ds(SUBLANES - left, tc), :] * cw_ref[pl.ds(0, 1), :]
    for kk in range(1, LRU_CONV_W):
        xc = xc + xpad_scr[pl.ds(SUBLANES - left + kk, tc), :] * cw_ref[pl.ds(kk, 1), :]

    xcb = xc.astype(BF16)
    parts = [jnp.dot(xcb[:, p * LANES:(p + 1) * LANES], wg_ref[p], preferred_element_type=F32)
             for p in range(w // LANES)]
    r = _sigmoid(jnp.concatenate([g[:, :LANES] for g in parts], axis=-1) + bg_ref[:, :w])
    ig = _sigmoid(jnp.concatenate([g[:, LANES:] for g in parts], axis=-1) + bg_ref[:, w:])
    nlam = -lam_ref[...]
    softplus = jnp.maximum(nlam, 0.0) + jnp.log(1.0 + jnp.exp(-jnp.abs(nlam)))
    a = jnp.exp(-LRU_C * r * softplus)
    bb = jnp.sqrt(1.0 - a * a) * (ig * xc)

    sub = tc // SUBLANES
    pitch = sub + SUBLANES
    ntile = w // LANES
    for c in range(ntile):
        for jj in range(SUBLANES):
            a_scr[c, pl.ds(jj * pitch, sub), :] = a[jj * sub:(jj + 1) * sub, c * LANES:(c + 1) * LANES]
            b_scr[c, pl.ds(jj * pitch, sub), :] = bb[jj * sub:(jj + 1) * sub, c * LANES:(c + 1) * LANES]

    def strided_rows(ref, k):
        return jnp.concatenate([ref[c, pl.ds(k, SUBLANES, stride=pitch), :] for c in range(ntile)], axis=-1)

    @pl.when(at_first if d == 0 else at_last)
    def _():
        h_scr[...] = jnp.zeros_like(h_scr)

    sub_id = lax.broadcasted_iota(jnp.int32, (SUBLANES, w), 0)
    hl = jnp.zeros((SUBLANES, w), F32)
    pp = jnp.ones((SUBLANES, w), F32)
    for k in (range(sub) if d == 0 else range(sub - 1, -1, -1)):
        av = strided_rows(a_scr, k)
        bv = strided_rows(b_scr, k)
        hl = av * hl + bv
        pp = av * pp
        hl_scr[pl.ds(k * SUBLANES, SUBLANES), :] = hl
        p_scr[pl.ds(k * SUBLANES, SUBLANES), :] = pp
    cur = h_scr[pl.ds(0, 1), :]
    carry = jnp.zeros((SUBLANES, w), F32)
    for jj in (range(SUBLANES) if d == 0 else range(SUBLANES - 1, -1, -1)):
        carry = jnp.where(sub_id == jj, cur, carry)
        cur = hl[jj:jj + 1, :] + pp[jj:jj + 1, :] * cur
    h_scr[pl.ds(0, 1), :] = cur
    for k in range(sub):
        rows = pl.ds(k * SUBLANES, SUBLANES)
        hv = hl_scr[rows, :] + p_scr[rows, :] * carry
        for c in range(ntile):
            a_scr[c, pl.ds(k, SUBLANES, stride=pitch), :] = hv[:, c * LANES:(c + 1) * LANES]
    for jj in range(SUBLANES):
        rows = pl.ds(jj * sub, sub)
        hv = jnp.concatenate([a_scr[c, pl.ds(jj * pitch, sub), :] for c in range(ntile)], axis=-1)
        if d == 0:
            o_ref[rows, :] = hv
        else:
            o_ref[rows, :] = ((hf_ref[rows, :] + hv) * _gelu_tanh(gl_ref[rows, :])).astype(o_ref.dtype)


def _lru(ze, conv_w, conv_b, wg, bg, lam, seg_lens):
    n = ze.shape[0]
    w = LRU_WIDTH
    tc = 256
    nt = n // tc
    starts = np.cumsum((0,) + tuple(seg_lens))
    seg_first = tuple(int(s) // tc for s in starts[:-1])
    seg_last = tuple(int(s) // tc - 1 for s in starts[1:])
    hb = tc // SUBLANES
    nhb = n // SUBLANES
    pitched = SUBLANES * (tc // SUBLANES + SUBLANES)

    def one_direction(d, extra_specs, extra_args, out_dtype):
        tile = (lambda i: i) if d == 0 else (lambda i: nt - 1 - i)
        return pl.pallas_call(
            functools.partial(_lru_kernel, tc=tc, nt=nt, seg_first=seg_first, seg_last=seg_last, d=d),
            grid=(nt,),
            in_specs=[pl.BlockSpec((tc, w), lambda i: (tile(i), E_XL)),
                      pl.BlockSpec((SUBLANES, w), lambda i: (jnp.maximum(tile(i) * hb - 1, 0), E_XL)),
                      pl.BlockSpec((SUBLANES, w), lambda i: (jnp.minimum((tile(i) + 1) * hb, nhb - 1), E_XL)),
                      pl.BlockSpec((LRU_CONV_W, w), lambda i: (0, 0)),
                      pl.BlockSpec((1, w), lambda i: (0, 0)),
                      _resident((None, w // LANES, LANES, 2 * LANES), lambda i: (d, 0, 0, 0)),
                      pl.BlockSpec((None, 1, 2 * w), lambda i: (d, 0, 0)),
                      pl.BlockSpec((None, 1, w), lambda i: (d, 0, 0))] + [spec(tile) for spec in extra_specs],
            out_specs=pl.BlockSpec((tc, w), lambda i: (tile(i), 0)),
            out_shape=jax.ShapeDtypeStruct((n, w), out_dtype),
            scratch_shapes=[pltpu.VMEM((tc + 2 * SUBLANES, w), F32), pltpu.VMEM((w // LANES, pitched, LANES), F32),
                            pltpu.VMEM((w // LANES, pitched, LANES), F32), pltpu.VMEM((tc, w), F32),
                            pltpu.VMEM((tc, w), F32), pltpu.VMEM((SUBLANES, w), F32)],
            compiler_params=_cparams(("arbitrary",)),
            name="rglru_fwd" if d == 0 else "rglru_bwd",
        )(ze, ze, ze, conv_w, conv_b.reshape(1, w), wg, bg, lam, *extra_args)

    hf = one_direction(0, [], [], F32)
    return one_direction(1, [lambda tile: pl.BlockSpec((tc, w), lambda i: (tile(i), 0)),
                             lambda tile: pl.BlockSpec((tc, w), lambda i: (tile(i), E_GL))], [hf, ze], BF16)


def _lru_gate_weights(wa, ba, wx, bx):
    per_tile = LANES * LRU_BLOCKS // LRU_WIDTH

    def tiles(wb):
        bw = wb.shape[-1]
        eye = jnp.eye(per_tile, dtype=wb.dtype)
        grouped = wb.reshape(2, LRU_BLOCKS // per_tile, per_tile, bw, bw)
        full = grouped[:, :, :, :, None, :] * eye[None, None, :, None, :, None]
        return full.reshape(2, LRU_BLOCKS // per_tile, LANES, LANES)

    wg = jnp.concatenate([tiles(wa), tiles(wx)], axis=-1).astype(BF16)
    bg = jnp.concatenate([ba, bx], axis=-1).reshape(2, 1, 2 * LRU_WIDTH)
    return wg, bg


def _s5_ktable_kernel(b_ref, w_ref, o_ref):
    o_ref[...] = jnp.dot(b_ref[...], w_ref[...], preferred_element_type=F32, precision=lax.Precision.HIGHEST)


def _s5_tables(lam_re, lam_im, log_dt, b_re, b_im, c_re, c_im):
    L, G, P, C = S5_CHUNK, S5_GROUPS, S5_STATE, S5_GROUP_CH
    dt = jnp.exp(log_dt)[..., None]
    mag = jnp.exp(lam_re * dt)
    ar = mag * jnp.cos(lam_im * dt)
    ai = mag * jnp.sin(lam_im * dt)
    den = lam_re * lam_re + lam_im * lam_im
    cr = ((ar - 1.0) * lam_re + ai * lam_im) / den
    ci = (ai * lam_re - (ar - 1.0) * lam_im) / den
    bbr = cr[..., None] * b_re - ci[..., None] * b_im
    bbi = cr[..., None] * b_im + ci[..., None] * b_re
    pr, pi = jnp.ones_like(ar)[None], jnp.zeros_like(ai)[None]
    nr, ni = ar, ai
    while pr.shape[0] < L + 1:
        pr, pi = (jnp.concatenate([pr, pr * nr - pi * ni], axis=0),
                  jnp.concatenate([pi, pr * ni + pi * nr], axis=0))
        nr, ni = nr * nr - ni * ni, 2.0 * nr * ni
    pr, pi = pr[:L + 1], pi[:L + 1]
    zr = pr[:L, ..., None] * bbr - pi[:L, ..., None] * bbi
    zi = pr[:L, ..., None] * bbi + pi[:L, ..., None] * bbr

    def c_pow(powers_r, powers_i):
        ctr = jnp.transpose(c_re, (0, 1, 3, 2))[:, :, :, None, :]
        cti = jnp.transpose(c_im, (0, 1, 3, 2))[:, :, :, None, :]
        qr = jnp.transpose(powers_r, (1, 2, 3, 0))[..., None]
        qi = jnp.transpose(powers_i, (1, 2, 3, 0))[..., None]
        return ctr * qr - cti * qi, ctr * qi + cti * qr

    wr, wi = c_pow(pr[:L], pi[:L])
    wmat = jnp.concatenate([wr, wi], axis=2).reshape(2 * G, 2 * P, L * C)
    bmat = jnp.concatenate([jnp.transpose(bbr, (0, 1, 3, 2)), -jnp.transpose(bbi, (0, 1, 3, 2))],
                           axis=-1).reshape(2 * G, C, 2 * P)

    def w_in(z, flip):
        zf = z[::-1] if flip else z
        return jnp.transpose(zf, (1, 0, 3, 2)).reshape(G, L * C, P)

    win = jnp.concatenate([w_in(zr[:, 0], True), w_in(zr[:, 1], False),
                           w_in(zi[:, 0], True), w_in(zi[:, 1], False)], axis=-1).astype(BF16)

    fr, fi = c_pow(pr[1:], pi[1:])
    br, bi = c_pow(pr[:0:-1], pi[:0:-1])
    wout = jnp.concatenate([fr[0], br[1], -fi[0], -bi[1]], axis=1).reshape(G, 4 * P, L * C).astype(BF16)
    al = jnp.concatenate([pr[L, 0], pr[L, 1], pi[L, 0], pi[L, 1]], axis=-1)
    return bmat, wmat, win, wout, al


def _s5_lag_table(kt):
    L, G, C = S5_CHUNK, S5_GROUPS, S5_GROUP_CH
    kk = jnp.concatenate([kt[1, :, :, :0:-1], kt[0, :, :, :1] + kt[1, :, :, :1], kt[0, :, :, 1:]], axis=2)
    kk = kk.reshape(G, C, (2 * L - 1) * C)
    return jnp.pad(kk, ((0, 0), (0, 0), (0, 2 * L * C - kk.shape[-1])))


def _s5_prepare(lam_re, lam_im, log_dt, b_re, b_im, c_re, c_im):
    L, G, P, C = S5_CHUNK, S5_GROUPS, S5_STATE, S5_GROUP_CH
    depth = lam_re.shape[0]
    bmat, wmat, win, wout, al = jax.vmap(_s5_tables)(lam_re, lam_im, log_dt, b_re, b_im, c_re, c_im)
    nb = depth * 2 * G
    kt = pl.pallas_call(
        _s5_ktable_kernel,
        grid=(nb,),
        in_specs=[pl.BlockSpec((None, C, 2 * P), lambda g: (g, 0, 0)),
                  pl.BlockSpec((None, 2 * P, L * C), lambda g: (g, 0, 0))],
        out_specs=pl.BlockSpec((None, C, L * C), lambda g: (g, 0, 0)),
        out_shape=jax.ShapeDtypeStruct((nb, C, L * C), F32),
        compiler_params=_cparams(("parallel",)),
        name="s5_ktable",
    )(bmat.reshape(nb, C, 2 * P), wmat.reshape(nb, 2 * P, L * C))
    kk = jax.vmap(_s5_lag_table)(kt.reshape(depth, 2, G, C, L, C))
    return kk, win, wout, al


def _s5_state_kernel(v_ref, win_ref, o_ref):
    o_ref[...] = jnp.dot(v_ref[...], win_ref[...], preferred_element_type=F32)


def _s5_scan_kernel(s_ref, al_ref, o_ref, *, seg_chunks):
    p2 = 2 * S5_STATE
    alr = al_ref[:, :p2]
    ali = al_ref[:, p2:]
    is_fwd = lax.broadcasted_iota(jnp.int32, alr.shape, 1) < S5_STATE
    zero = jnp.zeros_like(alr)
    start = 0
    for n_chunks in seg_chunks:
        def fwd(k, carry, start=start):
            xr, xi = carry
            c = start + k
            o_ref[c, :, :p2] = xr
            o_ref[c, :, p2:] = xi
            sr = s_ref[c, :, :p2]
            si = s_ref[c, :, p2:]
            return alr * xr - ali * xi + sr, alr * xi + ali * xr + si

        lax.fori_loop(0, n_chunks, fwd, (zero, zero))

        def bwd(k, carry, start=start, n_chunks=n_chunks):
            xr, xi = carry
            c = start + n_chunks - 1 - k
            o_ref[c, :, :p2] = jnp.where(is_fwd, o_ref[c, :, :p2], xr)
            o_ref[c, :, p2:] = jnp.where(is_fwd, o_ref[c, :, p2:], xi)
            sr = s_ref[c, :, :p2]
            si = s_ref[c, :, p2:]
            return alr * xr - ali * xi + sr, alr * xi + ali * xr + si

        lax.fori_loop(0, n_chunks, bwd, (zero, zero))
        start += n_chunks


def _s5_out_kernel(v_ref, kk_ref, x_ref, wout_ref, o_ref, mt_scr):
    L, C = S5_CHUNK, S5_GROUP_CH
    kk = kk_ref[...]
    width = kk.shape[-1]
    per_tile = LANES // C
    for rot in range(per_tile):
        shifted = kk if rot == 0 else pltpu.roll(kk, width - rot * C, axis=1)
        shifted = shifted.astype(BF16)
        for s in range(L):
            lag0 = L - 1 - s
            if lag0 % per_tile == rot:
                col = (lag0 // per_tile) * LANES
                mt_scr[pl.ds(s * C, C), :] = shifted[:, col:col + L * C]
    o_ref[...] = (jnp.dot(v_ref[...], mt_scr[...], preferred_element_type=F32)
                  + jnp.dot(x_ref[...].astype(BF16), wout_ref[...], preferred_element_type=F32)).astype(o_ref.dtype)


def _s5(u, tables, layer, seg_lens):
    kk, win, wout, al = tables
    L, G, P, C = S5_CHUNK, S5_GROUPS, S5_STATE, S5_GROUP_CH
    n = u.shape[0]
    nc = n // L
    lc = L * C
    v = jnp.transpose(u.reshape(nc, L, G, C), (2, 0, 1, 3)).reshape(G, nc, lc).astype(BF16)
    states = pl.pallas_call(
        _s5_state_kernel,
        grid=(G,),
        in_specs=[pl.BlockSpec((None, nc, lc), lambda g: (g, 0, 0)),
                  pl.BlockSpec((None, None, lc, 4 * P), lambda g: (layer, g, 0, 0))],
        out_specs=pl.BlockSpec((nc, 4 * P), lambda g: (0, g)),
        out_shape=jax.ShapeDtypeStruct((nc, G * 4 * P), F32),
        compiler_params=_cparams(("parallel",)),
        name="s5_chunk_state",
    )(v, win)
    gb = SUBLANES
    carried = pl.pallas_call(
        functools.partial(_s5_scan_kernel, seg_chunks=tuple(t // L for t in seg_lens)),
        grid=(G // gb,),
        in_specs=[pl.BlockSpec((nc, gb, 4 * P), lambda g: (0, g, 0)),
                  pl.BlockSpec((None, gb, 4 * P), lambda g: (layer, g, 0))],
        out_specs=pl.BlockSpec((nc, gb, 4 * P), lambda g: (0, g, 0)),
        out_shape=jax.ShapeDtypeStruct((nc, G, 4 * P), F32),
        compiler_params=_cparams(("parallel",)),
        name="s5_chunk_scan",
    )(states.reshape(nc, G, 4 * P), al)
    y = pl.pallas_call(
        _s5_out_kernel,
        grid=(G,),
        in_specs=[pl.BlockSpec((None, nc, lc), lambda g: (g, 0, 0)),
                  pl.BlockSpec((None, None, C, 2 * lc), lambda g: (layer, g, 0, 0)),
                  pl.BlockSpec((nc, 4 * P), lambda g: (0, g)),
                  pl.BlockSpec((None, None, 4 * P, lc), lambda g: (layer, g, 0, 0))],
        out_specs=pl.BlockSpec((None, nc, lc), lambda g: (g, 0, 0)),
        out_shape=jax.ShapeDtypeStruct((G, nc, lc), BF16),
        scratch_shapes=[pltpu.VMEM((lc, lc), BF16)],
        compiler_params=_cparams(("parallel",)),
        name="s5_chunk_out",
    )(v, kk, carried.reshape(nc, G * 4 * P), wout)
    return jnp.transpose(y.reshape(G, nc, L, C), (1, 2, 0, 3)).reshape(n, G * C)


def _s5_post_kernel(y_ref, u_ref, d_ref, w_ref, b_ref, o_ref):
    y1 = _gelu_tanh(y_ref[...].astype(F32) + d_ref[...] * u_ref[...])
    gate = jnp.dot(y1.astype(BF16), w_ref[...], preferred_element_type=F32) + b_ref[...]
    o_ref[...] = (y1 * _sigmoid(gate)).astype(o_ref.dtype)


def _s5_post(y5, ze, s5_d, glu_w, glu_b, layer):
    n, w = y5.shape
    tm = 512
    return pl.pallas_call(
        _s5_post_kernel,
        grid=(n // tm,),
        in_specs=[pl.BlockSpec((tm, w), lambda i: (i, 0)), pl.BlockSpec((tm, w), lambda i: (i, E_U)),
                  pl.BlockSpec((1, w), lambda i: (0, 0)), pl.BlockSpec((None, w, w), lambda i: (layer, 0, 0)),
                  pl.BlockSpec((1, w), lambda i: (0, 0))],
        out_specs=pl.BlockSpec((tm, w), lambda i: (i, 0)),
        out_shape=jax.ShapeDtypeStruct((n, w), BF16),
        compiler_params=_cparams(("parallel",)),
        name="s5_post",
    )(y5, ze, s5_d.reshape(1, w), glu_w, glu_b.reshape(1, w))


def _t5_buckets(rel):
    half = REL_BUCKETS // 2
    max_exact = half // 2
    sign = (rel > 0).astype(np.int32) * half
    n = np.abs(rel)
    large = max_exact + (np.log(np.maximum(n, 1) / max_exact)
                         / np.log(REL_MAX_DIST / max_exact) * (half - max_exact)).astype(np.int32)
    large = np.minimum(large, half - 1)
    return sign + np.where(n < max_exact, n, large)


def _att_bias_tile(rel_bias, group, dil):
    ncol = Q_BLOCK + 2 * HALF_WIN
    hs = slice(group * HEADS_PER_GROUP, (group + 1) * HEADS_PER_GROUP)
    offs = np.arange(-HALF_WIN, HALF_WIN + 1)
    vals = jnp.transpose(rel_bias[:, hs][_t5_buckets(offs * dil)]).astype(F32)
    width = 2 * ncol
    pad_lo = Q_BLOCK
    line = jnp.pad(vals, ((0, 0), (pad_lo, width - pad_lo - vals.shape[1])), constant_values=NEG_INF)
    rows = jnp.broadcast_to(line[:, None, :], (HEADS_PER_GROUP, Q_BLOCK, width)).reshape(HEADS_PER_GROUP, -1)
    skew = rows[:, :Q_BLOCK * (width - 1)].reshape(HEADS_PER_GROUP, Q_BLOCK, width - 1)
    tile = skew[:, :, pad_lo:pad_lo + ncol]
    col = np.arange(ncol)[None, None, :]
    before = col < HALF_WIN
    after = col >= HALF_WIN + Q_BLOCK
    return jnp.stack([tile, jnp.where(before, NEG_INF, tile), jnp.where(after, NEG_INF, tile),
                      jnp.where(before | after, NEG_INF, tile)])


def _att_kernel(q_ref, kp_ref, km_ref, kn_ref, vp_ref, vm_ref, vn_ref, *rest):
    bias_refs = rest[:ATT_STEP_BLOCKS]
    o_ref, lse_ref = rest[ATT_STEP_BLOCKS:]
    ncol = Q_BLOCK + 2 * HALF_WIN
    main_rows = ATT_STEP_BLOCKS * Q_BLOCK
    low = lax.broadcasted_iota(jnp.int32, (Q_BLOCK, LANES), 1) < HEAD_DIM
    ones = jnp.ones((ncol, LANES), BF16)
    zero = jnp.zeros((Q_BLOCK, LANES), BF16)
    heads = range(HEADS_PER_GROUP)
    pair_cols = [slice((h // 2) * LANES, (h // 2 + 1) * LANES) for h in heads]

    def window(prev_ref, main_ref, next_ref, sb):
        lo = sb * Q_BLOCK - HALF_WIN
        parts = []
        if lo < 0:
            parts.append(prev_ref[...])
        m_lo, m_hi = max(lo, 0), min(lo + ncol, main_rows)
        parts.append(main_ref[pl.ds(m_lo, m_hi - m_lo), :])
        if lo + ncol > main_rows:
            parts.append(next_ref[...])
        return jnp.concatenate(parts, axis=0) if len(parts) > 1 else parts[0]

    for sb in range(ATT_STEP_BLOCKS):
        rows = pl.ds(sb * Q_BLOCK, Q_BLOCK)
        q = q_ref[rows, :]
        k = window(kp_ref, km_ref, kn_ref, sb)
        v = window(vp_ref, vm_ref, vn_ref, sb)
        scores = []
        for h in heads:
            q2 = q[:, pair_cols[h]]
            qh = jnp.where(low, q2, zero) if h % 2 == 0 else jnp.where(low, zero, q2)
            s = lax.dot_general(qh, k[:, pair_cols[h]], (((1,), (1,)), ((), ())), preferred_element_type=F32)
            scores.append(s + bias_refs[sb][h])
        maxes = [jnp.max(s, axis=-1, keepdims=True) for s in scores]
        probs = [jnp.exp(s - m).astype(BF16) for s, m in zip(scores, maxes)]
        outs = [jnp.dot(p, v[:, pair_cols[h]], preferred_element_type=F32) for h, p in zip(heads, probs)]
        sums = [jnp.dot(p, ones, preferred_element_type=F32) for p in probs]
        for pair in range(HEADS_PER_GROUP // 2):
            a, b = 2 * pair, 2 * pair + 1
            l = jnp.where(low, sums[a], sums[b])
            o_ref[rows, pair_cols[a]] = (jnp.where(low, outs[a], outs[b]) / l).astype(o_ref.dtype)
            lse_ref[rows, pair_cols[a]] = jnp.where(low, maxes[a], maxes[b]) + jnp.log(l)


def _attention_group(qkv, bias_tile, dil, seg_lens):
    nd = qkv.shape[1]
    step_rows = ATT_STEP_BLOCKS * Q_BLOCK
    halves_per_step = step_rows // HALF_WIN
    assert nd % step_rows == 0
    nhalf = nd // HALF_WIN
    starts = np.cumsum((0,) + tuple(seg_lens)) // (dil * Q_BLOCK)
    blk_first = tuple(int(s) for s in starts[:-1])
    blk_last = tuple(int(s) - 1 for s in starts[1:])
    ncol = Q_BLOCK + 2 * HALF_WIN

    def variant(blk):
        return _any_eq(blk, blk_first).astype(jnp.int32) + 2 * _any_eq(blk, blk_last).astype(jnp.int32)

    def main(cblk):
        return pl.BlockSpec((None, step_rows, ATT_OUT), lambda r, b: (r, b, cblk))

    def prev(cblk):
        return pl.BlockSpec((None, HALF_WIN, ATT_OUT),
                            lambda r, b: (r, jnp.maximum(halves_per_step * b - 1, 0), cblk))

    def nxt(cblk):
        return pl.BlockSpec((None, HALF_WIN, ATT_OUT),
                            lambda r, b: (r, jnp.minimum(halves_per_step * (b + 1), nhalf - 1), cblk))

    bias_specs = [pl.BlockSpec((None, HEADS_PER_GROUP, Q_BLOCK, ncol),
                               lambda r, b, sb=sb: (variant(ATT_STEP_BLOCKS * b + sb), 0, 0, 0))
                  for sb in range(ATT_STEP_BLOCKS)]
    out_spec = pl.BlockSpec((None, step_rows, ATT_OUT), lambda r, b: (r, b, 0))
    return pl.pallas_call(
        _att_kernel,
        grid=(dil, nd // step_rows),
        in_specs=[main(0), prev(1), main(1), nxt(1), prev(2), main(2), nxt(2)] + bias_specs,
        out_specs=[out_spec, out_spec],
        out_shape=[jax.ShapeDtypeStruct((dil, nd, ATT_OUT), BF16), jax.ShapeDtypeStruct((dil, nd, ATT_OUT), F32)],
        compiler_params=_cparams(("parallel", "parallel")),
        name=f"attention_d{dil}",
    )(qkv, qkv, qkv, qkv, qkv, qkv, qkv, *([bias_tile] * ATT_STEP_BLOCKS))


def _merge_kernel(yl_ref, ys_ref, o0_ref, l0_ref, o1_ref, l1_ref, o2_ref, l2_ref,
                  ga_ref, gb_ref, gc_ref, wl_ref, ws_ref, wa_ref, m_ref, o1_scr, l1_scr, o2_scr, l2_scr, *, tm):
    ntile = ATT_OUT // LANES

    def sequence_order(src, dst, dil):
        rows = tm // dil
        for r in range(dil):
            blk = src[r].astype(F32)
            for c in range(ntile):
                dst[c, pl.ds(r, rows, stride=dil), :] = blk[:, c * LANES:(c + 1) * LANES]
        return jnp.concatenate([dst[c] for c in range(ntile)], axis=-1)

    def gate(ref):
        return _sigmoid(ref[...].astype(F32))

    part = (gate(ga_ref) * jnp.dot(yl_ref[...], wl_ref[...], preferred_element_type=F32)
            + gate(gb_ref) * jnp.dot(ys_ref[...], ws_ref[...], preferred_element_type=F32))

    o1 = sequence_order(o1_ref, o1_scr, ATT_GROUPS[1][1])
    l1 = sequence_order(l1_ref, l1_scr, ATT_GROUPS[1][1])
    o2 = sequence_order(o2_ref, o2_scr, ATT_GROUPS[2][1])
    l2 = sequence_order(l2_ref, l2_scr, ATT_GROUPS[2][1])
    l0 = l0_ref[...]
    mx = jnp.maximum(jnp.maximum(l0, l1), l2)
    e0, e1, e2 = jnp.exp(l0 - mx), jnp.exp(l1 - mx), jnp.exp(l2 - mx)
    yatt = ((o0_ref[...].astype(F32) * e0 + o1 * e1 + o2 * e2) / (e0 + e1 + e2)).astype(BF16)
    m_ref[...] = (part + gate(gc_ref) * jnp.dot(yatt, wa_ref[...], preferred_element_type=F32)).astype(m_ref.dtype)


def _merge(ylru, gates, ys5, att, w_br_lru, w_br_s5, w_br_att, layer):
    n = gates.shape[0]
    d = D_MODEL
    tm = 256
    wl, wa = LRU_WIDTH, ATT_OUT
    gspec = [pl.BlockSpec((tm, d), lambda i, c=c: (i, c)) for c in range(3)]
    att_specs, att_args = [], []
    for (_, dil), (o, l) in zip(ATT_GROUPS, att):
        blk = (None, tm, wa) if dil == 1 else (dil, tm // dil, wa)
        att_specs += [pl.BlockSpec(blk, lambda i: (0, i, 0))] * 2
        att_args += [o, l]
    return pl.pallas_call(
        functools.partial(_merge_kernel, tm=tm),
        grid=(n // tm,),
        in_specs=[pl.BlockSpec((tm, wl), lambda i: (i, 0)), pl.BlockSpec((tm, S5_WIDTH), lambda i: (i, 0))]
        + att_specs + gspec
        + [_resident((None, wl, d), lambda i: (layer, 0, 0)), _resident((None, S5_WIDTH, d), lambda i: (layer, 0, 0)),
           _resident((None, wa, d), lambda i: (layer, 0, 0))],
        out_specs=pl.BlockSpec((tm, d), lambda i: (i, 0)),
        out_shape=jax.ShapeDtypeStruct((n, d), BF16),
        scratch_shapes=[pltpu.VMEM((wa // LANES, tm, LANES), F32)] * 4,
        compiler_params=_cparams(("parallel",)),
        name="merge",
    )(ylru, ys5, *att_args, gates, gates, gates, w_br_lru, w_br_s5, w_br_att)


def _out_proj_kernel(x_ref, m_ref, w_ref, g_ref, o_ref, mix_scr):
    mix_scr[...] = jnp.dot(m_ref[...], w_ref[...], preferred_element_type=F32)
    g = g_ref[...]

    def finish(rows):
        o_ref[rows, :] = x_ref[rows, :] + _rms(mix_scr[rows, :], g)

    _for_row_blocks(mix_scr.shape[0], finish)


def _out_proj(x, m, w_out, g, layer):
    n, d = x.shape
    tm = 512
    return pl.pallas_call(
        _out_proj_kernel,
        grid=(n // tm,),
        in_specs=[pl.BlockSpec((tm, d), lambda i: (i, 0)), pl.BlockSpec((tm, d), lambda i: (i, 0)),
                  _resident((None, d, d), lambda i: (layer, 0, 0)), pl.BlockSpec((1, d), lambda i: (0, 0))],
        out_specs=pl.BlockSpec((tm, d), lambda i: (i, 0)),
        out_shape=jax.ShapeDtypeStruct((n, d), F32),
        scratch_shapes=[pltpu.VMEM((tm, d), F32)],
        compiler_params=_cparams(("parallel",)),
        name="mixer_out",
    )(x, m, w_out, g.reshape(1, d))


def _mixer(x, g_pre, g_post, seg_lens, layer, w_in, conv_w, conv_b, lru_wa, lru_ba, lru_wx, lru_bx, lru_L,
           s5_tables, s5_d, glu_w, glu_b, w_br_lru, w_br_s5, w_br_att, w_out, bias_tiles):
    gates, ze = _proj_elementwise(x, g_pre, w_in, layer)
    qkv = _proj_qkv(x, g_pre, w_in, layer)

    wg, bg = _lru_gate_weights(lru_wa, lru_ba, lru_wx, lru_bx)
    ylru = _lru(ze, conv_w, conv_b, wg, bg, lru_L.reshape(2, 1, LRU_WIDTH), seg_lens)

    y5 = _s5(ze[:, E_U * S5_WIDTH:(E_U + 1) * S5_WIDTH], s5_tables, layer, seg_lens)
    ys5 = _s5_post(y5, ze, s5_d, glu_w, glu_b, layer)

    att = [_attention_group(qkv[g], bias_tiles[g], dil, seg_lens) for g, (_, dil) in enumerate(ATT_GROUPS)]

    m = _merge(ylru, gates, ys5, att, w_br_lru, w_br_s5, w_br_att, layer)
    return _out_proj(x, m, w_out, g_post, layer)


def kernel(x_prompt, x_sample, norm_g, w_in, lru_conv_w, lru_conv_b, lru_wa, lru_ba, lru_wx, lru_bx, lru_L,
           s5_lam_re, s5_lam_im, s5_log_dt, s5_b_re, s5_b_im, s5_c_re, s5_c_im, s5_d, s5_glu_w, s5_glu_b,
           rel_bias, w_br_lru, w_br_s5, w_br_att, w_out, ffn_w1, ffn_w3, ffn_w2):
    bp, tp, d = x_prompt.shape
    bs, ts, _ = x_sample.shape
    seg_lens = (tp,) * bp + (ts,) * bs
    rows = (bp * tp, bs * ts)
    n = sum(rows)
    depth = norm_g.shape[0]
    w1, w3, w2 = ffn_w1.astype(BF16), ffn_w3.astype(BF16), ffn_w2.astype(BF16)
    w_in_b, glu_b16, w_out_b = w_in.astype(BF16), s5_glu_w.astype(BF16), w_out.astype(BF16)
    wbl, wbs, wba = w_br_lru.astype(BF16), w_br_s5.astype(BF16), w_br_att.astype(BF16)
    bias_tiles = [_att_bias_tile(rel_bias, g, dil) for g, (_, dil) in enumerate(ATT_GROUPS)]
    s5_tables = _s5_prepare(s5_lam_re, s5_lam_im, s5_log_dt, s5_b_re, s5_b_im, s5_c_re, s5_c_im)

    xs = [x_prompt.reshape(rows[0], d), x_sample.reshape(rows[1], d)]
    for l in range(depth):
        g = norm_g[l]
        (x,) = _ffn(xs, g[0], g[1], w1, w3, w2, l, 0, (n,))
        x = _mixer(x, g[2], g[3], seg_lens, l, w_in_b, lru_conv_w[l], lru_conv_b[l], lru_wa[l], lru_ba[l],
                   lru_wx[l], lru_bx[l], lru_L[l], s5_tables, s5_d[l], glu_b16, s5_glu_b[l], wbl, wbs, wba, w_out_b,
                   bias_tiles)
        xs = _ffn([x], g[4], g[5], w1, w3, w2, l, 1, rows if l == depth - 1 else (n,))
    return (xs[0].reshape(bp, tp, d), xs[1].reshape(bs, ts, d))
```

```python
import functools
import math

import numpy as np
import jax
import jax.numpy as jnp
from jax import lax
from jax.experimental import pallas as pl
from jax.experimental.pallas import tpu as pltpu

F32 = jnp.float32
BF16 = jnp.bfloat16

D_MODEL = 2048
LRU_WIDTH = 1024
LRU_BLOCKS = 16
LRU_CONV_W = 4
LRU_C = 8.0
S5_WIDTH = 1024
S5_GROUP_CH = 16
S5_GROUPS = 64
S5_STATE = 64
HEAD_DIM = 64
ATT_GROUPS = ((128, 1), (512, 4), (2048, 16))
ATT_WIDTH = 1536
HEADS_PER_GROUP = 8
ATT_OUT = 512
REL_BUCKETS = 32
REL_MAX_DIST = 1024
RMS_EPS = 1e-6
NEG_INF = -1e30

Q_BLOCK = 128
HALF_WIN = 64
ATT_STEP_BLOCKS = 4
S5_CHUNK = 64
LANES = 128
SUBLANES = 8
COL_QKV = 3 * 1024
E_TILE = 1536
E_XL, E_GL, E_U = 0, 1, 2
VMEM_LIMIT = 48 * 1024 * 1024


def _cparams(sem, vmem_limit=VMEM_LIMIT):
    return pltpu.CompilerParams(dimension_semantics=sem, vmem_limit_bytes=vmem_limit)


def _rms(v, g):
    width = v.shape[-1]
    sq = v * v
    part = sq[:, :LANES]
    for c in range(1, width // LANES):
        part = part + sq[:, c * LANES:(c + 1) * LANES]
    ms = jnp.sum(part, axis=-1, keepdims=True) * (1.0 / width)
    return v * lax.rsqrt(ms + RMS_EPS) * g


ROW_BLOCK = 16


def _for_row_blocks(n_rows, body):
    def step(t, carry):
        body(pl.ds(pl.multiple_of(t * ROW_BLOCK, ROW_BLOCK), ROW_BLOCK))
        return carry

    lax.fori_loop(0, n_rows // ROW_BLOCK, step, 0, unroll=16)


def _norm_rows_to(h_scr, x_ref, g_ref):
    g = g_ref[...]

    def body(rows):
        h_scr[rows, :] = _rms(x_ref[rows, :], g).astype(h_scr.dtype)

    _for_row_blocks(x_ref.shape[0], body)


def _gelu_tanh(v):
    return 0.5 * v * (1.0 + jnp.tanh(math.sqrt(2.0 / math.pi) * (v + 0.044715 * (v * v * v))))


def _sigmoid(v):
    return 0.5 * jnp.tanh(0.5 * v) + 0.5


def _any_eq(idx, values):
    hit = idx == values[0]
    for v in values[1:]:
        hit = jnp.logical_or(hit, idx == v)
    return hit


def _resident(shape, index_map):
    return pl.BlockSpec(shape, index_map, pipeline_mode=pl.Buffered(1))


def _ffn_kernel(*refs, nj, in_tiles, out_tiles):
    n_in, n_out = len(in_tiles), len(out_tiles)
    x_refs = refs[:n_in]
    gpre_ref, gpost_ref, w1_ref, w3_ref, w2_ref = refs[n_in:n_in + 5]
    o_refs = refs[n_in + 5:n_in + 5 + n_out]
    h_scr, acc_scr = refs[n_in + 5 + n_out:]
    i = pl.program_id(0)
    j = pl.program_id(1)

    def active(bounds, k):
        lo = sum(bounds[:k])
        return jnp.logical_and(i >= lo, i < lo + bounds[k])

    for k in range(n_in):
        @pl.when(jnp.logical_and(j == 0, active(in_tiles, k)))
        def _(k=k):
            _norm_rows_to(h_scr, x_refs[k], gpre_ref)
            acc_scr[...] = jnp.zeros_like(acc_scr)

    h = h_scr[...]
    a = jnp.dot(h, w1_ref[...], preferred_element_type=F32)
    b = jnp.dot(h, w3_ref[...], preferred_element_type=F32)
    g = (a * _sigmoid(a) * b).astype(BF16)
    acc_scr[...] += jnp.dot(g, w2_ref[...], preferred_element_type=F32)

    for ki in range(n_in):
        for ko in range(n_out):
            @pl.when(jnp.logical_and(j == nj - 1, jnp.logical_and(active(in_tiles, ki), active(out_tiles, ko))))
            def _(ki=ki, ko=ko):
                g_half = 0.5 * gpost_ref[...]

                def finish(rows):
                    o_refs[ko][rows, :] = x_refs[ki][rows, :] + _rms(acc_scr[rows, :], g_half)

                _for_row_blocks(acc_scr.shape[0], finish)


def _ffn(xs, g_pre, g_post, w1, w3, w2, layer, which, out_rows):
    d = xs[0].shape[1]
    dff = w1.shape[-1]
    tm, tf = 512, 512
    nj = dff // tf
    in_tiles = tuple(x.shape[0] // tm for x in xs)
    out_tiles = tuple(r // tm for r in out_rows)

    def piece(bounds, k):
        lo = sum(bounds[:k])
        return lambda i, j: (jnp.clip(i - lo, 0, bounds[k] - 1), 0)

    const = lambda i, j: (0, 0)
    wcol = pl.BlockSpec((None, None, d, tf), lambda i, j: (layer, which, 0, j))
    outs = pl.pallas_call(
        functools.partial(_ffn_kernel, nj=nj, in_tiles=in_tiles, out_tiles=out_tiles),
        grid=(sum(in_tiles), nj),
        in_specs=[pl.BlockSpec((tm, d), piece(in_tiles, k)) for k in range(len(xs))]
        + [pl.BlockSpec((1, d), const), pl.BlockSpec((1, d), const), wcol, wcol,
           pl.BlockSpec((None, None, tf, d), lambda i, j: (layer, which, j, 0))],
        out_specs=[pl.BlockSpec((tm, d), piece(out_tiles, k)) for k in range(len(out_rows))],
        out_shape=[jax.ShapeDtypeStruct((r, d), F32) for r in out_rows],
        scratch_shapes=[pltpu.VMEM((tm, d), BF16), pltpu.VMEM((tm, d), F32)],
        compiler_params=_cparams(("parallel", "arbitrary"),
                                 VMEM_LIMIT + (len(xs) + len(out_rows) - 2) * 2 * tm * d * 4),
        name="ffn",
    )(*xs, g_pre.reshape(1, d), g_post.reshape(1, d), w1, w3, w2)
    return list(outs)


def _norm_mm_kernel(x_ref, g_ref, w_ref, gates_ref, e_ref, h_scr, *, ngate):
    j = pl.program_id(1)

    @pl.when(j == 0)
    def _():
        _norm_rows_to(h_scr, x_ref, g_ref)

    z = jnp.dot(h_scr[...], w_ref[...], preferred_element_type=F32)

    @pl.when(j < ngate)
    def _():
        gates_ref[...] = z.astype(gates_ref.dtype)

    @pl.when(j >= ngate)
    def _():
        e_ref[...] = z


def _proj_elementwise(x, g, w_in, layer):
    n, d = x.shape
    tm, tn = 512, E_TILE
    lead = COL_QKV // tn
    skip = 3 * ATT_WIDTH // tn
    ncol = (w_in.shape[-1] - 3 * ATT_WIDTH) // tn
    ngate = ncol - lead
    return pl.pallas_call(
        functools.partial(_norm_mm_kernel, ngate=ngate),
        grid=(n // tm, ncol),
        in_specs=[pl.BlockSpec((tm, d), lambda i, j: (i, 0)), pl.BlockSpec((1, d), lambda i, j: (0, 0)),
                  pl.BlockSpec((None, d, tn),
                               lambda i, j: (layer, 0, jnp.where(j < ngate, j + lead + skip, j - ngate)))],
        out_specs=[pl.BlockSpec((tm, tn), lambda i, j: (i, jnp.minimum(j, ngate - 1))),
                   pl.BlockSpec((tm, tn), lambda i, j: (i, jnp.maximum(j - ngate, 0)))],
        out_shape=[jax.ShapeDtypeStruct((n, ngate * tn), BF16), jax.ShapeDtypeStruct((n, lead * tn), F32)],
        scratch_shapes=[pltpu.VMEM((tm, d), BF16)],
        compiler_params=_cparams(("parallel", "arbitrary")),
        name="mixer_in",
    )(x, g.reshape(1, d), w_in)


def _qkv_kernel(x_ref, g_ref, wq_ref, wk_ref, wv_ref, o0_ref, o1_ref, o2_ref, h_scr, res_scr, *, tm):
    step = pl.program_id(1)
    o_refs = (o0_ref, o1_ref, o2_ref)
    order = tuple(reversed(range(len(ATT_GROUPS))))
    ntile = 3 * ATT_OUT // LANES

    @pl.when(step == 0)
    def _():
        _norm_rows_to(h_scr, x_ref, g_ref)

    def project():
        h = h_scr[...]
        q = jnp.dot(h, wq_ref[...], preferred_element_type=F32) * (HEAD_DIM ** -0.5)
        k = jnp.dot(h, wk_ref[...], preferred_element_type=F32)
        v = jnp.dot(h, wv_ref[...], preferred_element_type=F32)
        return jnp.concatenate([q, k, v], axis=-1)

    def write_residue_major(slot, gi):
        dil = ATT_GROUPS[gi][1]
        rows = tm // dil
        for r in range(dil):
            o_refs[gi][r] = jnp.concatenate(
                [res_scr[slot, c, pl.ds(r, rows, stride=dil), :] for c in range(ntile)], axis=-1).astype(BF16)

    for s, gi in enumerate(order):
        @pl.when(step == s)
        def _(s=s, gi=gi):
            res = project()
            if s > 0 and ATT_GROUPS[order[s - 1]][1] > 1:
                write_residue_major((s - 1) % 2, order[s - 1])
            if ATT_GROUPS[gi][1] == 1:
                o_refs[gi][0] = res.astype(BF16)
            else:
                for c in range(ntile):
                    res_scr[s % 2, c] = res[:, c * LANES:(c + 1) * LANES]


def _proj_qkv(x, g, w_in, layer):
    n, d = x.shape
    tm = 512
    qb = COL_QKV // ATT_OUT

    ng = len(ATT_GROUPS)
    assert ATT_GROUPS[0][1] == 1

    def wspec(off):
        return pl.BlockSpec((None, d, ATT_OUT), lambda i, s: (layer, 0, qb + off + ng - 1 - s))

    return pl.pallas_call(
        functools.partial(_qkv_kernel, tm=tm),
        grid=(n // tm, ng),
        in_specs=[pl.BlockSpec((tm, d), lambda i, gq: (i, 0)), pl.BlockSpec((1, d), lambda i, gq: (0, 0)),
                  wspec(0), wspec(ng), wspec(2 * ng)],
        out_specs=[pl.BlockSpec((dil, tm // dil, 3 * ATT_OUT), lambda i, gq: (0, i, 0)) for _, dil in ATT_GROUPS],
        out_shape=[jax.ShapeDtypeStruct((dil, n // dil, 3 * ATT_OUT), BF16) for _, dil in ATT_GROUPS],
        scratch_shapes=[pltpu.VMEM((tm, d), BF16), pltpu.VMEM((2, 3 * ATT_OUT // LANES, tm, LANES), F32)],
        compiler_params=_cparams(("parallel", "arbitrary")),
        name="mixer_qkv",
    )(x, g.reshape(1, d), w_in, w_in, w_in)


def _lru_kernel(*refs, tc, nt, seg_first, seg_last, d):
    xm_ref, xp_ref, xn_ref, cw_ref, cb_ref, wg_ref, bg_ref, lam_ref = refs[:8]
    if d == 0:
        o_ref, xpad_scr, a_scr, b_scr, hl_scr, p_scr, h_scr = refs[8:]
    else:
        hf_ref, gl_ref, o_ref, xpad_scr, a_scr, b_scr, hl_scr, p_scr, h_scr = refs[8:]
    i = pl.program_id(0)
    ti = i if d == 0 else nt - 1 - i
    w = LRU_WIDTH
    at_first = _any_eq(ti, seg_first)
    at_last = _any_eq(ti, seg_last)

    xpad_scr[pl.ds(0, SUBLANES), :] = xp_ref[...] * jnp.where(at_first, 0.0, 1.0)
    xpad_scr[pl.ds(SUBLANES, tc), :] = xm_ref[...]
    xpad_scr[pl.ds(SUBLANES + tc, SUBLANES), :] = xn_ref[...] * jnp.where(at_last, 0.0, 1.0)
    left = LRU_CONV_W // 2
    xc = cb_ref[...] + xpad_scr[pl.ds(SUBLANES - left, tc), :] * cw_ref[pl.ds(0, 1), :]
    for kk in range(1, LRU_CONV_W):
        xc = xc + xpad_scr[pl.ds(SUBLANES - left + kk, tc), :] * cw_ref[pl.ds(kk, 1), :]

    xcb = xc.astype(BF16)
    parts = [jnp.dot(xcb[:, p * LANES:(p + 1) * LANES], wg_ref[p], preferred_element_type=F32)
             for p in range(w // LANES)]
    r = _sigmoid(jnp.concatenate([g[:, :LANES] for g in parts], axis=-1) + bg_ref[:, :w])
    ig = _sigmoid(jnp.concatenate([g[:, LANES:] for g in parts], axis=-1) + bg_ref[:, w:])
    nlam = -lam_ref[...]
    softplus = jnp.maximum(nlam, 0.0) + jnp.log(1.0 + jnp.exp(-jnp.abs(nlam)))
    a = jnp.exp(-LRU_C * r * softplus)
    bb = jnp.sqrt(1.0 - a * a) * (ig * xc)

    sub = tc // SUBLANES
    pitch = sub + SUBLANES
    ntile = w // LANES
    for c in range(ntile):
        for jj in range(SUBLANES):
            a_scr[c, pl.ds(jj * pitch, sub), :] = a[jj * sub:(jj + 1) * sub, c * LANES:(c + 1) * LANES]
            b_scr[c, pl.ds(jj * pitch, sub), :] = bb[jj * sub:(jj + 1) * sub, c * LANES:(c + 1) * LANES]

    def strided_rows(ref, k):
        return jnp.concatenate([ref[c, pl.ds(k, SUBLANES, stride=pitch), :] for c in range(ntile)], axis=-1)

    @pl.when(at_first if d == 0 else at_last)
    def _():
        h_scr[...] = jnp.zeros_like(h_scr)

    sub_id = lax.broadcasted_iota(jnp.int32, (SUBLANES, w), 0)
    hl = jnp.zeros((SUBLANES, w), F32)
    pp = jnp.ones((SUBLANES, w), F32)
    for k in (range(sub) if d == 0 else range(sub - 1, -1, -1)):
        av = strided_rows(a_scr, k)
        bv = strided_rows(b_scr, k)
        hl = av * hl + bv
        pp = av * pp
        hl_scr[pl.ds(k * SUBLANES, SUBLANES), :] = hl
        p_scr[pl.ds(k * SUBLANES, SUBLANES), :] = pp
    cur = h_scr[pl.ds(0, 1), :]
    carry = jnp.zeros((SUBLANES, w), F32)
    for jj in (range(SUBLANES) if d == 0 else range(SUBLANES - 1, -1, -1)):
        carry = jnp.where(sub_id == jj, cur, carry)
        cur = hl[jj:jj + 1, :] + pp[jj:jj + 1, :] * cur
    h_scr[pl.ds(0, 1), :] = cur
    for k in range(sub):
        rows = pl.ds(k * SUBLANES, SUBLANES)
        hv = hl_scr[rows, :] + p_scr[rows, :] * carry
        for c in range(ntile):
            a_scr[c, pl.ds(k, SUBLANES, stride=pitch), :] = hv[:, c * LANES:(c + 1) * LANES]
    for jj in range(SUBLANES):
        rows = pl.ds(jj * sub, sub)
        hv = jnp.concatenate([a_scr[c, pl.ds(jj * pitch, sub), :] for c in range(ntile)], axis=-1)
        if d == 0:
            o_ref[rows, :] = hv
        else:
            o_ref[rows, :] = ((hf_ref[rows, :] + hv) * _gelu_tanh(gl_ref[rows, :])).astype(o_ref.dtype)


def _lru(ze, conv_w, conv_b, wg, bg, lam, seg_lens):
    n = ze.shape[0]
    w = LRU_WIDTH
    tc = 256
    nt = n // tc
    starts = np.cumsum((0,) + tuple(seg_lens))
    seg_first = tuple(int(s) // tc for s in starts[:-1])
    seg_last = tuple(int(s) // tc - 1 for s in starts[1:])
    hb = tc // SUBLANES
    nhb = n // SUBLANES
    pitched = SUBLANES * (tc // SUBLANES + SUBLANES)

    def one_direction(d, extra_specs, extra_args, out_dtype):
        tile = (lambda i: i) if d == 0 else (lambda i: nt - 1 - i)
        return pl.pallas_call(
            functools.partial(_lru_kernel, tc=tc, nt=nt, seg_first=seg_first, seg_last=seg_last, d=d),
            grid=(nt,),
            in_specs=[pl.BlockSpec((tc, w), lambda i: (tile(i), E_XL)),
                      pl.BlockSpec((SUBLANES, w), lambda i: (jnp.maximum(tile(i) * hb - 1, 0), E_XL)),
                      pl.BlockSpec((SUBLANES, w), lambda i: (jnp.minimum((tile(i) + 1) * hb, nhb - 1), E_XL)),
                      pl.BlockSpec((LRU_CONV_W, w), lambda i: (0, 0)),
                      pl.BlockSpec((1, w), lambda i: (0, 0)),
                      _resident((None, w // LANES, LANES, 2 * LANES), lambda i: (d, 0, 0, 0)),
                      pl.BlockSpec((None, 1, 2 * w), lambda i: (d, 0, 0)),
                      pl.BlockSpec((None, 1, w), lambda i: (d, 0, 0))] + [spec(tile) for spec in extra_specs],
            out_specs=pl.BlockSpec((tc, w), lambda i: (tile(i), 0)),
            out_shape=jax.ShapeDtypeStruct((n, w), out_dtype),
            scratch_shapes=[pltpu.VMEM((tc + 2 * SUBLANES, w), F32), pltpu.VMEM((w // LANES, pitched, LANES), F32),
                            pltpu.VMEM((w // LANES, pitched, LANES), F32), pltpu.VMEM((tc, w), F32),
                            pltpu.VMEM((tc, w), F32), pltpu.VMEM((SUBLANES, w), F32)],
            compiler_params=_cparams(("arbitrary",)),
            name="rglru_fwd" if d == 0 else "rglru_bwd",
        )(ze, ze, ze, conv_w, conv_b.reshape(1, w), wg, bg, lam, *extra_args)

    hf = one_direction(0, [], [], F32)
    return one_direction(1, [lambda tile: pl.BlockSpec((tc, w), lambda i: (tile(i), 0)),
                             lambda tile: pl.BlockSpec((tc, w), lambda i: (tile(i), E_GL))], [hf, ze], BF16)


def _lru_gate_weights(wa, ba, wx, bx):
    per_tile = LANES * LRU_BLOCKS // LRU_WIDTH

    def tiles(wb):
        bw = wb.shape[-1]
        eye = jnp.eye(per_tile, dtype=wb.dtype)
        grouped = wb.reshape(2, LRU_BLOCKS // per_tile, per_tile, bw, bw)
        full = grouped[:, :, :, :, None, :] * eye[None, None, :, None, :, None]
        return full.reshape(2, LRU_BLOCKS // per_tile, LANES, LANES)

    wg = jnp.concatenate([tiles(wa), tiles(wx)], axis=-1).astype(BF16)
    bg = jnp.concatenate([ba, bx], axis=-1).reshape(2, 1, 2 * LRU_WIDTH)
    return wg, bg


def _s5_ktable_kernel(b_ref, w_ref, o_ref):
    o_ref[...] = jnp.dot(b_ref[...], w_ref[...], preferred_element_type=F32, precision=lax.Precision.HIGHEST)


def _s5_tables(lam_re, lam_im, log_dt, b_re, b_im, c_re, c_im):
    L, G, P, C = S5_CHUNK, S5_GROUPS, S5_STATE, S5_GROUP_CH
    dt = jnp.exp(log_dt)[..., None]
    mag = jnp.exp(lam_re * dt)
    ar = mag * jnp.cos(lam_im * dt)
    ai = mag * jnp.sin(lam_im * dt)
    den = lam_re * lam_re + lam_im * lam_im
    cr = ((ar - 1.0) * lam_re + ai * lam_im) / den
    ci = (ai * lam_re - (ar - 1.0) * lam_im) / den
    bbr = cr[..., None] * b_re - ci[..., None] * b_im
    bbi = cr[..., None] * b_im + ci[..., None] * b_re
    pr, pi = jnp.ones_like(ar)[None], jnp.zeros_like(ai)[None]
    nr, ni = ar, ai
    while pr.shape[0] < L + 1:
        pr, pi = (jnp.concatenate([pr, pr * nr - pi * ni], axis=0),
                  jnp.concatenate([pi, pr * ni + pi * nr], axis=0))
        nr, ni = nr * nr - ni * ni, 2.0 * nr * ni
    pr, pi = pr[:L + 1], pi[:L + 1]
    zr = pr[:L, ..., None] * bbr - pi[:L, ..., None] * bbi
    zi = pr[:L, ..., None] * bbi + pi[:L, ..., None] * bbr

    def c_pow(powers_r, powers_i):
        ctr = jnp.transpose(c_re, (0, 1, 3, 2))[:, :, :, None, :]
        cti = jnp.transpose(c_im, (0, 1, 3, 2))[:, :, :, None, :]
        qr = jnp.transpose(powers_r, (1, 2, 3, 0))[..., None]
        qi = jnp.transpose(powers_i, (1, 2, 3, 0))[..., None]
        return ctr * qr - cti * qi, ctr * qi + cti * qr

    wr, wi = c_pow(pr[:L], pi[:L])
    wmat = jnp.concatenate([wr, wi], axis=2).reshape(2 * G, 2 * P, L * C)
    bmat = jnp.concatenate([jnp.transpose(bbr, (0, 1, 3, 2)), -jnp.transpose(bbi, (0, 1, 3, 2))],
                           axis=-1).reshape(2 * G, C, 2 * P)

    def w_in(z, flip):
        zf = z[::-1] if flip else z
        return jnp.transpose(zf, (1, 0, 3, 2)).reshape(G, L * C, P)

    win = jnp.concatenate([w_in(zr[:, 0], True), w_in(zr[:, 1], False),
                           w_in(zi[:, 0], True), w_in(zi[:, 1], False)], axis=-1).astype(BF16)

    fr, fi = c_pow(pr[1:], pi[1:])
    br, bi = c_pow(pr[:0:-1], pi[:0:-1])
    wout = jnp.concatenate([fr[0], br[1], -fi[0], -bi[1]], axis=1).reshape(G, 4 * P, L * C).astype(BF16)
    al = jnp.concatenate([pr[L, 0], pr[L, 1], pi[L, 0], pi[L, 1]], axis=-1)
    return bmat, wmat, win, wout, al


def _s5_lag_table(kt):
    L, G, C = S5_CHUNK, S5_GROUPS, S5_GROUP_CH
    kk = jnp.concatenate([kt[1, :, :, :0:-1], kt[0, :, :, :1] + kt[1, :, :, :1], kt[0, :, :, 1:]], axis=2)
    kk = kk.reshape(G, C, (2 * L - 1) * C)
    return jnp.pad(kk, ((0, 0), (0, 0), (0, 2 * L * C - kk.shape[-1])))


def _s5_prepare(lam_re, lam_im, log_dt, b_re, b_im, c_re, c_im):
    L, G, P, C = S5_CHUNK, S5_GROUPS, S5_STATE, S5_GROUP_CH
    bmat, wmat, win, wout, al = _s5_tables(lam_re, lam_im, log_dt, b_re, b_im, c_re, c_im)
    kt = pl.pallas_call(
        _s5_ktable_kernel,
        grid=(2 * G,),
        in_specs=[pl.BlockSpec((None, C, 2 * P), lambda g: (g, 0, 0)),
                  pl.BlockSpec((None, 2 * P, L * C), lambda g: (g, 0, 0))],
        out_specs=pl.BlockSpec((None, C, L * C), lambda g: (g, 0, 0)),
        out_shape=jax.ShapeDtypeStruct((2 * G, C, L * C), F32),
        compiler_params=_cparams(("parallel",)),
        name="s5_ktable",
    )(bmat, wmat)
    return _s5_lag_table(kt.reshape(2, G, C, L, C)), win, wout, al


def _s5_state_kernel(v_ref, win_ref, o_ref):
    o_ref[...] = jnp.dot(v_ref[...], win_ref[...], preferred_element_type=F32)


def _s5_scan_kernel(s_ref, al_ref, o_ref, *, seg_chunks):
    p2 = 2 * S5_STATE
    alr = al_ref[:, :p2]
    ali = al_ref[:, p2:]
    is_fwd = lax.broadcasted_iota(jnp.int32, alr.shape, 1) < S5_STATE
    zero = jnp.zeros_like(alr)
    start = 0
    for n_chunks in seg_chunks:
        def fwd(k, carry, start=start):
            xr, xi = carry
            c = start + k
            o_ref[c, :, :p2] = xr
            o_ref[c, :, p2:] = xi
            sr = s_ref[c, :, :p2]
            si = s_ref[c, :, p2:]
            return alr * xr - ali * xi + sr, alr * xi + ali * xr + si

        lax.fori_loop(0, n_chunks, fwd, (zero, zero))

        def bwd(k, carry, start=start, n_chunks=n_chunks):
            xr, xi = carry
            c = start + n_chunks - 1 - k
            o_ref[c, :, :p2] = jnp.where(is_fwd, o_ref[c, :, :p2], xr)
            o_ref[c, :, p2:] = jnp.where(is_fwd, o_ref[c, :, p2:], xi)
            sr = s_ref[c, :, :p2]
            si = s_ref[c, :, p2:]
            return alr * xr - ali * xi + sr, alr * xi + ali * xr + si

        lax.fori_loop(0, n_chunks, bwd, (zero, zero))
        start += n_chunks


def _s5_out_kernel(v_ref, kk_ref, x_ref, wout_ref, o_ref, mt_scr):
    L, C = S5_CHUNK, S5_GROUP_CH
    kk = kk_ref[...]
    width = kk.shape[-1]
    per_tile = LANES // C
    for rot in range(per_tile):
        shifted = kk if rot == 0 else pltpu.roll(kk, width - rot * C, axis=1)
        shifted = shifted.astype(BF16)
        for s in range(L):
            lag0 = L - 1 - s
            if lag0 % per_tile == rot:
                col = (lag0 // per_tile) * LANES
                mt_scr[pl.ds(s * C, C), :] = shifted[:, col:col + L * C]
    o_ref[...] = (jnp.dot(v_ref[...], mt_scr[...], preferred_element_type=F32)
                  + jnp.dot(x_ref[...].astype(BF16), wout_ref[...], preferred_element_type=F32)).astype(o_ref.dtype)


def _s5(u, tables, seg_lens):
    kk, win, wout, al = tables
    L, G, P, C = S5_CHUNK, S5_GROUPS, S5_STATE, S5_GROUP_CH
    n = u.shape[0]
    nc = n // L
    lc = L * C
    v = jnp.transpose(u.reshape(nc, L, G, C), (2, 0, 1, 3)).reshape(G, nc, lc).astype(BF16)
    states = pl.pallas_call(
        _s5_state_kernel,
        grid=(G,),
        in_specs=[pl.BlockSpec((None, nc, lc), lambda g: (g, 0, 0)),
                  pl.BlockSpec((None, lc, 4 * P), lambda g: (g, 0, 0))],
        out_specs=pl.BlockSpec((nc, 4 * P), lambda g: (0, g)),
        out_shape=jax.ShapeDtypeStruct((nc, G * 4 * P), F32),
        compiler_params=_cparams(("parallel",)),
        name="s5_chunk_state",
    )(v, win)
    gb = SUBLANES
    carried = pl.pallas_call(
        functools.partial(_s5_scan_kernel, seg_chunks=tuple(t // L for t in seg_lens)),
        grid=(G // gb,),
        in_specs=[pl.BlockSpec((nc, gb, 4 * P), lambda g: (0, g, 0)),
                  pl.BlockSpec((gb, 4 * P), lambda g: (g, 0))],
        out_specs=pl.BlockSpec((nc, gb, 4 * P), lambda g: (0, g, 0)),
        out_shape=jax.ShapeDtypeStruct((nc, G, 4 * P), F32),
        compiler_params=_cparams(("parallel",)),
        name="s5_chunk_scan",
    )(states.reshape(nc, G, 4 * P), al)
    y = pl.pallas_call(
        _s5_out_kernel,
        grid=(G,),
        in_specs=[pl.BlockSpec((None, nc, lc), lambda g: (g, 0, 0)),
                  pl.BlockSpec((None, C, 2 * lc), lambda g: (g, 0, 0)),
                  pl.BlockSpec((nc, 4 * P), lambda g: (0, g)),
                  pl.BlockSpec((None, 4 * P, lc), lambda g: (g, 0, 0))],
        out_specs=pl.BlockSpec((None, nc, lc), lambda g: (g, 0, 0)),
        out_shape=jax.ShapeDtypeStruct((G, nc, lc), BF16),
        scratch_shapes=[pltpu.VMEM((lc, lc), BF16)],
        compiler_params=_cparams(("parallel",)),
        name="s5_chunk_out",
    )(v, kk, carried.reshape(nc, G * 4 * P), wout)
    return jnp.transpose(y.reshape(G, nc, L, C), (1, 2, 0, 3)).reshape(n, G * C)


def _s5_post_kernel(y_ref, u_ref, d_ref, w_ref, b_ref, o_ref):
    y1 = _gelu_tanh(y_ref[...].astype(F32) + d_ref[...] * u_ref[...])
    gate = jnp.dot(y1.astype(BF16), w_ref[...], preferred_element_type=F32) + b_ref[...]
    o_ref[...] = (y1 * _sigmoid(gate)).astype(o_ref.dtype)


def _s5_post(y5, ze, s5_d, glu_w, glu_b, layer):
    n, w = y5.shape
    tm = 512
    return pl.pallas_call(
        _s5_post_kernel,
        grid=(n // tm,),
        in_specs=[pl.BlockSpec((tm, w), lambda i: (i, 0)), pl.BlockSpec((tm, w), lambda i: (i, E_U)),
                  pl.BlockSpec((1, w), lambda i: (0, 0)), pl.BlockSpec((None, w, w), lambda i: (layer, 0, 0)),
                  pl.BlockSpec((1, w), lambda i: (0, 0))],
        out_specs=pl.BlockSpec((tm, w), lambda i: (i, 0)),
        out_shape=jax.ShapeDtypeStruct((n, w), BF16),
        compiler_params=_cparams(("parallel",)),
        name="s5_post",
    )(y5, ze, s5_d.reshape(1, w), glu_w, glu_b.reshape(1, w))


def _t5_buckets(rel):
    half = REL_BUCKETS // 2
    max_exact = half // 2
    sign = (rel > 0).astype(np.int32) * half
    n = np.abs(rel)
    large = max_exact + (np.log(np.maximum(n, 1) / max_exact)
                         / np.log(REL_MAX_DIST / max_exact) * (half - max_exact)).astype(np.int32)
    large = np.minimum(large, half - 1)
    return sign + np.where(n < max_exact, n, large)


def _att_bias_tile(rel_bias, group, dil):
    ncol = Q_BLOCK + 2 * HALF_WIN
    hs = slice(group * HEADS_PER_GROUP, (group + 1) * HEADS_PER_GROUP)
    offs = np.arange(-HALF_WIN, HALF_WIN + 1)
    vals = jnp.transpose(rel_bias[:, hs][_t5_buckets(offs * dil)]).astype(F32)
    width = 2 * ncol
    pad_lo = Q_BLOCK
    line = jnp.pad(vals, ((0, 0), (pad_lo, width - pad_lo - vals.shape[1])), constant_values=NEG_INF)
    rows = jnp.broadcast_to(line[:, None, :], (HEADS_PER_GROUP, Q_BLOCK, width)).reshape(HEADS_PER_GROUP, -1)
    skew = rows[:, :Q_BLOCK * (width - 1)].reshape(HEADS_PER_GROUP, Q_BLOCK, width - 1)
    tile = skew[:, :, pad_lo:pad_lo + ncol]
    col = np.arange(ncol)[None, None, :]
    before = col < HALF_WIN
    after = col >= HALF_WIN + Q_BLOCK
    return jnp.stack([tile, jnp.where(before, NEG_INF, tile), jnp.where(after, NEG_INF, tile),
                      jnp.where(before | after, NEG_INF, tile)])


def _att_kernel(q_ref, kp_ref, km_ref, kn_ref, vp_ref, vm_ref, vn_ref, *rest):
    bias_refs = rest[:ATT_STEP_BLOCKS]
    o_ref, lse_ref = rest[ATT_STEP_BLOCKS:]
    ncol = Q_BLOCK + 2 * HALF_WIN
    main_rows = ATT_STEP_BLOCKS * Q_BLOCK
    low = lax.broadcasted_iota(jnp.int32, (Q_BLOCK, LANES), 1) < HEAD_DIM
    ones = jnp.ones((ncol, LANES), BF16)
    zero = jnp.zeros((Q_BLOCK, LANES), BF16)
    heads = range(HEADS_PER_GROUP)
    pair_cols = [slice((h // 2) * LANES, (h // 2 + 1) * LANES) for h in heads]

    def window(prev_ref, main_ref, next_ref, sb):
        lo = sb * Q_BLOCK - HALF_WIN
        parts = []
        if lo < 0:
            parts.append(prev_ref[...])
        m_lo, m_hi = max(lo, 0), min(lo + ncol, main_rows)
        parts.append(main_ref[pl.ds(m_lo, m_hi - m_lo), :])
        if lo + ncol > main_rows:
            parts.append(next_ref[...])
        return jnp.concatenate(parts, axis=0) if len(parts) > 1 else parts[0]

    for sb in range(ATT_STEP_BLOCKS):
        rows = pl.ds(sb * Q_BLOCK, Q_BLOCK)
        q = q_ref[rows, :]
        k = window(kp_ref, km_ref, kn_ref, sb)
        v = window(vp_ref, vm_ref, vn_ref, sb)
        scores = []
        for h in heads:
            q2 = q[:, pair_cols[h]]
            qh = jnp.where(low, q2, zero) if h % 2 == 0 else jnp.where(low, zero, q2)
            s = lax.dot_general(qh, k[:, pair_cols[h]], (((1,), (1,)), ((), ())), preferred_element_type=F32)
            scores.append(s + bias_refs[sb][h])
        maxes = [jnp.max(s, axis=-1, keepdims=True) for s in scores]
        probs = [jnp.exp(s - m).astype(BF16) for s, m in zip(scores, maxes)]
        outs = [jnp.dot(p, v[:, pair_cols[h]], preferred_element_type=F32) for h, p in zip(heads, probs)]
        sums = [jnp.dot(p, ones, preferred_element_type=F32) for p in probs]
        for pair in range(HEADS_PER_GROUP // 2):
            a, b = 2 * pair, 2 * pair + 1
            l = jnp.where(low, sums[a], sums[b])
            o_ref[rows, pair_cols[a]] = (jnp.where(low, outs[a], outs[b]) / l).astype(o_ref.dtype)
            lse_ref[rows, pair_cols[a]] = jnp.where(low, maxes[a], maxes[b]) + jnp.log(l)


def _attention_group(qkv, bias_tile, dil, seg_lens):
    nd = qkv.shape[1]
    step_rows = ATT_STEP_BLOCKS * Q_BLOCK
    halves_per_step = step_rows // HALF_WIN
    assert nd % step_rows == 0
    nhalf = nd // HALF_WIN
    starts = np.cumsum((0,) + tuple(seg_lens)) // (dil * Q_BLOCK)
    blk_first = tuple(int(s) for s in starts[:-1])
    blk_last = tuple(int(s) - 1 for s in starts[1:])
    ncol = Q_BLOCK + 2 * HALF_WIN

    def variant(blk):
        return _any_eq(blk, blk_first).astype(jnp.int32) + 2 * _any_eq(blk, blk_last).astype(jnp.int32)

    def main(cblk):
        return pl.BlockSpec((None, step_rows, ATT_OUT), lambda r, b: (r, b, cblk))

    def prev(cblk):
        return pl.BlockSpec((None, HALF_WIN, ATT_OUT),
                            lambda r, b: (r, jnp.maximum(halves_per_step * b - 1, 0), cblk))

    def nxt(cblk):
        return pl.BlockSpec((None, HALF_WIN, ATT_OUT),
                            lambda r, b: (r, jnp.minimum(halves_per_step * (b + 1), nhalf - 1), cblk))

    bias_specs = [pl.BlockSpec((None, HEADS_PER_GROUP, Q_BLOCK, ncol),
                               lambda r, b, sb=sb: (variant(ATT_STEP_BLOCKS * b + sb), 0, 0, 0))
                  for sb in range(ATT_STEP_BLOCKS)]
    out_spec = pl.BlockSpec((None, step_rows, ATT_OUT), lambda r, b: (r, b, 0))
    return pl.pallas_call(
        _att_kernel,
        grid=(dil, nd // step_rows),
        in_specs=[main(0), prev(1), main(1), nxt(1), prev(2), main(2), nxt(2)] + bias_specs,
        out_specs=[out_spec, out_spec],
        out_shape=[jax.ShapeDtypeStruct((dil, nd, ATT_OUT), BF16), jax.ShapeDtypeStruct((dil, nd, ATT_OUT), F32)],
        compiler_params=_cparams(("parallel", "parallel")),
        name=f"attention_d{dil}",
    )(qkv, qkv, qkv, qkv, qkv, qkv, qkv, *([bias_tile] * ATT_STEP_BLOCKS))


def _merge_kernel(yl_ref, ys_ref, o0_ref, l0_ref, o1_ref, l1_ref, o2_ref, l2_ref,
                  ga_ref, gb_ref, gc_ref, wl_ref, ws_ref, wa_ref, m_ref, o1_scr, l1_scr, o2_scr, l2_scr, *, tm):
    ntile = ATT_OUT // LANES

    def sequence_order(src, dst, dil):
        rows = tm // dil
        for r in range(dil):
            blk = src[r].astype(F32)
            for c in range(ntile):
                dst[c, pl.ds(r, rows, stride=dil), :] = blk[:, c * LANES:(c + 1) * LANES]
        return jnp.concatenate([dst[c] for c in range(ntile)], axis=-1)

    def gate(ref):
        return _sigmoid(ref[...].astype(F32))

    part = (gate(ga_ref) * jnp.dot(yl_ref[...], wl_ref[...], preferred_element_type=F32)
            + gate(gb_ref) * jnp.dot(ys_ref[...], ws_ref[...], preferred_element_type=F32))

    o1 = sequence_order(o1_ref, o1_scr, ATT_GROUPS[1][1])
    l1 = sequence_order(l1_ref, l1_scr, ATT_GROUPS[1][1])
    o2 = sequence_order(o2_ref, o2_scr, ATT_GROUPS[2][1])
    l2 = sequence_order(l2_ref, l2_scr, ATT_GROUPS[2][1])
    l0 = l0_ref[...]
    mx = jnp.maximum(jnp.maximum(l0, l1), l2)
    e0, e1, e2 = jnp.exp(l0 - mx), jnp.exp(l1 - mx), jnp.exp(l2 - mx)
    yatt = ((o0_ref[...].astype(F32) * e0 + o1 * e1 + o2 * e2) / (e0 + e1 + e2)).astype(BF16)
    m_ref[...] = (part + gate(gc_ref) * jnp.dot(yatt, wa_ref[...], preferred_element_type=F32)).astype(m_ref.dtype)


def _merge(ylru, gates, ys5, att, w_br_lru, w_br_s5, w_br_att, layer):
    n = gates.shape[0]
    d = D_MODEL
    tm = 256
    wl, wa = LRU_WIDTH, ATT_OUT
    gspec = [pl.BlockSpec((tm, d), lambda i, c=c: (i, c)) for c in range(3)]
    att_specs, att_args = [], []
    for (_, dil), (o, l) in zip(ATT_GROUPS, att):
        blk = (None, tm, wa) if dil == 1 else (dil, tm // dil, wa)
        att_specs += [pl.BlockSpec(blk, lambda i: (0, i, 0))] * 2
        att_args += [o, l]
    return pl.pallas_call(
        functools.partial(_merge_kernel, tm=tm),
        grid=(n // tm,),
        in_specs=[pl.BlockSpec((tm, wl), lambda i: (i, 0)), pl.BlockSpec((tm, S5_WIDTH), lambda i: (i, 0))]
        + att_specs + gspec
        + [_resident((None, wl, d), lambda i: (layer, 0, 0)), _resident((None, S5_WIDTH, d), lambda i: (layer, 0, 0)),
           _resident((None, wa, d), lambda i: (layer, 0, 0))],
        out_specs=pl.BlockSpec((tm, d), lambda i: (i, 0)),
        out_shape=jax.ShapeDtypeStruct((n, d), BF16),
        scratch_shapes=[pltpu.VMEM((wa // LANES, tm, LANES), F32)] * 4,
        compiler_params=_cparams(("parallel",)),
        name="merge",
    )(ylru, ys5, *att_args, gates, gates, gates, w_br_lru, w_br_s5, w_br_att)


def _out_proj_kernel(x_ref, m_ref, w_ref, g_ref, o_ref):
    mix = jnp.dot(m_ref[...], w_ref[...], preferred_element_type=F32)
    o_ref[...] = x_ref[...] + _rms(mix, g_ref[...])


def _out_proj(x, m, w_out, g, layer):
    n, d = x.shape
    tm = 512
    return pl.pallas_call(
        _out_proj_kernel,
        grid=(n // tm,),
        in_specs=[pl.BlockSpec((tm, d), lambda i: (i, 0)), pl.BlockSpec((tm, d), lambda i: (i, 0)),
                  _resident((None, d, d), lambda i: (layer, 0, 0)), pl.BlockSpec((1, d), lambda i: (0, 0))],
        out_specs=pl.BlockSpec((tm, d), lambda i: (i, 0)),
        out_shape=jax.ShapeDtypeStruct((n, d), F32),
        compiler_params=_cparams(("parallel",)),
        name="mixer_out",
    )(x, m, w_out, g.reshape(1, d))


def _mixer(x, g_pre, g_post, seg_lens, layer, w_in, conv_w, conv_b, lru_wa, lru_ba, lru_wx, lru_bx, lru_L,
           s5_params, s5_d, glu_w, glu_b, w_br_lru, w_br_s5, w_br_att, w_out, bias_tiles):
    gates, ze = _proj_elementwise(x, g_pre, w_in, layer)
    qkv = _proj_qkv(x, g_pre, w_in, layer)

    wg, bg = _lru_gate_weights(lru_wa, lru_ba, lru_wx, lru_bx)
    ylru = _lru(ze, conv_w, conv_b, wg, bg, lru_L.reshape(2, 1, LRU_WIDTH), seg_lens)

    y5 = _s5(ze[:, E_U * S5_WIDTH:(E_U + 1) * S5_WIDTH], _s5_prepare(*s5_params), seg_lens)
    ys5 = _s5_post(y5, ze, s5_d, glu_w, glu_b, layer)

    att = [_attention_group(qkv[g], bias_tiles[g], dil, seg_lens) for g, (_, dil) in enumerate(ATT_GROUPS)]

    m = _merge(ylru, gates, ys5, att, w_br_lru, w_br_s5, w_br_att, layer)
    return _out_proj(x, m, w_out, g_post, layer)


def kernel(x_prompt, x_sample, norm_g, w_in, lru_conv_w, lru_conv_b, lru_wa, lru_ba, lru_wx, lru_bx, lru_L,
           s5_lam_re, s5_lam_im, s5_log_dt, s5_b_re, s5_b_im, s5_c_re, s5_c_im, s5_d, s5_glu_w, s5_glu_b,
           rel_bias, w_br_lru, w_br_s5, w_br_att, w_out, ffn_w1, ffn_w3, ffn_w2):
    bp, tp, d = x_prompt.shape
    bs, ts, _ = x_sample.shape
    seg_lens = (tp,) * bp + (ts,) * bs
    rows = (bp * tp, bs * ts)
    n = sum(rows)
    depth = norm_g.shape[0]
    w1, w3, w2 = ffn_w1.astype(BF16), ffn_w3.astype(BF16), ffn_w2.astype(BF16)
    w_in_b, glu_b16, w_out_b = w_in.astype(BF16), s5_glu_w.astype(BF16), w_out.astype(BF16)
    wbl, wbs, wba = w_br_lru.astype(BF16), w_br_s5.astype(BF16), w_br_att.astype(BF16)
    bias_tiles = [_att_bias_tile(rel_bias, g, dil) for g, (_, dil) in enumerate(ATT_GROUPS)]
    s5_stack = (s5_lam_re, s5_lam_im, s5_log_dt, s5_b_re, s5_b_im, s5_c_re, s5_c_im)

    xs = [x_prompt.reshape(rows[0], d), x_sample.reshape(rows[1], d)]
    for l in range(depth):
        g = norm_g[l]
        (x,) = _ffn(xs, g[0], g[1], w1, w3, w2, l, 0, (n,))
        x = _mixer(x, g[2], g[3], seg_lens, l, w_in_b, lru_conv_w[l], lru_conv_b[l], lru_wa[l], lru_ba[l],
                   lru_wx[l], lru_bx[l], lru_L[l], [p[l] for p in s5_stack], s5_d[l], glu_b16, s5_glu_b[l],
                   wbl, wbs, wba, w_out_b,
                   bias_tiles)
        xs = _ffn([x], g[4], g[5], w1, w3, w2, l, 1, rows if l == depth - 1 else (n,))
    return (xs[0].reshape(bp, tp, d), xs[1].reshape(bs, ts, d))
```

```python
import functools
import math

import numpy as np
import jax
import jax.numpy as jnp
from jax import lax
from jax.experimental import pallas as pl
from jax.experimental.pallas import tpu as pltpu

F32 = jnp.float32
BF16 = jnp.bfloat16

D_MODEL = 2048
LRU_WIDTH = 1024
LRU_BLOCKS = 16
LRU_CONV_W = 4
LRU_C = 8.0
S5_WIDTH = 1024
S5_GROUP_CH = 16
S5_GROUPS = 64
S5_STATE = 64
HEAD_DIM = 64
ATT_GROUPS = ((128, 1), (512, 4), (2048, 16))
ATT_WIDTH = 1536
HEADS_PER_GROUP = 8
ATT_OUT = 512
REL_BUCKETS = 32
REL_MAX_DIST = 1024
RMS_EPS = 1e-6
NEG_INF = -1e30

Q_BLOCK = 128
HALF_WIN = 64
ATT_STEP_BLOCKS = 4
S5_CHUNK = 64
LANES = 128
SUBLANES = 8
COL_QKV = 3 * 1024
E_TILE = 1536
E_XL, E_GL, E_U = 0, 1, 2
VMEM_LIMIT = 48 * 1024 * 1024


def _cparams(sem, vmem_limit=VMEM_LIMIT):
    return pltpu.CompilerParams(dimension_semantics=sem, vmem_limit_bytes=vmem_limit)


def _rms(v, g):
    width = v.shape[-1]
    sq = v * v
    part = sq[:, :LANES]
    for c in range(1, width // LANES):
        part = part + sq[:, c * LANES:(c + 1) * LANES]
    ms = jnp.sum(part, axis=-1, keepdims=True) * (1.0 / width)
    return v * lax.rsqrt(ms + RMS_EPS) * g


ROW_BLOCK = 16


def _for_row_blocks(n_rows, body):
    def step(t, carry):
        body(pl.ds(pl.multiple_of(t * ROW_BLOCK, ROW_BLOCK), ROW_BLOCK))
        return carry

    lax.fori_loop(0, n_rows // ROW_BLOCK, step, 0, unroll=16)


def _norm_rows_to(h_scr, x_ref, g_ref):
    g = g_ref[...]

    def body(rows):
        h_scr[rows, :] = _rms(x_ref[rows, :], g).astype(h_scr.dtype)

    _for_row_blocks(x_ref.shape[0], body)


def _gelu_tanh(v):
    return 0.5 * v * (1.0 + jnp.tanh(math.sqrt(2.0 / math.pi) * (v + 0.044715 * (v * v * v))))


def _sigmoid(v):
    return 0.5 * jnp.tanh(0.5 * v) + 0.5


def _any_eq(idx, values):
    hit = idx == values[0]
    for v in values[1:]:
        hit = jnp.logical_or(hit, idx == v)
    return hit


def _resident(shape, index_map):
    return pl.BlockSpec(shape, index_map, pipeline_mode=pl.Buffered(1))


def _ffn_kernel(*refs, nj, in_tiles, out_tiles):
    n_in, n_out = len(in_tiles), len(out_tiles)
    x_refs = refs[:n_in]
    gpre_ref, gpost_ref, w1_ref, w3_ref, w2_ref = refs[n_in:n_in + 5]
    o_refs = refs[n_in + 5:n_in + 5 + n_out]
    h_scr, acc_scr = refs[n_in + 5 + n_out:]
    i = pl.program_id(0)
    j = pl.program_id(1)

    def active(bounds, k):
        lo = sum(bounds[:k])
        return jnp.logical_and(i >= lo, i < lo + bounds[k])

    for k in range(n_in):
        @pl.when(jnp.logical_and(j == 0, active(in_tiles, k)))
        def _(k=k):
            _norm_rows_to(h_scr, x_refs[k], gpre_ref)
            acc_scr[...] = jnp.zeros_like(acc_scr)

    h = h_scr[...]
    a = jnp.dot(h, w1_ref[...], preferred_element_type=F32)
    b = jnp.dot(h, w3_ref[...], preferred_element_type=F32)
    g = (a * _sigmoid(a) * b).astype(BF16)
    acc_scr[...] += jnp.dot(g, w2_ref[...], preferred_element_type=F32)

    for ki in range(n_in):
        for ko in range(n_out):
            @pl.when(jnp.logical_and(j == nj - 1, jnp.logical_and(active(in_tiles, ki), active(out_tiles, ko))))
            def _(ki=ki, ko=ko):
                g_half = 0.5 * gpost_ref[...]

                def finish(rows):
                    o_refs[ko][rows, :] = x_refs[ki][rows, :] + _rms(acc_scr[rows, :], g_half)

                _for_row_blocks(acc_scr.shape[0], finish)


def _ffn(xs, g_pre, g_post, w1, w3, w2, layer, which, out_rows):
    d = xs[0].shape[1]
    dff = w1.shape[-1]
    tm, tf = 512, 512
    nj = dff // tf
    in_tiles = tuple(x.shape[0] // tm for x in xs)
    out_tiles = tuple(r // tm for r in out_rows)

    def piece(bounds, k):
        lo = sum(bounds[:k])
        return lambda i, j: (jnp.clip(i - lo, 0, bounds[k] - 1), 0)

    const = lambda i, j: (0, 0)
    wcol = pl.BlockSpec((None, None, d, tf), lambda i, j: (layer, which, 0, j))
    outs = pl.pallas_call(
        functools.partial(_ffn_kernel, nj=nj, in_tiles=in_tiles, out_tiles=out_tiles),
        grid=(sum(in_tiles), nj),
        in_specs=[pl.BlockSpec((tm, d), piece(in_tiles, k)) for k in range(len(xs))]
        + [pl.BlockSpec((1, d), const), pl.BlockSpec((1, d), const), wcol, wcol,
           pl.BlockSpec((None, None, tf, d), lambda i, j: (layer, which, j, 0))],
        out_specs=[pl.BlockSpec((tm, d), piece(out_tiles, k)) for k in range(len(out_rows))],
        out_shape=[jax.ShapeDtypeStruct((r, d), F32) for r in out_rows],
        scratch_shapes=[pltpu.VMEM((tm, d), BF16), pltpu.VMEM((tm, d), F32)],
        compiler_params=_cparams(("parallel", "arbitrary"),
                                 VMEM_LIMIT + (len(xs) + len(out_rows) - 2) * 2 * tm * d * 4),
        name="ffn",
    )(*xs, g_pre.reshape(1, d), g_post.reshape(1, d), w1, w3, w2)
    return list(outs)


def _norm_mm_kernel(x_ref, g_ref, w_ref, gates_ref, e_ref, h_scr, *, ngate):
    j = pl.program_id(1)

    @pl.when(j == 0)
    def _():
        _norm_rows_to(h_scr, x_ref, g_ref)

    @pl.when(j < ngate)
    def _():
        gates_ref[...] = jnp.dot(h_scr[...], w_ref[...], preferred_element_type=F32).astype(gates_ref.dtype)

    @pl.when(j >= ngate)
    def _():
        e_ref[...] = jnp.dot(h_scr[...], w_ref[...], preferred_element_type=F32)


def _proj_elementwise(x, g, w_in, layer):
    n, d = x.shape
    tm, tn = 512, E_TILE
    lead = COL_QKV // tn
    skip = 3 * ATT_WIDTH // tn
    ncol = (w_in.shape[-1] - 3 * ATT_WIDTH) // tn
    ngate = ncol - lead
    return pl.pallas_call(
        functools.partial(_norm_mm_kernel, ngate=ngate),
        grid=(n // tm, ncol),
        in_specs=[pl.BlockSpec((tm, d), lambda i, j: (i, 0)), pl.BlockSpec((1, d), lambda i, j: (0, 0)),
                  pl.BlockSpec((None, d, tn),
                               lambda i, j: (layer, 0, jnp.where(j < ngate, j + lead + skip, j - ngate)))],
        out_specs=[pl.BlockSpec((tm, tn), lambda i, j: (i, jnp.minimum(j, ngate - 1))),
                   pl.BlockSpec((tm, tn), lambda i, j: (i, jnp.maximum(j - ngate, 0)))],
        out_shape=[jax.ShapeDtypeStruct((n, ngate * tn), BF16), jax.ShapeDtypeStruct((n, lead * tn), F32)],
        scratch_shapes=[pltpu.VMEM((tm, d), BF16)],
        compiler_params=_cparams(("parallel", "arbitrary")),
        name="mixer_in",
    )(x, g.reshape(1, d), w_in)


def _qkv_kernel(x_ref, g_ref, wq_ref, wk_ref, wv_ref, o0_ref, o1_ref, o2_ref, h_scr, res_scr, *, tm):
    step = pl.program_id(1)
    o_refs = (o0_ref, o1_ref, o2_ref)
    order = tuple(reversed(range(len(ATT_GROUPS))))
    ntile = 3 * ATT_OUT // LANES

    @pl.when(step == 0)
    def _():
        _norm_rows_to(h_scr, x_ref, g_ref)

    def project():
        h = h_scr[...]
        q = jnp.dot(h, wq_ref[...], preferred_element_type=F32) * (HEAD_DIM ** -0.5)
        k = jnp.dot(h, wk_ref[...], preferred_element_type=F32)
        v = jnp.dot(h, wv_ref[...], preferred_element_type=F32)
        return jnp.concatenate([q, k, v], axis=-1)

    def write_residue_major(slot, gi):
        dil = ATT_GROUPS[gi][1]
        rows = tm // dil
        for r in range(dil):
            o_refs[gi][r] = jnp.concatenate(
                [res_scr[slot, c, pl.ds(r, rows, stride=dil), :] for c in range(ntile)], axis=-1).astype(BF16)

    for s, gi in enumerate(order):
        @pl.when(step == s)
        def _(s=s, gi=gi):
            res = project()
            if s > 0 and ATT_GROUPS[order[s - 1]][1] > 1:
                write_residue_major((s - 1) % 2, order[s - 1])
            if ATT_GROUPS[gi][1] == 1:
                o_refs[gi][0] = res.astype(BF16)
            else:
                for c in range(ntile):
                    res_scr[s % 2, c] = res[:, c * LANES:(c + 1) * LANES]


def _proj_qkv(x, g, w_in, layer):
    n, d = x.shape
    tm = 512
    qb = COL_QKV // ATT_OUT

    ng = len(ATT_GROUPS)
    assert ATT_GROUPS[0][1] == 1

    def wspec(off):
        return pl.BlockSpec((None, d, ATT_OUT), lambda i, s: (layer, 0, qb + off + ng - 1 - s))

    return pl.pallas_call(
        functools.partial(_qkv_kernel, tm=tm),
        grid=(n // tm, ng),
        in_specs=[pl.BlockSpec((tm, d), lambda i, gq: (i, 0)), pl.BlockSpec((1, d), lambda i, gq: (0, 0)),
                  wspec(0), wspec(ng), wspec(2 * ng)],
        out_specs=[pl.BlockSpec((dil, tm // dil, 3 * ATT_OUT), lambda i, gq: (0, i, 0)) for _, dil in ATT_GROUPS],
        out_shape=[jax.ShapeDtypeStruct((dil, n // dil, 3 * ATT_OUT), BF16) for _, dil in ATT_GROUPS],
        scratch_shapes=[pltpu.VMEM((tm, d), BF16), pltpu.VMEM((2, 3 * ATT_OUT // LANES, tm, LANES), F32)],
        compiler_params=_cparams(("parallel", "arbitrary")),
        name="mixer_qkv",
    )(x, g.reshape(1, d), w_in, w_in, w_in)


def _lru_kernel(*refs, tc, nt, seg_first, seg_last, d):
    xm_ref, xp_ref, xn_ref, cw_ref, cb_ref, wg_ref, bg_ref, lam_ref = refs[:8]
    if d == 0:
        o_ref, xpad_scr, a_scr, b_scr, hl_scr, p_scr, h_scr = refs[8:]
    else:
        hf_ref, gl_ref, o_ref, xpad_scr, a_scr, b_scr, hl_scr, p_scr, h_scr = refs[8:]
    i = pl.program_id(0)
    ti = i if d == 0 else nt - 1 - i
    w = LRU_WIDTH
    at_first = _any_eq(ti, seg_first)
    at_last = _any_eq(ti, seg_last)

    xpad_scr[pl.ds(0, SUBLANES), :] = xp_ref[...] * jnp.where(at_first, 0.0, 1.0)
    xpad_scr[pl.ds(SUBLANES, tc), :] = xm_ref[...]
    xpad_scr[pl.ds(SUBLANES + tc, SUBLANES), :] = xn_ref[...] * jnp.where(at_last, 0.0, 1.0)
    left = LRU_CONV_W // 2
    xc = cb_ref[...] + xpad_scr[pl.ds(SUBLANES - left, tc), :] * cw_ref[pl.ds(0, 1), :]
    for kk in range(1, LRU_CONV_W):
        xc = xc + xpad_scr[pl.ds(SUBLANES - left + kk, tc), :] * cw_ref[pl.ds(kk, 1), :]

    xcb = xc.astype(BF16)
    parts = [jnp.dot(xcb[:, p * LANES:(p + 1) * LANES], wg_ref[p], preferred_element_type=F32)
             for p in range(w // LANES)]
    r = _sigmoid(jnp.concatenate([g[:, :LANES] for g in parts], axis=-1) + bg_ref[:, :w])
    ig = _sigmoid(jnp.concatenate([g[:, LANES:] for g in parts], axis=-1) + bg_ref[:, w:])
    nlam = -lam_ref[...]
    softplus = jnp.maximum(nlam, 0.0) + jnp.log(1.0 + jnp.exp(-jnp.abs(nlam)))
    a = jnp.exp(-LRU_C * r * softplus)
    bb = jnp.sqrt(1.0 - a * a) * (ig * xc)

    sub = tc // SUBLANES
    pitch = sub + SUBLANES
    ntile = w // LANES
    for c in range(ntile):
        for jj in range(SUBLANES):
            a_scr[c, pl.ds(jj * pitch, sub), :] = a[jj * sub:(jj + 1) * sub, c * LANES:(c + 1) * LANES]
            b_scr[c, pl.ds(jj * pitch, sub), :] = bb[jj * sub:(jj + 1) * sub, c * LANES:(c + 1) * LANES]

    def strided_rows(ref, k):
        return jnp.concatenate([ref[c, pl.ds(k, SUBLANES, stride=pitch), :] for c in range(ntile)], axis=-1)

    @pl.when(at_first if d == 0 else at_last)
    def _():
        h_scr[...] = jnp.zeros_like(h_scr)

    sub_id = lax.broadcasted_iota(jnp.int32, (SUBLANES, w), 0)
    hl = jnp.zeros((SUBLANES, w), F32)
    pp = jnp.ones((SUBLANES, w), F32)
    for k in (range(sub) if d == 0 else range(sub - 1, -1, -1)):
        av = strided_rows(a_scr, k)
        bv = strided_rows(b_scr, k)
        hl = av * hl + bv
        pp = av * pp
        hl_scr[pl.ds(k * SUBLANES, SUBLANES), :] = hl
        p_scr[pl.ds(k * SUBLANES, SUBLANES), :] = pp
    cur = h_scr[pl.ds(0, 1), :]
    carry = jnp.zeros((SUBLANES, w), F32)
    for jj in (range(SUBLANES) if d == 0 else range(SUBLANES - 1, -1, -1)):
        carry = jnp.where(sub_id == jj, cur, carry)
        cur = hl[jj:jj + 1, :] + pp[jj:jj + 1, :] * cur
    h_scr[pl.ds(0, 1), :] = cur
    for k in range(sub):
        rows = pl.ds(k * SUBLANES, SUBLANES)
        hv = hl_scr[rows, :] + p_scr[rows, :] * carry
        for c in range(ntile):
            a_scr[c, pl.ds(k, SUBLANES, stride=pitch), :] = hv[:, c * LANES:(c + 1) * LANES]
    for jj in range(SUBLANES):
        rows = pl.ds(jj * sub, sub)
        hv = jnp.concatenate([a_scr[c, pl.ds(jj * pitch, sub), :] for c in range(ntile)], axis=-1)
        if d == 0:
            o_ref[rows, :] = hv
        else:
            o_ref[rows, :] = ((hf_ref[rows, :] + hv) * _gelu_tanh(gl_ref[rows, :])).astype(o_ref.dtype)


def _lru(ze, conv_w, conv_b, wg, bg, lam, seg_lens):
    n = ze.shape[0]
    w = LRU_WIDTH
    tc = 256
    nt = n // tc
    starts = np.cumsum((0,) + tuple(seg_lens))
    seg_first = tuple(int(s) // tc for s in starts[:-1])
    seg_last = tuple(int(s) // tc - 1 for s in starts[1:])
    hb = tc // SUBLANES
    nhb = n // SUBLANES
    pitched = SUBLANES * (tc // SUBLANES + SUBLANES)

    def one_direction(d, extra_specs, extra_args, out_dtype):
        tile = (lambda i: i) if d == 0 else (lambda i: nt - 1 - i)
        return pl.pallas_call(
            functools.partial(_lru_kernel, tc=tc, nt=nt, seg_first=seg_first, seg_last=seg_last, d=d),
            grid=(nt,),
            in_specs=[pl.BlockSpec((tc, w), lambda i: (tile(i), E_XL)),
                      pl.BlockSpec((SUBLANES, w), lambda i: (jnp.maximum(tile(i) * hb - 1, 0), E_XL)),
                      pl.BlockSpec((SUBLANES, w), lambda i: (jnp.minimum((tile(i) + 1) * hb, nhb - 1), E_XL)),
                      pl.BlockSpec((LRU_CONV_W, w), lambda i: (0, 0)),
                      pl.BlockSpec((1, w), lambda i: (0, 0)),
                      _resident((None, w // LANES, LANES, 2 * LANES), lambda i: (d, 0, 0, 0)),
                      pl.BlockSpec((None, 1, 2 * w), lambda i: (d, 0, 0)),
                      pl.BlockSpec((None, 1, w), lambda i: (d, 0, 0))] + [spec(tile) for spec in extra_specs],
            out_specs=pl.BlockSpec((tc, w), lambda i: (tile(i), 0)),
            out_shape=jax.ShapeDtypeStruct((n, w), out_dtype),
            scratch_shapes=[pltpu.VMEM((tc + 2 * SUBLANES, w), F32), pltpu.VMEM((w // LANES, pitched, LANES), F32),
                            pltpu.VMEM((w // LANES, pitched, LANES), F32), pltpu.VMEM((tc, w), F32),
                            pltpu.VMEM((tc, w), F32), pltpu.VMEM((SUBLANES, w), F32)],
            compiler_params=_cparams(("arbitrary",)),
            name="rglru_fwd" if d == 0 else "rglru_bwd",
        )(ze, ze, ze, conv_w, conv_b.reshape(1, w), wg, bg, lam, *extra_args)

    hf = one_direction(0, [], [], F32)
    return one_direction(1, [lambda tile: pl.BlockSpec((tc, w), lambda i: (tile(i), 0)),
                             lambda tile: pl.BlockSpec((tc, w), lambda i: (tile(i), E_GL))], [hf, ze], BF16)


def _lru_gate_weights(wa, ba, wx, bx):
    per_tile = LANES * LRU_BLOCKS // LRU_WIDTH

    def tiles(wb):
        bw = wb.shape[-1]
        eye = jnp.eye(per_tile, dtype=wb.dtype)
        grouped = wb.reshape(2, LRU_BLOCKS // per_tile, per_tile, bw, bw)
        full = grouped[:, :, :, :, None, :] * eye[None, None, :, None, :, None]
        return full.reshape(2, LRU_BLOCKS // per_tile, LANES, LANES)

    wg = jnp.concatenate([tiles(wa), tiles(wx)], axis=-1).astype(BF16)
    bg = jnp.concatenate([ba, bx], axis=-1).reshape(2, 1, 2 * LRU_WIDTH)
    return wg, bg


S5_GROUP_BATCH = 8


def _s5_ktable_kernel(b_ref, w_ref, o_ref):
    for i in range(S5_GROUP_BATCH):
        o_ref[i] = jnp.dot(b_ref[i], w_ref[i], preferred_element_type=F32, precision=lax.Precision.HIGHEST)


def _s5_tables(lam_re, lam_im, log_dt, b_re, b_im, c_re, c_im):
    L, G, P, C = S5_CHUNK, S5_GROUPS, S5_STATE, S5_GROUP_CH
    dt = jnp.exp(log_dt)[..., None]
    mag = jnp.exp(lam_re * dt)
    ar = mag * jnp.cos(lam_im * dt)
    ai = mag * jnp.sin(lam_im * dt)
    den = lam_re * lam_re + lam_im * lam_im
    cr = ((ar - 1.0) * lam_re + ai * lam_im) / den
    ci = (ai * lam_re - (ar - 1.0) * lam_im) / den
    bbr = cr[..., None] * b_re - ci[..., None] * b_im
    bbi = cr[..., None] * b_im + ci[..., None] * b_re
    pr, pi = jnp.ones_like(ar)[None], jnp.zeros_like(ai)[None]
    nr, ni = ar, ai
    while pr.shape[0] < L + 1:
        pr, pi = (jnp.concatenate([pr, pr * nr - pi * ni], axis=0),
                  jnp.concatenate([pi, pr * ni + pi * nr], axis=0))
        nr, ni = nr * nr - ni * ni, 2.0 * nr * ni
    pr, pi = pr[:L + 1], pi[:L + 1]
    zr = pr[:L, ..., None] * bbr - pi[:L, ..., None] * bbi
    zi = pr[:L, ..., None] * bbi + pi[:L, ..., None] * bbr

    def c_pow(powers_r, powers_i):
        ctr = jnp.transpose(c_re, (0, 1, 3, 2))[:, :, :, None, :]
        cti = jnp.transpose(c_im, (0, 1, 3, 2))[:, :, :, None, :]
        qr = jnp.transpose(powers_r, (1, 2, 3, 0))[..., None]
        qi = jnp.transpose(powers_i, (1, 2, 3, 0))[..., None]
        return ctr * qr - cti * qi, ctr * qi + cti * qr

    wr, wi = c_pow(pr[:L], pi[:L])
    wmat = jnp.concatenate([wr, wi], axis=2).reshape(2 * G, 2 * P, L * C)
    bmat = jnp.concatenate([jnp.transpose(bbr, (0, 1, 3, 2)), -jnp.transpose(bbi, (0, 1, 3, 2))],
                           axis=-1).reshape(2 * G, C, 2 * P)

    def w_in(z, flip):
        zf = z[::-1] if flip else z
        return jnp.transpose(zf, (1, 0, 3, 2)).reshape(G, L * C, P)

    win = jnp.concatenate([w_in(zr[:, 0], True), w_in(zr[:, 1], False),
                           w_in(zi[:, 0], True), w_in(zi[:, 1], False)], axis=-1).astype(BF16)

    fr, fi = c_pow(pr[1:], pi[1:])
    br, bi = c_pow(pr[:0:-1], pi[:0:-1])
    wout = jnp.concatenate([fr[0], br[1], -fi[0], -bi[1]], axis=1).reshape(G, 4 * P, L * C).astype(BF16)
    al = jnp.concatenate([pr[L, 0], pr[L, 1], pi[L, 0], pi[L, 1]], axis=-1)
    return bmat, wmat, win, wout, al


def _s5_lag_table(kt):
    L, G, C = S5_CHUNK, S5_GROUPS, S5_GROUP_CH
    kk = jnp.concatenate([kt[1, :, :, :0:-1], kt[0, :, :, :1] + kt[1, :, :, :1], kt[0, :, :, 1:]], axis=2)
    kk = kk.reshape(G, C, (2 * L - 1) * C)
    return jnp.pad(kk, ((0, 0), (0, 0), (0, 2 * L * C - kk.shape[-1])))


def _s5_prepare(lam_re, lam_im, log_dt, b_re, b_im, c_re, c_im):
    L, G, P, C = S5_CHUNK, S5_GROUPS, S5_STATE, S5_GROUP_CH
    bmat, wmat, win, wout, al = _s5_tables(lam_re, lam_im, log_dt, b_re, b_im, c_re, c_im)
    kt = pl.pallas_call(
        _s5_ktable_kernel,
        grid=(2 * G // S5_GROUP_BATCH,),
        in_specs=[pl.BlockSpec((S5_GROUP_BATCH, C, 2 * P), lambda g: (g, 0, 0)),
                  pl.BlockSpec((S5_GROUP_BATCH, 2 * P, L * C), lambda g: (g, 0, 0))],
        out_specs=pl.BlockSpec((S5_GROUP_BATCH, C, L * C), lambda g: (g, 0, 0)),
        out_shape=jax.ShapeDtypeStruct((2 * G, C, L * C), F32),
        compiler_params=_cparams(("parallel",)),
        name="s5_ktable",
    )(bmat, wmat)
    return _s5_lag_table(kt.reshape(2, G, C, L, C)), win, wout, al


def _s5_state_kernel(v_ref, win_ref, o_ref):
    width = win_ref.shape[-1]
    for i in range(S5_GROUP_BATCH):
        o_ref[:, i * width:(i + 1) * width] = jnp.dot(v_ref[i], win_ref[i], preferred_element_type=F32)


def _s5_scan_kernel(s_ref, al_ref, o_ref, *, seg_chunks):
    p2 = 2 * S5_STATE
    alr = al_ref[:, :p2]
    ali = al_ref[:, p2:]
    is_fwd = lax.broadcasted_iota(jnp.int32, alr.shape, 1) < S5_STATE
    zero = jnp.zeros_like(alr)
    start = 0
    for n_chunks in seg_chunks:
        def fwd(k, carry, start=start):
            xr, xi = carry
            c = start + k
            o_ref[c, :, :p2] = xr
            o_ref[c, :, p2:] = xi
            sr = s_ref[c, :, :p2]
            si = s_ref[c, :, p2:]
            return alr * xr - ali * xi + sr, alr * xi + ali * xr + si

        lax.fori_loop(0, n_chunks, fwd, (zero, zero))

        def bwd(k, carry, start=start, n_chunks=n_chunks):
            xr, xi = carry
            c = start + n_chunks - 1 - k
            o_ref[c, :, :p2] = jnp.where(is_fwd, o_ref[c, :, :p2], xr)
            o_ref[c, :, p2:] = jnp.where(is_fwd, o_ref[c, :, p2:], xi)
            sr = s_ref[c, :, :p2]
            si = s_ref[c, :, p2:]
            return alr * xr - ali * xi + sr, alr * xi + ali * xr + si

        lax.fori_loop(0, n_chunks, bwd, (zero, zero))
        start += n_chunks


def _s5_out_kernel(v_ref, kk_ref, x_ref, wout_ref, o_ref, mt_scr):
    L, C = S5_CHUNK, S5_GROUP_CH
    kk = kk_ref[...]
    width = kk.shape[-1]
    per_tile = LANES // C
    for rot in range(per_tile):
        shifted = kk if rot == 0 else pltpu.roll(kk, width - rot * C, axis=1)
        shifted = shifted.astype(BF16)
        for s in range(L):
            lag0 = L - 1 - s
            if lag0 % per_tile == rot:
                col = (lag0 // per_tile) * LANES
                mt_scr[pl.ds(s * C, C), :] = shifted[:, col:col + L * C]
    o_ref[...] = (jnp.dot(v_ref[...], mt_scr[...], preferred_element_type=F32)
                  + jnp.dot(x_ref[...].astype(BF16), wout_ref[...], preferred_element_type=F32)).astype(o_ref.dtype)


def _s5(u, tables, seg_lens):
    kk, win, wout, al = tables
    L, G, P, C = S5_CHUNK, S5_GROUPS, S5_STATE, S5_GROUP_CH
    n = u.shape[0]
    nc = n // L
    lc = L * C
    v = jnp.transpose(u.reshape(nc, L, G, C), (2, 0, 1, 3)).reshape(G, nc, lc).astype(BF16)
    states = pl.pallas_call(
        _s5_state_kernel,
        grid=(G // S5_GROUP_BATCH,),
        in_specs=[pl.BlockSpec((S5_GROUP_BATCH, nc, lc), lambda g: (g, 0, 0)),
                  pl.BlockSpec((S5_GROUP_BATCH, lc, 4 * P), lambda g: (g, 0, 0))],
        out_specs=pl.BlockSpec((nc, S5_GROUP_BATCH * 4 * P), lambda g: (0, g)),
        out_shape=jax.ShapeDtypeStruct((nc, G * 4 * P), F32),
        compiler_params=_cparams(("parallel",)),
        name="s5_chunk_state",
    )(v, win)
    gb = SUBLANES
    carried = pl.pallas_call(
        functools.partial(_s5_scan_kernel, seg_chunks=tuple(t // L for t in seg_lens)),
        grid=(G // gb,),
        in_specs=[pl.BlockSpec((nc, gb, 4 * P), lambda g: (0, g, 0)),
                  pl.BlockSpec((gb, 4 * P), lambda g: (g, 0))],
        out_specs=pl.BlockSpec((nc, gb, 4 * P), lambda g: (0, g, 0)),
        out_shape=jax.ShapeDtypeStruct((nc, G, 4 * P), F32),
        compiler_params=_cparams(("parallel",)),
        name="s5_chunk_scan",
    )(states.reshape(nc, G, 4 * P), al)
    y = pl.pallas_call(
        _s5_out_kernel,
        grid=(G,),
        in_specs=[pl.BlockSpec((None, nc, lc), lambda g: (g, 0, 0)),
                  pl.BlockSpec((None, C, 2 * lc), lambda g: (g, 0, 0)),
                  pl.BlockSpec((nc, 4 * P), lambda g: (0, g)),
                  pl.BlockSpec((None, 4 * P, lc), lambda g: (g, 0, 0))],
        out_specs=pl.BlockSpec((None, nc, lc), lambda g: (g, 0, 0)),
        out_shape=jax.ShapeDtypeStruct((G, nc, lc), BF16),
        scratch_shapes=[pltpu.VMEM((lc, lc), BF16)],
        compiler_params=_cparams(("parallel",)),
        name="s5_chunk_out",
    )(v, kk, carried.reshape(nc, G * 4 * P), wout)
    return jnp.transpose(y.reshape(G, nc, L, C), (1, 2, 0, 3)).reshape(n, G * C)


def _s5_post_kernel(y_ref, u_ref, d_ref, w_ref, b_ref, o_ref):
    y1 = _gelu_tanh(y_ref[...].astype(F32) + d_ref[...] * u_ref[...])
    gate = jnp.dot(y1.astype(BF16), w_ref[...], preferred_element_type=F32) + b_ref[...]
    o_ref[...] = (y1 * _sigmoid(gate)).astype(o_ref.dtype)


def _s5_post(y5, ze, s5_d, glu_w, glu_b, layer):
    n, w = y5.shape
    tm = 512
    return pl.pallas_call(
        _s5_post_kernel,
        grid=(n // tm,),
        in_specs=[pl.BlockSpec((tm, w), lambda i: (i, 0)), pl.BlockSpec((tm, w), lambda i: (i, E_U)),
                  pl.BlockSpec((1, w), lambda i: (0, 0)), pl.BlockSpec((None, w, w), lambda i: (layer, 0, 0)),
                  pl.BlockSpec((1, w), lambda i: (0, 0))],
        out_specs=pl.BlockSpec((tm, w), lambda i: (i, 0)),
        out_shape=jax.ShapeDtypeStruct((n, w), BF16),
        compiler_params=_cparams(("parallel",)),
        name="s5_post",
    )(y5, ze, s5_d.reshape(1, w), glu_w, glu_b.reshape(1, w))


def _t5_buckets(rel):
    half = REL_BUCKETS // 2
    max_exact = half // 2
    sign = (rel > 0).astype(np.int32) * half
    n = np.abs(rel)
    large = max_exact + (np.log(np.maximum(n, 1) / max_exact)
                         / np.log(REL_MAX_DIST / max_exact) * (half - max_exact)).astype(np.int32)
    large = np.minimum(large, half - 1)
    return sign + np.where(n < max_exact, n, large)


def _att_bias_tile(rel_bias, group, dil):
    ncol = Q_BLOCK + 2 * HALF_WIN
    hs = slice(group * HEADS_PER_GROUP, (group + 1) * HEADS_PER_GROUP)
    offs = np.arange(-HALF_WIN, HALF_WIN + 1)
    vals = jnp.transpose(rel_bias[:, hs][_t5_buckets(offs * dil)]).astype(F32)
    width = 2 * ncol
    pad_lo = Q_BLOCK
    line = jnp.pad(vals, ((0, 0), (pad_lo, width - pad_lo - vals.shape[1])), constant_values=NEG_INF)
    rows = jnp.broadcast_to(line[:, None, :], (HEADS_PER_GROUP, Q_BLOCK, width)).reshape(HEADS_PER_GROUP, -1)
    skew = rows[:, :Q_BLOCK * (width - 1)].reshape(HEADS_PER_GROUP, Q_BLOCK, width - 1)
    tile = skew[:, :, pad_lo:pad_lo + ncol]
    col = np.arange(ncol)[None, None, :]
    before = col < HALF_WIN
    after = col >= HALF_WIN + Q_BLOCK
    return jnp.stack([tile, jnp.where(before, NEG_INF, tile), jnp.where(after, NEG_INF, tile),
                      jnp.where(before | after, NEG_INF, tile)])


def _att_kernel(q_ref, kp_ref, km_ref, kn_ref, vp_ref, vm_ref, vn_ref, *rest):
    bias_refs = rest[:ATT_STEP_BLOCKS]
    o_ref, lse_ref = rest[ATT_STEP_BLOCKS:]
    ncol = Q_BLOCK + 2 * HALF_WIN
    main_rows = ATT_STEP_BLOCKS * Q_BLOCK
    low = lax.broadcasted_iota(jnp.int32, (Q_BLOCK, LANES), 1) < HEAD_DIM
    ones = jnp.ones((ncol, LANES), BF16)
    zero = jnp.zeros((Q_BLOCK, LANES), BF16)
    heads = range(HEADS_PER_GROUP)
    pair_cols = [slice((h // 2) * LANES, (h // 2 + 1) * LANES) for h in heads]

    def window(prev_ref, main_ref, next_ref, sb):
        lo = sb * Q_BLOCK - HALF_WIN
        parts = []
        if lo < 0:
            parts.append(prev_ref[...])
        m_lo, m_hi = max(lo, 0), min(lo + ncol, main_rows)
        parts.append(main_ref[pl.ds(m_lo, m_hi - m_lo), :])
        if lo + ncol > main_rows:
            parts.append(next_ref[...])
        return jnp.concatenate(parts, axis=0) if len(parts) > 1 else parts[0]

    for sb in range(ATT_STEP_BLOCKS):
        rows = pl.ds(sb * Q_BLOCK, Q_BLOCK)
        q = q_ref[rows, :]
        k = window(kp_ref, km_ref, kn_ref, sb)
        v = window(vp_ref, vm_ref, vn_ref, sb)
        scores = []
        for h in heads:
            q2 = q[:, pair_cols[h]]
            qh = jnp.where(low, q2, zero) if h % 2 == 0 else jnp.where(low, zero, q2)
            s = lax.dot_general(qh, k[:, pair_cols[h]], (((1,), (1,)), ((), ())), preferred_element_type=F32)
            scores.append(s + bias_refs[sb][h])
        maxes = [jnp.max(s, axis=-1, keepdims=True) for s in scores]
        probs = [jnp.exp(s - m).astype(BF16) for s, m in zip(scores, maxes)]
        outs = [jnp.dot(p, v[:, pair_cols[h]], preferred_element_type=F32) for h, p in zip(heads, probs)]
        sums = [jnp.dot(p, ones, preferred_element_type=F32) for p in probs]
        for pair in range(HEADS_PER_GROUP // 2):
            a, b = 2 * pair, 2 * pair + 1
            l = jnp.where(low, sums[a], sums[b])
            o_ref[rows, pair_cols[a]] = (jnp.where(low, outs[a], outs[b]) / l).astype(o_ref.dtype)
            lse_ref[rows, pair_cols[a]] = jnp.where(low, maxes[a], maxes[b]) + jnp.log(l)


def _attention_group(qkv, bias_tile, dil, seg_lens):
    nd = qkv.shape[1]
    step_rows = ATT_STEP_BLOCKS * Q_BLOCK
    halves_per_step = step_rows // HALF_WIN
    assert nd % step_rows == 0
    nhalf = nd // HALF_WIN
    starts = np.cumsum((0,) + tuple(seg_lens)) // (dil * Q_BLOCK)
    blk_first = tuple(int(s) for s in starts[:-1])
    blk_last = tuple(int(s) - 1 for s in starts[1:])
    ncol = Q_BLOCK + 2 * HALF_WIN

    def variant(blk):
        return _any_eq(blk, blk_first).astype(jnp.int32) + 2 * _any_eq(blk, blk_last).astype(jnp.int32)

    def main(cblk):
        return pl.BlockSpec((None, step_rows, ATT_OUT), lambda r, b: (r, b, cblk))

    def prev(cblk):
        return pl.BlockSpec((None, HALF_WIN, ATT_OUT),
                            lambda r, b: (r, jnp.maximum(halves_per_step * b - 1, 0), cblk))

    def nxt(cblk):
        return pl.BlockSpec((None, HALF_WIN, ATT_OUT),
                            lambda r, b: (r, jnp.minimum(halves_per_step * (b + 1), nhalf - 1), cblk))

    bias_specs = [pl.BlockSpec((None, HEADS_PER_GROUP, Q_BLOCK, ncol),
                               lambda r, b, sb=sb: (variant(ATT_STEP_BLOCKS * b + sb), 0, 0, 0))
                  for sb in range(ATT_STEP_BLOCKS)]
    out_spec = pl.BlockSpec((None, step_rows, ATT_OUT), lambda r, b: (r, b, 0))
    return pl.pallas_call(
        _att_kernel,
        grid=(dil, nd // step_rows),
        in_specs=[main(0), prev(1), main(1), nxt(1), prev(2), main(2), nxt(2)] + bias_specs,
        out_specs=[out_spec, out_spec],
        out_shape=[jax.ShapeDtypeStruct((dil, nd, ATT_OUT), BF16), jax.ShapeDtypeStruct((dil, nd, ATT_OUT), F32)],
        compiler_params=_cparams(("parallel", "parallel")),
        name=f"attention_d{dil}",
    )(qkv, qkv, qkv, qkv, qkv, qkv, qkv, *([bias_tile] * ATT_STEP_BLOCKS))


def _merge_kernel(yl_ref, ys_ref, o0_ref, l0_ref, o1_ref, l1_ref, o2_ref, l2_ref,
                  ga_ref, gb_ref, gc_ref, wl_ref, ws_ref, wa_ref, m_ref, o1_scr, l1_scr, o2_scr, l2_scr, *, tm):
    ntile = ATT_OUT // LANES

    def sequence_order(src, dst, dil):
        rows = tm // dil
        for r in range(dil):
            blk = src[r].astype(F32)
            for c in range(ntile):
                dst[c, pl.ds(r, rows, stride=dil), :] = blk[:, c * LANES:(c + 1) * LANES]
        return jnp.concatenate([dst[c] for c in range(ntile)], axis=-1)

    def gate(ref):
        return _sigmoid(ref[...].astype(F32))

    part = (gate(ga_ref) * jnp.dot(yl_ref[...], wl_ref[...], preferred_element_type=F32)
            + gate(gb_ref) * jnp.dot(ys_ref[...], ws_ref[...], preferred_element_type=F32))

    o1 = sequence_order(o1_ref, o1_scr, ATT_GROUPS[1][1])
    l1 = sequence_order(l1_ref, l1_scr, ATT_GROUPS[1][1])
    o2 = sequence_order(o2_ref, o2_scr, ATT_GROUPS[2][1])
    l2 = sequence_order(l2_ref, l2_scr, ATT_GROUPS[2][1])
    l0 = l0_ref[...]
    mx = jnp.maximum(jnp.maximum(l0, l1), l2)
    e0, e1, e2 = jnp.exp(l0 - mx), jnp.exp(l1 - mx), jnp.exp(l2 - mx)
    yatt = ((o0_ref[...].astype(F32) * e0 + o1 * e1 + o2 * e2) / (e0 + e1 + e2)).astype(BF16)
    m_ref[...] = (part + gate(gc_ref) * jnp.dot(yatt, wa_ref[...], preferred_element_type=F32)).astype(m_ref.dtype)


def _merge(ylru, gates, ys5, att, w_br_lru, w_br_s5, w_br_att, layer):
    n = gates.shape[0]
    d = D_MODEL
    tm = 256
    wl, wa = LRU_WIDTH, ATT_OUT
    gspec = [pl.BlockSpec((tm, d), lambda i, c=c: (i, c)) for c in range(3)]
    att_specs, att_args = [], []
    for (_, dil), (o, l) in zip(ATT_GROUPS, att):
        blk = (None, tm, wa) if dil == 1 else (dil, tm // dil, wa)
        att_specs += [pl.BlockSpec(blk, lambda i: (0, i, 0))] * 2
        att_args += [o, l]
    return pl.pallas_call(
        functools.partial(_merge_kernel, tm=tm),
        grid=(n // tm,),
        in_specs=[pl.BlockSpec((tm, wl), lambda i: (i, 0)), pl.BlockSpec((tm, S5_WIDTH), lambda i: (i, 0))]
        + att_specs + gspec
        + [_resident((None, wl, d), lambda i: (layer, 0, 0)), _resident((None, S5_WIDTH, d), lambda i: (layer, 0, 0)),
           _resident((None, wa, d), lambda i: (layer, 0, 0))],
        out_specs=pl.BlockSpec((tm, d), lambda i: (i, 0)),
        out_shape=jax.ShapeDtypeStruct((n, d), BF16),
        scratch_shapes=[pltpu.VMEM((wa // LANES, tm, LANES), F32)] * 4,
        compiler_params=_cparams(("parallel",)),
        name="merge",
    )(ylru, ys5, *att_args, gates, gates, gates, w_br_lru, w_br_s5, w_br_att)


def _out_proj_kernel(x_ref, m_ref, w_ref, g_ref, o_ref):
    mix = jnp.dot(m_ref[...], w_ref[...], preferred_element_type=F32)
    o_ref[...] = x_ref[...] + _rms(mix, g_ref[...])


def _out_proj(x, m, w_out, g, layer):
    n, d = x.shape
    tm = 512
    return pl.pallas_call(
        _out_proj_kernel,
        grid=(n // tm,),
        in_specs=[pl.BlockSpec((tm, d), lambda i: (i, 0)), pl.BlockSpec((tm, d), lambda i: (i, 0)),
                  _resident((None, d, d), lambda i: (layer, 0, 0)), pl.BlockSpec((1, d), lambda i: (0, 0))],
        out_specs=pl.BlockSpec((tm, d), lambda i: (i, 0)),
        out_shape=jax.ShapeDtypeStruct((n, d), F32),
        compiler_params=_cparams(("parallel",)),
        name="mixer_out",
    )(x, m, w_out, g.reshape(1, d))


def _mixer(x, g_pre, g_post, seg_lens, layer, w_in, conv_w, conv_b, lru_wa, lru_ba, lru_wx, lru_bx, lru_L,
           s5_params, s5_d, glu_w, glu_b, w_br_lru, w_br_s5, w_br_att, w_out, bias_tiles):
    gates, ze = _proj_elementwise(x, g_pre, w_in, layer)
    qkv = _proj_qkv(x, g_pre, w_in, layer)

    wg, bg = _lru_gate_weights(lru_wa, lru_ba, lru_wx, lru_bx)
    ylru = _lru(ze, conv_w, conv_b, wg, bg, lru_L.reshape(2, 1, LRU_WIDTH), seg_lens)

    y5 = _s5(ze[:, E_U * S5_WIDTH:(E_U + 1) * S5_WIDTH], _s5_prepare(*s5_params), seg_lens)
    ys5 = _s5_post(y5, ze, s5_d, glu_w, glu_b, layer)

    att = [_attention_group(qkv[g], bias_tiles[g], dil, seg_lens) for g, (_, dil) in enumerate(ATT_GROUPS)]

    m = _merge(ylru, gates, ys5, att, w_br_lru, w_br_s5, w_br_att, layer)
    return _out_proj(x, m, w_out, g_post, layer)


def kernel(x_prompt, x_sample, norm_g, w_in, lru_conv_w, lru_conv_b, lru_wa, lru_ba, lru_wx, lru_bx, lru_L,
           s5_lam_re, s5_lam_im, s5_log_dt, s5_b_re, s5_b_im, s5_c_re, s5_c_im, s5_d, s5_glu_w, s5_glu_b,
           rel_bias, w_br_lru, w_br_s5, w_br_att, w_out, ffn_w1, ffn_w3, ffn_w2):
    bp, tp, d = x_prompt.shape
    bs, ts, _ = x_sample.shape
    seg_lens = (tp,) * bp + (ts,) * bs
    rows = (bp * tp, bs * ts)
    n = sum(rows)
    depth = norm_g.shape[0]
    w1, w3, w2 = ffn_w1.astype(BF16), ffn_w3.astype(BF16), ffn_w2.astype(BF16)
    w_in_b, glu_b16, w_out_b = w_in.astype(BF16), s5_glu_w.astype(BF16), w_out.astype(BF16)
    wbl, wbs, wba = w_br_lru.astype(BF16), w_br_s5.astype(BF16), w_br_att.astype(BF16)
    bias_tiles = [_att_bias_tile(rel_bias, g, dil) for g, (_, dil) in enumerate(ATT_GROUPS)]
    s5_stack = (s5_lam_re, s5_lam_im, s5_log_dt, s5_b_re, s5_b_im, s5_c_re, s5_c_im)

    xs = [x_prompt.reshape(rows[0], d), x_sample.reshape(rows[1], d)]
    for l in range(depth):
        g = norm_g[l]
        (x,) = _ffn(xs, g[0], g[1], w1, w3, w2, l, 0, (n,))
        x = _mixer(x, g[2], g[3], seg_lens, l, w_in_b, lru_conv_w[l], lru_conv_b[l], lru_wa[l], lru_ba[l],
                   lru_wx[l], lru_bx[l], lru_L[l], [p[l] for p in s5_stack], s5_d[l], glu_b16, s5_glu_b[l],
                   wbl, wbs, wba, w_out_b,
                   bias_tiles)
        xs = _ffn([x], g[4], g[5], w1, w3, w2, l, 1, rows if l == depth - 1 else (n,))
    return (xs[0].reshape(bp, tp, d), xs[1].reshape(bs, ts, d))
```

```python
import functools
import math

import numpy as np
import jax
import jax.numpy as jnp
from jax import lax
from jax.experimental import pallas as pl
from jax.experimental.pallas import tpu as pltpu

F32 = jnp.float32
BF16 = jnp.bfloat16

D_MODEL = 2048
LRU_WIDTH = 1024
LRU_BLOCKS = 16
LRU_CONV_W = 4
LRU_C = 8.0
S5_WIDTH = 1024
S5_GROUP_CH = 16
S5_GROUPS = 64
S5_STATE = 64
HEAD_DIM = 64
ATT_GROUPS = ((128, 1), (512, 4), (2048, 16))
ATT_WIDTH = 1536
HEADS_PER_GROUP = 8
ATT_OUT = 512
REL_BUCKETS = 32
REL_MAX_DIST = 1024
RMS_EPS = 1e-6
NEG_INF = -1e30

Q_BLOCK = 128
HALF_WIN = 64
ATT_STEP_BLOCKS = 4
S5_CHUNK = 64
LANES = 128
SUBLANES = 8
COL_QKV = 3 * 1024
E_TILE = 1536
E_XL, E_GL, E_U = 0, 1, 2
VMEM_LIMIT = 48 * 1024 * 1024


def _cparams(sem, vmem_limit=VMEM_LIMIT):
    return pltpu.CompilerParams(dimension_semantics=sem, vmem_limit_bytes=vmem_limit)


def _rms(v, g):
    width = v.shape[-1]
    sq = v * v
    part = sq[:, :LANES]
    for c in range(1, width // LANES):
        part = part + sq[:, c * LANES:(c + 1) * LANES]
    ms = jnp.sum(part, axis=-1, keepdims=True) * (1.0 / width)
    return v * lax.rsqrt(ms + RMS_EPS) * g


ROW_BLOCK = 16


def _for_row_blocks(n_rows, body):
    def step(t, carry):
        body(pl.ds(pl.multiple_of(t * ROW_BLOCK, ROW_BLOCK), ROW_BLOCK))
        return carry

    lax.fori_loop(0, n_rows // ROW_BLOCK, step, 0, unroll=16)


def _norm_rows_to(h_scr, x_ref, g_ref):
    g = g_ref[...]

    def body(rows):
        h_scr[rows, :] = _rms(x_ref[rows, :], g).astype(h_scr.dtype)

    _for_row_blocks(x_ref.shape[0], body)


def _gelu_tanh(v):
    return 0.5 * v * (1.0 + jnp.tanh(math.sqrt(2.0 / math.pi) * (v + 0.044715 * (v * v * v))))


def _sigmoid(v):
    return 0.5 * jnp.tanh(0.5 * v) + 0.5


def _any_eq(idx, values):
    hit = idx == values[0]
    for v in values[1:]:
        hit = jnp.logical_or(hit, idx == v)
    return hit


def _resident(shape, index_map):
    return pl.BlockSpec(shape, index_map, pipeline_mode=pl.Buffered(1))


def _ffn_kernel(*refs, nj, in_tiles, out_tiles):
    n_in, n_out = len(in_tiles), len(out_tiles)
    x_refs = refs[:n_in]
    gpre_ref, gpost_ref, w1_ref, w3_ref, w2_ref = refs[n_in:n_in + 5]
    o_refs = refs[n_in + 5:n_in + 5 + n_out]
    h_scr, acc_scr = refs[n_in + 5 + n_out:]
    i = pl.program_id(0)
    j = pl.program_id(1)

    def active(bounds, k):
        lo = sum(bounds[:k])
        return jnp.logical_and(i >= lo, i < lo + bounds[k])

    for k in range(n_in):
        @pl.when(jnp.logical_and(j == 0, active(in_tiles, k)))
        def _(k=k):
            _norm_rows_to(h_scr, x_refs[k], gpre_ref)
            acc_scr[...] = jnp.zeros_like(acc_scr)

    h = h_scr[...]
    a = jnp.dot(h, w1_ref[...], preferred_element_type=F32)
    b = jnp.dot(h, w3_ref[...], preferred_element_type=F32)
    g = (a * _sigmoid(a) * b).astype(BF16)
    acc_scr[...] += jnp.dot(g, w2_ref[...], preferred_element_type=F32)

    for ki in range(n_in):
        for ko in range(n_out):
            @pl.when(jnp.logical_and(j == nj - 1, jnp.logical_and(active(in_tiles, ki), active(out_tiles, ko))))
            def _(ki=ki, ko=ko):
                g_half = 0.5 * gpost_ref[...]

                def finish(rows):
                    o_refs[ko][rows, :] = x_refs[ki][rows, :] + _rms(acc_scr[rows, :], g_half)

                _for_row_blocks(acc_scr.shape[0], finish)


def _ffn(xs, g_pre, g_post, w1, w3, w2, layer, which, out_rows):
    d = xs[0].shape[1]
    dff = w1.shape[-1]
    tm, tf = 512, 512
    nj = dff // tf
    in_tiles = tuple(x.shape[0] // tm for x in xs)
    out_tiles = tuple(r // tm for r in out_rows)

    def piece(bounds, k):
        lo = sum(bounds[:k])
        return lambda i, j: (jnp.clip(i - lo, 0, bounds[k] - 1), 0)

    const = lambda i, j: (0, 0)
    wcol = pl.BlockSpec((None, None, d, tf), lambda i, j: (layer, which, 0, j))
    outs = pl.pallas_call(
        functools.partial(_ffn_kernel, nj=nj, in_tiles=in_tiles, out_tiles=out_tiles),
        grid=(sum(in_tiles), nj),
        in_specs=[pl.BlockSpec((tm, d), piece(in_tiles, k)) for k in range(len(xs))]
        + [pl.BlockSpec((1, d), const), pl.BlockSpec((1, d), const), wcol, wcol,
           pl.BlockSpec((None, None, tf, d), lambda i, j: (layer, which, j, 0))],
        out_specs=[pl.BlockSpec((tm, d), piece(out_tiles, k)) for k in range(len(out_rows))],
        out_shape=[jax.ShapeDtypeStruct((r, d), F32) for r in out_rows],
        scratch_shapes=[pltpu.VMEM((tm, d), BF16), pltpu.VMEM((tm, d), F32)],
        compiler_params=_cparams(("parallel", "arbitrary"),
                                 VMEM_LIMIT + (len(xs) + len(out_rows) - 2) * 2 * tm * d * 4),
        name="ffn",
    )(*xs, g_pre.reshape(1, d), g_post.reshape(1, d), w1, w3, w2)
    return list(outs)


def _norm_mm_kernel(x_ref, g_ref, w_ref, gates_ref, e_ref, h_scr, *, ngate):
    j = pl.program_id(1)

    @pl.when(j == 0)
    def _():
        _norm_rows_to(h_scr, x_ref, g_ref)

    @pl.when(j < ngate)
    def _():
        gates_ref[...] = jnp.dot(h_scr[...], w_ref[...], preferred_element_type=F32).astype(gates_ref.dtype)

    @pl.when(j >= ngate)
    def _():
        e_ref[...] = jnp.dot(h_scr[...], w_ref[...], preferred_element_type=F32)


def _proj_elementwise(x, g, w_in, layer):
    n, d = x.shape
    tm, tn = 1024, E_TILE
    vmem = 2 * (tm * d * 4 + d * tn * 2 + tm * tn * (2 + 4)) + tm * d * 2 + (4 << 20)
    lead = COL_QKV // tn
    skip = 3 * ATT_WIDTH // tn
    ncol = (w_in.shape[-1] - 3 * ATT_WIDTH) // tn
    ngate = ncol - lead
    return pl.pallas_call(
        functools.partial(_norm_mm_kernel, ngate=ngate),
        grid=(n // tm, ncol),
        in_specs=[pl.BlockSpec((tm, d), lambda i, j: (i, 0)), pl.BlockSpec((1, d), lambda i, j: (0, 0)),
                  pl.BlockSpec((None, d, tn),
                               lambda i, j: (layer, 0, jnp.where(j < ngate, j + lead + skip, j - ngate)))],
        out_specs=[pl.BlockSpec((tm, tn), lambda i, j: (i, jnp.minimum(j, ngate - 1))),
                   pl.BlockSpec((tm, tn), lambda i, j: (i, jnp.maximum(j - ngate, 0)))],
        out_shape=[jax.ShapeDtypeStruct((n, ngate * tn), BF16), jax.ShapeDtypeStruct((n, lead * tn), F32)],
        scratch_shapes=[pltpu.VMEM((tm, d), BF16)],
        compiler_params=_cparams(("parallel", "arbitrary"), vmem),
        name="mixer_in",
    )(x, g.reshape(1, d), w_in)


def _qkv_kernel(x_ref, g_ref, wq_ref, wk_ref, wv_ref, o0_ref, o1_ref, o2_ref, h_scr, res_scr, *, tm):
    step = pl.program_id(1)
    o_refs = (o0_ref, o1_ref, o2_ref)
    order = tuple(reversed(range(len(ATT_GROUPS))))
    ntile = 3 * ATT_OUT // LANES

    @pl.when(step == 0)
    def _():
        _norm_rows_to(h_scr, x_ref, g_ref)

    def project():
        h = h_scr[...]
        q = jnp.dot(h, wq_ref[...], preferred_element_type=F32) * (HEAD_DIM ** -0.5)
        k = jnp.dot(h, wk_ref[...], preferred_element_type=F32)
        v = jnp.dot(h, wv_ref[...], preferred_element_type=F32)
        return jnp.concatenate([q, k, v], axis=-1)

    def write_residue_major(slot, gi):
        dil = ATT_GROUPS[gi][1]
        rows = tm // dil
        for r in range(dil):
            o_refs[gi][r] = jnp.concatenate(
                [res_scr[slot, c, pl.ds(r, rows, stride=dil), :] for c in range(ntile)], axis=-1).astype(BF16)

    for s, gi in enumerate(order):
        @pl.when(step == s)
        def _(s=s, gi=gi):
            res = project()
            if s > 0 and ATT_GROUPS[order[s - 1]][1] > 1:
                write_residue_major((s - 1) % 2, order[s - 1])
            if ATT_GROUPS[gi][1] == 1:
                o_refs[gi][0] = res.astype(BF16)
            else:
                for c in range(ntile):
                    res_scr[s % 2, c] = res[:, c * LANES:(c + 1) * LANES]


def _proj_qkv(x, g, w_in, layer):
    n, d = x.shape
    tm = 512
    qb = COL_QKV // ATT_OUT

    ng = len(ATT_GROUPS)
    assert ATT_GROUPS[0][1] == 1

    def wspec(off):
        return pl.BlockSpec((None, d, ATT_OUT), lambda i, s: (layer, 0, qb + off + ng - 1 - s))

    return pl.pallas_call(
        functools.partial(_qkv_kernel, tm=tm),
        grid=(n // tm, ng),
        in_specs=[pl.BlockSpec((tm, d), lambda i, gq: (i, 0)), pl.BlockSpec((1, d), lambda i, gq: (0, 0)),
                  wspec(0), wspec(ng), wspec(2 * ng)],
        out_specs=[pl.BlockSpec((dil, tm // dil, 3 * ATT_OUT), lambda i, gq: (0, i, 0)) for _, dil in ATT_GROUPS],
        out_shape=[jax.ShapeDtypeStruct((dil, n // dil, 3 * ATT_OUT), BF16) for _, dil in ATT_GROUPS],
        scratch_shapes=[pltpu.VMEM((tm, d), BF16), pltpu.VMEM((2, 3 * ATT_OUT // LANES, tm, LANES), F32)],
        compiler_params=_cparams(("parallel", "arbitrary")),
        name="mixer_qkv",
    )(x, g.reshape(1, d), w_in, w_in, w_in)


def _lru_kernel(*refs, tc, nt, seg_first, seg_last, d):
    xm_ref, xp_ref, xn_ref, cw_ref, cb_ref, wg_ref, bg_ref, lam_ref = refs[:8]
    if d == 0:
        o_ref, perm_scr, a_scr, b_scr, h_scr = refs[8:]
    else:
        hf_ref, gl_ref, o_ref, perm_scr, a_scr, b_scr, h_scr = refs[8:]
    i = pl.program_id(0)
    ti = i if d == 0 else nt - 1 - i
    w = LRU_WIDTH
    at_first = _any_eq(ti, seg_first)
    at_last = _any_eq(ti, seg_last)

    sub = tc // SUBLANES
    pitch = sub + SUBLANES
    ntile = w // LANES
    for c in range(ntile):
        for jj in range(SUBLANES):
            perm_scr[c, pl.ds(jj * pitch, sub), :] = xm_ref[pl.ds(jj * sub, sub), c * LANES:(c + 1) * LANES]

    def block(k):
        return jnp.concatenate([perm_scr[c, pl.ds(k, SUBLANES, stride=pitch), :] for c in range(ntile)], axis=-1)

    sub_id = lax.broadcasted_iota(jnp.int32, (SUBLANES, w), 0)
    x_blk = [block(k) for k in range(sub)]
    before = xp_ref[...] * jnp.where(at_first, 0.0, 1.0)
    after = xn_ref[...] * jnp.where(at_last, 0.0, 1.0)

    def from_previous_subchunk(blk, edge_row):
        return jnp.where(sub_id == 0, edge_row, pltpu.roll(blk, 1, axis=0))

    def from_next_subchunk(blk, edge_row):
        return jnp.where(sub_id == SUBLANES - 1, edge_row, pltpu.roll(blk, SUBLANES - 1, axis=0))

    def shifted(k, off):
        kk = k + off
        if 0 <= kk < sub:
            return x_blk[kk]
        if kk < 0:
            return from_previous_subchunk(x_blk[kk + sub], before[SUBLANES + kk:SUBLANES + kk + 1, :])
        return from_next_subchunk(x_blk[kk - sub], after[kk - sub:kk - sub + 1, :])

    left = LRU_CONV_W // 2
    taps = [cw_ref[pl.ds(kk, 1), :] for kk in range(LRU_CONV_W)]
    xc = jnp.concatenate(
        [cb_ref[...] + sum(shifted(k, kk - left) * taps[kk] for kk in range(LRU_CONV_W)) for k in range(sub)], axis=0)

    xcb = xc.astype(BF16)
    parts = [jnp.dot(xcb[:, p * LANES:(p + 1) * LANES], wg_ref[p], preferred_element_type=F32)
             for p in range(w // LANES)]
    r = _sigmoid(jnp.concatenate([g[:, :LANES] for g in parts], axis=-1) + bg_ref[:, :w])
    ig = _sigmoid(jnp.concatenate([g[:, LANES:] for g in parts], axis=-1) + bg_ref[:, w:])
    nlam = -lam_ref[...]
    softplus = jnp.maximum(nlam, 0.0) + jnp.log(1.0 + jnp.exp(-jnp.abs(nlam)))
    a = jnp.exp(-LRU_C * r * softplus)
    a_scr[...] = a
    b_scr[...] = jnp.sqrt(1.0 - a * a) * (ig * xc)

    @pl.when(at_first if d == 0 else at_last)
    def _():
        h_scr[...] = jnp.zeros_like(h_scr)

    hl = jnp.zeros((SUBLANES, w), F32)
    pp = jnp.ones((SUBLANES, w), F32)
    for k in (range(sub) if d == 0 else range(sub - 1, -1, -1)):
        rows = pl.ds(k * SUBLANES, SUBLANES)
        av = a_scr[rows, :]
        hl = av * hl + b_scr[rows, :]
        pp = av * pp
        a_scr[rows, :] = pp
        b_scr[rows, :] = hl
    cur = h_scr[pl.ds(0, 1), :]
    carry = jnp.zeros((SUBLANES, w), F32)
    for jj in (range(SUBLANES) if d == 0 else range(SUBLANES - 1, -1, -1)):
        carry = jnp.where(sub_id == jj, cur, carry)
        cur = hl[jj:jj + 1, :] + pp[jj:jj + 1, :] * cur
    h_scr[pl.ds(0, 1), :] = cur
    for k in range(sub):
        rows = pl.ds(k * SUBLANES, SUBLANES)
        hv = b_scr[rows, :] + a_scr[rows, :] * carry
        for c in range(ntile):
            perm_scr[c, pl.ds(k, SUBLANES, stride=pitch), :] = hv[:, c * LANES:(c + 1) * LANES]
    for jj in range(SUBLANES):
        rows = pl.ds(jj * sub, sub)
        hv = jnp.concatenate([perm_scr[c, pl.ds(jj * pitch, sub), :] for c in range(ntile)], axis=-1)
        if d == 0:
            o_ref[rows, :] = hv
        else:
            o_ref[rows, :] = ((hf_ref[rows, :] + hv) * _gelu_tanh(gl_ref[rows, :])).astype(o_ref.dtype)


def _lru(ze, conv_w, conv_b, wg, bg, lam, seg_lens):
    n = ze.shape[0]
    w = LRU_WIDTH
    tc = 256
    nt = n // tc
    starts = np.cumsum((0,) + tuple(seg_lens))
    seg_first = tuple(int(s) // tc for s in starts[:-1])
    seg_last = tuple(int(s) // tc - 1 for s in starts[1:])
    hb = tc // SUBLANES
    nhb = n // SUBLANES
    pitched = SUBLANES * (tc // SUBLANES + SUBLANES)

    def one_direction(d, extra_specs, extra_args, out_dtype):
        tile = (lambda i: i) if d == 0 else (lambda i: nt - 1 - i)
        return pl.pallas_call(
            functools.partial(_lru_kernel, tc=tc, nt=nt, seg_first=seg_first, seg_last=seg_last, d=d),
            grid=(nt,),
            in_specs=[pl.BlockSpec((tc, w), lambda i: (tile(i), E_XL)),
                      pl.BlockSpec((SUBLANES, w), lambda i: (jnp.maximum(tile(i) * hb - 1, 0), E_XL)),
                      pl.BlockSpec((SUBLANES, w), lambda i: (jnp.minimum((tile(i) + 1) * hb, nhb - 1), E_XL)),
                      pl.BlockSpec((LRU_CONV_W, w), lambda i: (0, 0)),
                      pl.BlockSpec((1, w), lambda i: (0, 0)),
                      _resident((None, w // LANES, LANES, 2 * LANES), lambda i: (d, 0, 0, 0)),
                      pl.BlockSpec((None, 1, 2 * w), lambda i: (d, 0, 0)),
                      pl.BlockSpec((None, 1, w), lambda i: (d, 0, 0))] + [spec(tile) for spec in extra_specs],
            out_specs=pl.BlockSpec((tc, w), lambda i: (tile(i), 0)),
            out_shape=jax.ShapeDtypeStruct((n, w), out_dtype),
            scratch_shapes=[pltpu.VMEM((w // LANES, pitched, LANES), F32), pltpu.VMEM((tc, w), F32),
                            pltpu.VMEM((tc, w), F32), pltpu.VMEM((SUBLANES, w), F32)],
            compiler_params=_cparams(("arbitrary",)),
            name="rglru_fwd" if d == 0 else "rglru_bwd",
        )(ze, ze, ze, conv_w, conv_b.reshape(1, w), wg, bg, lam, *extra_args)

    hf = one_direction(0, [], [], F32)
    return one_direction(1, [lambda tile: pl.BlockSpec((tc, w), lambda i: (tile(i), 0)),
                             lambda tile: pl.BlockSpec((tc, w), lambda i: (tile(i), E_GL))], [hf, ze], BF16)


def _lru_gate_weights(wa, ba, wx, bx):
    per_tile = LANES * LRU_BLOCKS // LRU_WIDTH

    def tiles(wb):
        bw = wb.shape[-1]
        eye = jnp.eye(per_tile, dtype=wb.dtype)
        grouped = wb.reshape(2, LRU_BLOCKS // per_tile, per_tile, bw, bw)
        full = grouped[:, :, :, :, None, :] * eye[None, None, :, None, :, None]
        return full.reshape(2, LRU_BLOCKS // per_tile, LANES, LANES)

    wg = jnp.concatenate([tiles(wa), tiles(wx)], axis=-1).astype(BF16)
    bg = jnp.concatenate([ba, bx], axis=-1).reshape(2, 1, 2 * LRU_WIDTH)
    return wg, bg


S5_GROUP_BATCH = 8


def _s5_ktable_kernel(b_ref, w_ref, o_ref):
    for i in range(S5_GROUP_BATCH):
        o_ref[i] = jnp.dot(b_ref[i], w_ref[i], preferred_element_type=F32, precision=lax.Precision.HIGHEST)


def _s5_tables(lam_re, lam_im, log_dt, b_re, b_im, c_re, c_im):
    L, G, P, C = S5_CHUNK, S5_GROUPS, S5_STATE, S5_GROUP_CH
    dt = jnp.exp(log_dt)[..., None]
    mag = jnp.exp(lam_re * dt)
    ar = mag * jnp.cos(lam_im * dt)
    ai = mag * jnp.sin(lam_im * dt)
    den = lam_re * lam_re + lam_im * lam_im
    cr = ((ar - 1.0) * lam_re + ai * lam_im) / den
    ci = (ai * lam_re - (ar - 1.0) * lam_im) / den
    bbr = cr[..., None] * b_re - ci[..., None] * b_im
    bbi = cr[..., None] * b_im + ci[..., None] * b_re
    pr, pi = jnp.ones_like(ar)[None], jnp.zeros_like(ai)[None]
    nr, ni = ar, ai
    while pr.shape[0] < L + 1:
        pr, pi = (jnp.concatenate([pr, pr * nr - pi * ni], axis=0),
                  jnp.concatenate([pi, pr * ni + pi * nr], axis=0))
        nr, ni = nr * nr - ni * ni, 2.0 * nr * ni
    pr, pi = pr[:L + 1], pi[:L + 1]
    zr = pr[:L, ..., None] * bbr - pi[:L, ..., None] * bbi
    zi = pr[:L, ..., None] * bbi + pi[:L, ..., None] * bbr

    def c_pow(powers_r, powers_i):
        ctr = jnp.transpose(c_re, (0, 1, 3, 2))[:, :, :, None, :]
        cti = jnp.transpose(c_im, (0, 1, 3, 2))[:, :, :, None, :]
        qr = jnp.transpose(powers_r, (1, 2, 3, 0))[..., None]
        qi = jnp.transpose(powers_i, (1, 2, 3, 0))[..., None]
        return ctr * qr - cti * qi, ctr * qi + cti * qr

    wr, wi = c_pow(pr[:L], pi[:L])
    wmat = jnp.concatenate([wr, wi], axis=2).reshape(2 * G, 2 * P, L * C)
    bmat = jnp.concatenate([jnp.transpose(bbr, (0, 1, 3, 2)), -jnp.transpose(bbi, (0, 1, 3, 2))],
                           axis=-1).reshape(2 * G, C, 2 * P)

    def w_in(z, flip):
        zf = z[::-1] if flip else z
        return jnp.transpose(zf, (1, 0, 3, 2)).reshape(G, L * C, P)

    win = jnp.concatenate([w_in(zr[:, 0], True), w_in(zr[:, 1], False),
                           w_in(zi[:, 0], True), w_in(zi[:, 1], False)], axis=-1).astype(BF16)

    fr, fi = c_pow(pr[1:], pi[1:])
    br, bi = c_pow(pr[:0:-1], pi[:0:-1])
    wout = jnp.concatenate([fr[0], br[1], -fi[0], -bi[1]], axis=1).reshape(G, 4 * P, L * C).astype(BF16)
    al = jnp.concatenate([pr[L, 0], pr[L, 1], pi[L, 0], pi[L, 1]], axis=-1)
    return bmat, wmat, win, wout, al


def _s5_lag_table(kt):
    L, G, C = S5_CHUNK, S5_GROUPS, S5_GROUP_CH
    kk = jnp.concatenate([kt[1, :, :, :0:-1], kt[0, :, :, :1] + kt[1, :, :, :1], kt[0, :, :, 1:]], axis=2)
    kk = kk.reshape(G, C, (2 * L - 1) * C)
    return jnp.pad(kk, ((0, 0), (0, 0), (0, 2 * L * C - kk.shape[-1])))


def _s5_prepare(lam_re, lam_im, log_dt, b_re, b_im, c_re, c_im):
    L, G, P, C = S5_CHUNK, S5_GROUPS, S5_STATE, S5_GROUP_CH
    bmat, wmat, win, wout, al = _s5_tables(lam_re, lam_im, log_dt, b_re, b_im, c_re, c_im)
    kt = pl.pallas_call(
        _s5_ktable_kernel,
        grid=(2 * G // S5_GROUP_BATCH,),
        in_specs=[pl.BlockSpec((S5_GROUP_BATCH, C, 2 * P), lambda g: (g, 0, 0)),
                  pl.BlockSpec((S5_GROUP_BATCH, 2 * P, L * C), lambda g: (g, 0, 0))],
        out_specs=pl.BlockSpec((S5_GROUP_BATCH, C, L * C), lambda g: (g, 0, 0)),
        out_shape=jax.ShapeDtypeStruct((2 * G, C, L * C), F32),
        compiler_params=_cparams(("parallel",)),
        name="s5_ktable",
    )(bmat, wmat)
    return _s5_lag_table(kt.reshape(2, G, C, L, C)), win, wout, al


def _s5_state_kernel(v_ref, win_ref, o_ref):
    width = win_ref.shape[-1]
    for i in range(S5_GROUP_BATCH):
        o_ref[:, i * width:(i + 1) * width] = jnp.dot(v_ref[i], win_ref[i], preferred_element_type=F32)


def _s5_scan_kernel(s_ref, al_ref, o_ref, *, seg_chunks):
    p2 = 2 * S5_STATE
    alr = al_ref[:, :p2]
    ali = al_ref[:, p2:]
    is_fwd = lax.broadcasted_iota(jnp.int32, alr.shape, 1) < S5_STATE
    zero = jnp.zeros_like(alr)
    start = 0
    for n_chunks in seg_chunks:
        def fwd(k, carry, start=start):
            xr, xi = carry
            c = start + k
            o_ref[c, :, :p2] = xr
            o_ref[c, :, p2:] = xi
            sr = s_ref[c, :, :p2]
            si = s_ref[c, :, p2:]
            return alr * xr - ali * xi + sr, alr * xi + ali * xr + si

        lax.fori_loop(0, n_chunks, fwd, (zero, zero))

        def bwd(k, carry, start=start, n_chunks=n_chunks):
            xr, xi = carry
            c = start + n_chunks - 1 - k
            o_ref[c, :, :p2] = jnp.where(is_fwd, o_ref[c, :, :p2], xr)
            o_ref[c, :, p2:] = jnp.where(is_fwd, o_ref[c, :, p2:], xi)
            sr = s_ref[c, :, :p2]
            si = s_ref[c, :, p2:]
            return alr * xr - ali * xi + sr, alr * xi + ali * xr + si

        lax.fori_loop(0, n_chunks, bwd, (zero, zero))
        start += n_chunks


def _s5_out_kernel(v_ref, kk_ref, x_ref, wout_ref, o_ref, mt_scr):
    L, C = S5_CHUNK, S5_GROUP_CH
    kk = kk_ref[...]
    width = kk.shape[-1]
    per_tile = LANES // C
    for rot in range(per_tile):
        shifted = kk if rot == 0 else pltpu.roll(kk, width - rot * C, axis=1)
        shifted = shifted.astype(BF16)
        for s in range(L):
            lag0 = L - 1 - s
            if lag0 % per_tile == rot:
                col = (lag0 // per_tile) * LANES
                mt_scr[pl.ds(s * C, C), :] = shifted[:, col:col + L * C]
    o_ref[...] = (jnp.dot(v_ref[...], mt_scr[...], preferred_element_type=F32)
                  + jnp.dot(x_ref[...].astype(BF16), wout_ref[...], preferred_element_type=F32)).astype(o_ref.dtype)


def _s5(u, tables, seg_lens):
    kk, win, wout, al = tables
    L, G, P, C = S5_CHUNK, S5_GROUPS, S5_STATE, S5_GROUP_CH
    n = u.shape[0]
    nc = n // L
    lc = L * C
    v = jnp.transpose(u.reshape(nc, L, G, C), (2, 0, 1, 3)).reshape(G, nc, lc).astype(BF16)
    states = pl.pallas_call(
        _s5_state_kernel,
        grid=(G // S5_GROUP_BATCH,),
        in_specs=[pl.BlockSpec((S5_GROUP_BATCH, nc, lc), lambda g: (g, 0, 0)),
                  pl.BlockSpec((S5_GROUP_BATCH, lc, 4 * P), lambda g: (g, 0, 0))],
        out_specs=pl.BlockSpec((nc, S5_GROUP_BATCH * 4 * P), lambda g: (0, g)),
        out_shape=jax.ShapeDtypeStruct((nc, G * 4 * P), F32),
        compiler_params=_cparams(("parallel",)),
        name="s5_chunk_state",
    )(v, win)
    gb = SUBLANES
    carried = pl.pallas_call(
        functools.partial(_s5_scan_kernel, seg_chunks=tuple(t // L for t in seg_lens)),
        grid=(G // gb,),
        in_specs=[pl.BlockSpec((nc, gb, 4 * P), lambda g: (0, g, 0)),
                  pl.BlockSpec((gb, 4 * P), lambda g: (g, 0))],
        out_specs=pl.BlockSpec((nc, gb, 4 * P), lambda g: (0, g, 0)),
        out_shape=jax.ShapeDtypeStruct((nc, G, 4 * P), F32),
        compiler_params=_cparams(("parallel",)),
        name="s5_chunk_scan",
    )(states.reshape(nc, G, 4 * P), al)
    y = pl.pallas_call(
        _s5_out_kernel,
        grid=(G,),
        in_specs=[pl.BlockSpec((None, nc, lc), lambda g: (g, 0, 0)),
                  pl.BlockSpec((None, C, 2 * lc), lambda g: (g, 0, 0)),
                  pl.BlockSpec((nc, 4 * P), lambda g: (0, g)),
                  pl.BlockSpec((None, 4 * P, lc), lambda g: (g, 0, 0))],
        out_specs=pl.BlockSpec((None, nc, lc), lambda g: (g, 0, 0)),
        out_shape=jax.ShapeDtypeStruct((G, nc, lc), BF16),
        scratch_shapes=[pltpu.VMEM((lc, lc), BF16)],
        compiler_params=_cparams(("parallel",)),
        name="s5_chunk_out",
    )(v, kk, carried.reshape(nc, G * 4 * P), wout)
    return jnp.transpose(y.reshape(G, nc, L, C), (1, 2, 0, 3)).reshape(n, G * C)


def _s5_post_kernel(y_ref, u_ref, d_ref, w_ref, b_ref, o_ref):
    y1 = _gelu_tanh(y_ref[...].astype(F32) + d_ref[...] * u_ref[...])
    gate = jnp.dot(y1.astype(BF16), w_ref[...], preferred_element_type=F32) + b_ref[...]
    o_ref[...] = (y1 * _sigmoid(gate)).astype(o_ref.dtype)


def _s5_post(y5, ze, s5_d, glu_w, glu_b, layer):
    n, w = y5.shape
    tm = 512
    return pl.pallas_call(
        _s5_post_kernel,
        grid=(n // tm,),
        in_specs=[pl.BlockSpec((tm, w), lambda i: (i, 0)), pl.BlockSpec((tm, w), lambda i: (i, E_U)),
                  pl.BlockSpec((1, w), lambda i: (0, 0)), pl.BlockSpec((None, w, w), lambda i: (layer, 0, 0)),
                  pl.BlockSpec((1, w), lambda i: (0, 0))],
        out_specs=pl.BlockSpec((tm, w), lambda i: (i, 0)),
        out_shape=jax.ShapeDtypeStruct((n, w), BF16),
        compiler_params=_cparams(("parallel",)),
        name="s5_post",
    )(y5, ze, s5_d.reshape(1, w), glu_w, glu_b.reshape(1, w))


def _t5_buckets(rel):
    half = REL_BUCKETS // 2
    max_exact = half // 2
    sign = (rel > 0).astype(np.int32) * half
    n = np.abs(rel)
    large = max_exact + (np.log(np.maximum(n, 1) / max_exact)
                         / np.log(REL_MAX_DIST / max_exact) * (half - max_exact)).astype(np.int32)
    large = np.minimum(large, half - 1)
    return sign + np.where(n < max_exact, n, large)


def _att_bias_tile(rel_bias, group, dil):
    ncol = Q_BLOCK + 2 * HALF_WIN
    hs = slice(group * HEADS_PER_GROUP, (group + 1) * HEADS_PER_GROUP)
    offs = np.arange(-HALF_WIN, HALF_WIN + 1)
    vals = jnp.transpose(rel_bias[:, hs][_t5_buckets(offs * dil)]).astype(F32)
    width = 2 * ncol
    pad_lo = Q_BLOCK
    line = jnp.pad(vals, ((0, 0), (pad_lo, width - pad_lo - vals.shape[1])), constant_values=NEG_INF)
    rows = jnp.broadcast_to(line[:, None, :], (HEADS_PER_GROUP, Q_BLOCK, width)).reshape(HEADS_PER_GROUP, -1)
    skew = rows[:, :Q_BLOCK * (width - 1)].reshape(HEADS_PER_GROUP, Q_BLOCK, width - 1)
    tile = skew[:, :, pad_lo:pad_lo + ncol]
    col = np.arange(ncol)[None, None, :]
    before = col < HALF_WIN
    after = col >= HALF_WIN + Q_BLOCK
    return jnp.stack([tile, jnp.where(before, NEG_INF, tile), jnp.where(after, NEG_INF, tile),
                      jnp.where(before | after, NEG_INF, tile)])


def _att_kernel(q_ref, kp_ref, km_ref, kn_ref, vp_ref, vm_ref, vn_ref, *rest):
    bias_refs = rest[:ATT_STEP_BLOCKS]
    o_ref, lse_ref = rest[ATT_STEP_BLOCKS:]
    ncol = Q_BLOCK + 2 * HALF_WIN
    main_rows = ATT_STEP_BLOCKS * Q_BLOCK
    low = lax.broadcasted_iota(jnp.int32, (Q_BLOCK, LANES), 1) < HEAD_DIM
    ones = jnp.ones((ncol, LANES), BF16)
    zero = jnp.zeros((Q_BLOCK, LANES), BF16)
    heads = range(HEADS_PER_GROUP)
    pair_cols = [slice((h // 2) * LANES, (h // 2 + 1) * LANES) for h in heads]

    def window(prev_ref, main_ref, next_ref, sb):
        lo = sb * Q_BLOCK - HALF_WIN
        parts = []
        if lo < 0:
            parts.append(prev_ref[...])
        m_lo, m_hi = max(lo, 0), min(lo + ncol, main_rows)
        parts.append(main_ref[pl.ds(m_lo, m_hi - m_lo), :])
        if lo + ncol > main_rows:
            parts.append(next_ref[...])
        return jnp.concatenate(parts, axis=0) if len(parts) > 1 else parts[0]

    for sb in range(ATT_STEP_BLOCKS):
        rows = pl.ds(sb * Q_BLOCK, Q_BLOCK)
        q = q_ref[rows, :]
        k = window(kp_ref, km_ref, kn_ref, sb)
        v = window(vp_ref, vm_ref, vn_ref, sb)
        scores = []
        for h in heads:
            q2 = q[:, pair_cols[h]]
            qh = jnp.where(low, q2, zero) if h % 2 == 0 else jnp.where(low, zero, q2)
            s = lax.dot_general(qh, k[:, pair_cols[h]], (((1,), (1,)), ((), ())), preferred_element_type=F32)
            scores.append(s + bias_refs[sb][h])
        maxes = [jnp.max(s, axis=-1, keepdims=True) for s in scores]
        probs = [jnp.exp(s - m).astype(BF16) for s, m in zip(scores, maxes)]
        outs = [jnp.dot(p, v[:, pair_cols[h]], preferred_element_type=F32) for h, p in zip(heads, probs)]
        sums = [jnp.dot(p, ones, preferred_element_type=F32) for p in probs]
        for pair in range(HEADS_PER_GROUP // 2):
            a, b = 2 * pair, 2 * pair + 1
            l = jnp.where(low, sums[a], sums[b])
            o_ref[rows, pair_cols[a]] = (jnp.where(low, outs[a], outs[b]) / l).astype(o_ref.dtype)
            lse_ref[rows, pair_cols[a]] = jnp.where(low, maxes[a], maxes[b]) + jnp.log(l)


def _attention_group(qkv, bias_tile, dil, seg_lens):
    nd = qkv.shape[1]
    step_rows = ATT_STEP_BLOCKS * Q_BLOCK
    halves_per_step = step_rows // HALF_WIN
    assert nd % step_rows == 0
    nhalf = nd // HALF_WIN
    starts = np.cumsum((0,) + tuple(seg_lens)) // (dil * Q_BLOCK)
    blk_first = tuple(int(s) for s in starts[:-1])
    blk_last = tuple(int(s) - 1 for s in starts[1:])
    ncol = Q_BLOCK + 2 * HALF_WIN

    def variant(blk):
        return _any_eq(blk, blk_first).astype(jnp.int32) + 2 * _any_eq(blk, blk_last).astype(jnp.int32)

    def main(cblk):
        return pl.BlockSpec((None, step_rows, ATT_OUT), lambda r, b: (r, b, cblk))

    def prev(cblk):
        return pl.BlockSpec((None, HALF_WIN, ATT_OUT),
                            lambda r, b: (r, jnp.maximum(halves_per_step * b - 1, 0), cblk))

    def nxt(cblk):
        return pl.BlockSpec((None, HALF_WIN, ATT_OUT),
                            lambda r, b: (r, jnp.minimum(halves_per_step * (b + 1), nhalf - 1), cblk))

    bias_specs = [pl.BlockSpec((None, HEADS_PER_GROUP, Q_BLOCK, ncol),
                               lambda r, b, sb=sb: (variant(ATT_STEP_BLOCKS * b + sb), 0, 0, 0))
                  for sb in range(ATT_STEP_BLOCKS)]
    out_spec = pl.BlockSpec((None, step_rows, ATT_OUT), lambda r, b: (r, b, 0))
    return pl.pallas_call(
        _att_kernel,
        grid=(dil, nd // step_rows),
        in_specs=[main(0), prev(1), main(1), nxt(1), prev(2), main(2), nxt(2)] + bias_specs,
        out_specs=[out_spec, out_spec],
        out_shape=[jax.ShapeDtypeStruct((dil, nd, ATT_OUT), BF16), jax.ShapeDtypeStruct((dil, nd, ATT_OUT), F32)],
        compiler_params=_cparams(("parallel", "parallel")),
        name=f"attention_d{dil}",
    )(qkv, qkv, qkv, qkv, qkv, qkv, qkv, *([bias_tile] * ATT_STEP_BLOCKS))


def _merge_kernel(yl_ref, ys_ref, o0_ref, l0_ref, o1_ref, l1_ref, o2_ref, l2_ref,
                  ga_ref, gb_ref, gc_ref, wl_ref, ws_ref, wa_ref, m_ref, o1_scr, l1_scr, o2_scr, l2_scr, *, tm):
    ntile = ATT_OUT // LANES

    def sequence_order(src, dst, dil):
        rows = tm // dil
        for r in range(dil):
            blk = src[r].astype(F32)
            for c in range(ntile):
                dst[c, pl.ds(r, rows, stride=dil), :] = blk[:, c * LANES:(c + 1) * LANES]
        return jnp.concatenate([dst[c] for c in range(ntile)], axis=-1)

    def gate(ref):
        return _sigmoid(ref[...].astype(F32))

    part = (gate(ga_ref) * jnp.dot(yl_ref[...], wl_ref[...], preferred_element_type=F32)
            + gate(gb_ref) * jnp.dot(ys_ref[...], ws_ref[...], preferred_element_type=F32))

    o1 = sequence_order(o1_ref, o1_scr, ATT_GROUPS[1][1])
    l1 = sequence_order(l1_ref, l1_scr, ATT_GROUPS[1][1])
    o2 = sequence_order(o2_ref, o2_scr, ATT_GROUPS[2][1])
    l2 = sequence_order(l2_ref, l2_scr, ATT_GROUPS[2][1])
    l0 = l0_ref[...]
    mx = jnp.maximum(jnp.maximum(l0, l1), l2)
    e0, e1, e2 = jnp.exp(l0 - mx), jnp.exp(l1 - mx), jnp.exp(l2 - mx)
    yatt = ((o0_ref[...].astype(F32) * e0 + o1 * e1 + o2 * e2) / (e0 + e1 + e2)).astype(BF16)
    m_ref[...] = (part + gate(gc_ref) * jnp.dot(yatt, wa_ref[...], preferred_element_type=F32)).astype(m_ref.dtype)


def _merge(ylru, gates, ys5, att, w_br_lru, w_br_s5, w_br_att, layer):
    n = gates.shape[0]
    d = D_MODEL
    tm = 256
    wl, wa = LRU_WIDTH, ATT_OUT
    gspec = [pl.BlockSpec((tm, d), lambda i, c=c: (i, c)) for c in range(3)]
    att_specs, att_args = [], []
    for (_, dil), (o, l) in zip(ATT_GROUPS, att):
        blk = (None, tm, wa) if dil == 1 else (dil, tm // dil, wa)
        att_specs += [pl.BlockSpec(blk, lambda i: (0, i, 0))] * 2
        att_args += [o, l]
    return pl.pallas_call(
        functools.partial(_merge_kernel, tm=tm),
        grid=(n // tm,),
        in_specs=[pl.BlockSpec((tm, wl), lambda i: (i, 0)), pl.BlockSpec((tm, S5_WIDTH), lambda i: (i, 0))]
        + att_specs + gspec
        + [_resident((None, wl, d), lambda i: (layer, 0, 0)), _resident((None, S5_WIDTH, d), lambda i: (layer, 0, 0)),
           _resident((None, wa, d), lambda i: (layer, 0, 0))],
        out_specs=pl.BlockSpec((tm, d), lambda i: (i, 0)),
        out_shape=jax.ShapeDtypeStruct((n, d), BF16),
        scratch_shapes=[pltpu.VMEM((wa // LANES, tm, LANES), F32)] * 4,
        compiler_params=_cparams(("parallel",)),
        name="merge",
    )(ylru, ys5, *att_args, gates, gates, gates, w_br_lru, w_br_s5, w_br_att)


def _out_proj_kernel(x_ref, m_ref, w_ref, g_ref, o_ref):
    mix = jnp.dot(m_ref[...], w_ref[...], preferred_element_type=F32)
    o_ref[...] = x_ref[...] + _rms(mix, g_ref[...])


def _out_proj(x, m, w_out, g, layer):
    n, d = x.shape
    tm = 512
    return pl.pallas_call(
        _out_proj_kernel,
        grid=(n // tm,),
        in_specs=[pl.BlockSpec((tm, d), lambda i: (i, 0)), pl.BlockSpec((tm, d), lambda i: (i, 0)),
                  _resident((None, d, d), lambda i: (layer, 0, 0)), pl.BlockSpec((1, d), lambda i: (0, 0))],
        out_specs=pl.BlockSpec((tm, d), lambda i: (i, 0)),
        out_shape=jax.ShapeDtypeStruct((n, d), F32),
        compiler_params=_cparams(("parallel",)),
        name="mixer_out",
    )(x, m, w_out, g.reshape(1, d))


def _mixer(x, g_pre, g_post, seg_lens, layer, w_in, conv_w, conv_b, lru_wa, lru_ba, lru_wx, lru_bx, lru_L,
           s5_params, s5_d, glu_w, glu_b, w_br_lru, w_br_s5, w_br_att, w_out, bias_tiles):
    gates, ze = _proj_elementwise(x, g_pre, w_in, layer)
    qkv = _proj_qkv(x, g_pre, w_in, layer)

    wg, bg = _lru_gate_weights(lru_wa, lru_ba, lru_wx, lru_bx)
    ylru = _lru(ze, conv_w, conv_b, wg, bg, lru_L.reshape(2, 1, LRU_WIDTH), seg_lens)

    y5 = _s5(ze[:, E_U * S5_WIDTH:(E_U + 1) * S5_WIDTH], _s5_prepare(*s5_params), seg_lens)
    ys5 = _s5_post(y5, ze, s5_d, glu_w, glu_b, layer)

    att = [_attention_group(qkv[g], bias_tiles[g], dil, seg_lens) for g, (_, dil) in enumerate(ATT_GROUPS)]

    m = _merge(ylru, gates, ys5, att, w_br_lru, w_br_s5, w_br_att, layer)
    return _out_proj(x, m, w_out, g_post, layer)


def kernel(x_prompt, x_sample, norm_g, w_in, lru_conv_w, lru_conv_b, lru_wa, lru_ba, lru_wx, lru_bx, lru_L,
           s5_lam_re, s5_lam_im, s5_log_dt, s5_b_re, s5_b_im, s5_c_re, s5_c_im, s5_d, s5_glu_w, s5_glu_b,
           rel_bias, w_br_lru, w_br_s5, w_br_att, w_out, ffn_w1, ffn_w3, ffn_w2):
    bp, tp, d = x_prompt.shape
    bs, ts, _ = x_sample.shape
    seg_lens = (tp,) * bp + (ts,) * bs
    rows = (bp * tp, bs * ts)
    n = sum(rows)
    depth = norm_g.shape[0]
    w1, w3, w2 = ffn_w1.astype(BF16), ffn_w3.astype(BF16), ffn_w2.astype(BF16)
    w_in_b, glu_b16, w_out_b = w_in.astype(BF16), s5_glu_w.astype(BF16), w_out.astype(BF16)
    wbl, wbs, wba = w_br_lru.astype(BF16), w_br_s5.astype(BF16), w_br_att.astype(BF16)
    bias_tiles = [_att_bias_tile(rel_bias, g, dil) for g, (_, dil) in enumerate(ATT_GROUPS)]
    s5_stack = (s5_lam_re, s5_lam_im, s5_log_dt, s5_b_re, s5_b_im, s5_c_re, s5_c_im)

    xs = [x_prompt.reshape(rows[0], d), x_sample.reshape(rows[1], d)]
    for l in range(depth):
        g = norm_g[l]
        (x,) = _ffn(xs, g[0], g[1], w1, w3, w2, l, 0, (n,))
        x = _mixer(x, g[2], g[3], seg_lens, l, w_in_b, lru_conv_w[l], lru_conv_b[l], lru_wa[l], lru_ba[l],
                   lru_wx[l], lru_bx[l], lru_L[l], [p[l] for p in s5_stack], s5_d[l], glu_b16, s5_glu_b[l],
                   wbl, wbs, wba, w_out_b,
                   bias_tiles)
        xs = _ffn([x], g[4], g[5], w1, w3, w2, l, 1, rows if l == depth - 1 else (n,))
    return (xs[0].reshape(bp, tp, d), xs[1].reshape(bs, ts, d))
```

```python
import functools
import math

import numpy as np
import jax
import jax.numpy as jnp
from jax import lax
from jax.experimental import pallas as pl
from jax.experimental.pallas import tpu as pltpu

F32 = jnp.float32
BF16 = jnp.bfloat16

D_MODEL = 2048
LRU_WIDTH = 1024
LRU_BLOCKS = 16
LRU_CONV_W = 4
LRU_C = 8.0
S5_WIDTH = 1024
S5_GROUP_CH = 16
S5_GROUPS = 64
S5_STATE = 64
HEAD_DIM = 64
ATT_GROUPS = ((128, 1), (512, 4), (2048, 16))
ATT_WIDTH = 1536
HEADS_PER_GROUP = 8
ATT_OUT = 512
REL_BUCKETS = 32
REL_MAX_DIST = 1024
RMS_EPS = 1e-6
NEG_INF = -1e30

Q_BLOCK = 128
HALF_WIN = 64
ATT_STEP_BLOCKS = 4
S5_CHUNK = 64
LANES = 128
SUBLANES = 8
COL_QKV = 3 * 1024
E_TILE = 1536
E_XL, E_GL, E_U = 0, 1, 2
VMEM_LIMIT = 48 * 1024 * 1024


def _cparams(sem, vmem_limit=VMEM_LIMIT):
    return pltpu.CompilerParams(dimension_semantics=sem, vmem_limit_bytes=vmem_limit)


def _rms(v, g):
    width = v.shape[-1]
    sq = v * v
    part = sq[:, :LANES]
    for c in range(1, width // LANES):
        part = part + sq[:, c * LANES:(c + 1) * LANES]
    ms = jnp.sum(part, axis=-1, keepdims=True) * (1.0 / width)
    return v * lax.rsqrt(ms + RMS_EPS) * g


ROW_BLOCK = 16


def _for_row_blocks(n_rows, body):
    def step(t, carry):
        body(pl.ds(pl.multiple_of(t * ROW_BLOCK, ROW_BLOCK), ROW_BLOCK))
        return carry

    lax.fori_loop(0, n_rows // ROW_BLOCK, step, 0, unroll=16)


def _norm_rows_to(h_scr, x_ref, g_ref):
    g = g_ref[...]

    def body(rows):
        h_scr[rows, :] = _rms(x_ref[rows, :], g).astype(h_scr.dtype)

    _for_row_blocks(x_ref.shape[0], body)


def _gelu_tanh(v):
    return 0.5 * v * (1.0 + jnp.tanh(math.sqrt(2.0 / math.pi) * (v + 0.044715 * (v * v * v))))


def _sigmoid(v):
    return 0.5 * jnp.tanh(0.5 * v) + 0.5


def _any_eq(idx, values):
    hit = idx == values[0]
    for v in values[1:]:
        hit = jnp.logical_or(hit, idx == v)
    return hit


def _resident(shape, index_map):
    return pl.BlockSpec(shape, index_map, pipeline_mode=pl.Buffered(1))


def _ffn_kernel(*refs, nj, in_tiles, out_tiles):
    n_in, n_out = len(in_tiles), len(out_tiles)
    x_refs = refs[:n_in]
    gpre_ref, gpost_ref, w1_ref, w3_ref, w2_ref = refs[n_in:n_in + 5]
    o_refs = refs[n_in + 5:n_in + 5 + n_out]
    if n_in == 1 and n_out == 1:
        (h_scr,) = refs[n_in + 5 + n_out:]
        acc_scr = o_refs[0]
    else:
        h_scr, acc_scr = refs[n_in + 5 + n_out:]
    i = pl.program_id(0)
    j = pl.program_id(1)

    def active(bounds, k):
        lo = sum(bounds[:k])
        return jnp.logical_and(i >= lo, i < lo + bounds[k])

    for k in range(n_in):
        @pl.when(jnp.logical_and(j == 0, active(in_tiles, k)))
        def _(k=k):
            _norm_rows_to(h_scr, x_refs[k], gpre_ref)
            acc_scr[...] = jnp.zeros_like(acc_scr)

    h = h_scr[...]
    a = jnp.dot(h, w1_ref[...], preferred_element_type=F32)
    b = jnp.dot(h, w3_ref[...], preferred_element_type=F32)
    g = (a * _sigmoid(a) * b).astype(BF16)
    acc_scr[...] += jnp.dot(g, w2_ref[...], preferred_element_type=F32)

    for ki in range(n_in):
        for ko in range(n_out):
            @pl.when(jnp.logical_and(j == nj - 1, jnp.logical_and(active(in_tiles, ki), active(out_tiles, ko))))
            def _(ki=ki, ko=ko):
                g_half = 0.5 * gpost_ref[...]

                def finish(rows):
                    o_refs[ko][rows, :] = x_refs[ki][rows, :] + _rms(acc_scr[rows, :], g_half)

                _for_row_blocks(acc_scr.shape[0], finish)


def _ffn(xs, g_pre, g_post, w1, w3, w2, layer, which, out_rows):
    d = xs[0].shape[1]
    dff = w1.shape[-1]
    single = len(xs) == 1 and len(out_rows) == 1
    tm, tf = (1024 if single else 512), 512
    nj = dff // tf
    blocks = (len(xs) + len(out_rows)) * 2 * tm * d * 4 + tm * d * 2 + 2 * 3 * d * tf * 2
    vmem = blocks + (0 if single else tm * d * 4) + ((6 if single else 12) << 20)
    in_tiles = tuple(x.shape[0] // tm for x in xs)
    out_tiles = tuple(r // tm for r in out_rows)

    def piece(bounds, k):
        lo = sum(bounds[:k])
        return lambda i, j: (jnp.clip(i - lo, 0, bounds[k] - 1), 0)

    const = lambda i, j: (0, 0)
    wcol = pl.BlockSpec((None, None, d, tf), lambda i, j: (layer, which, 0, j))
    outs = pl.pallas_call(
        functools.partial(_ffn_kernel, nj=nj, in_tiles=in_tiles, out_tiles=out_tiles),
        grid=(sum(in_tiles), nj),
        in_specs=[pl.BlockSpec((tm, d), piece(in_tiles, k)) for k in range(len(xs))]
        + [pl.BlockSpec((1, d), const), pl.BlockSpec((1, d), const), wcol, wcol,
           pl.BlockSpec((None, None, tf, d), lambda i, j: (layer, which, j, 0))],
        out_specs=[pl.BlockSpec((tm, d), piece(out_tiles, k)) for k in range(len(out_rows))],
        out_shape=[jax.ShapeDtypeStruct((r, d), F32) for r in out_rows],
        scratch_shapes=[pltpu.VMEM((tm, d), BF16)] + ([] if single else [pltpu.VMEM((tm, d), F32)]),
        compiler_params=_cparams(("parallel", "arbitrary"), vmem),
        name="ffn",
    )(*xs, g_pre.reshape(1, d), g_post.reshape(1, d), w1, w3, w2)
    return list(outs)


def _norm_mm_kernel(x_ref, g_ref, w_ref, gates_ref, e_ref, h_scr, *, ngate):
    j = pl.program_id(1)

    @pl.when(j == 0)
    def _():
        _norm_rows_to(h_scr, x_ref, g_ref)

    @pl.when(j < ngate)
    def _():
        gates_ref[...] = jnp.dot(h_scr[...], w_ref[...], preferred_element_type=F32).astype(gates_ref.dtype)

    @pl.when(j >= ngate)
    def _():
        e_ref[...] = jnp.dot(h_scr[...], w_ref[...], preferred_element_type=F32)


def _proj_elementwise(x, g, w_in, layer):
    n, d = x.shape
    tm, tn = 1024, E_TILE
    vmem = 2 * (tm * d * 4 + d * tn * 2 + tm * tn * (2 + 4)) + tm * d * 2 + (4 << 20)
    lead = COL_QKV // tn
    skip = 3 * ATT_WIDTH // tn
    ncol = (w_in.shape[-1] - 3 * ATT_WIDTH) // tn
    ngate = ncol - lead
    return pl.pallas_call(
        functools.partial(_norm_mm_kernel, ngate=ngate),
        grid=(n // tm, ncol),
        in_specs=[pl.BlockSpec((tm, d), lambda i, j: (i, 0)), pl.BlockSpec((1, d), lambda i, j: (0, 0)),
                  pl.BlockSpec((None, d, tn),
                               lambda i, j: (layer, 0, jnp.where(j < ngate, j + lead + skip, j - ngate)))],
        out_specs=[pl.BlockSpec((tm, tn), lambda i, j: (i, jnp.minimum(j, ngate - 1))),
                   pl.BlockSpec((tm, tn), lambda i, j: (i, jnp.maximum(j - ngate, 0)))],
        out_shape=[jax.ShapeDtypeStruct((n, ngate * tn), BF16), jax.ShapeDtypeStruct((n, lead * tn), F32)],
        scratch_shapes=[pltpu.VMEM((tm, d), BF16)],
        compiler_params=_cparams(("parallel", "arbitrary"), vmem),
        name="mixer_in",
    )(x, g.reshape(1, d), w_in)


def _qkv_kernel(x_ref, g_ref, wq_ref, wk_ref, wv_ref, o0_ref, o1_ref, o2_ref, h_scr, res_scr, *, tm):
    step = pl.program_id(1)
    o_refs = (o0_ref, o1_ref, o2_ref)
    order = tuple(reversed(range(len(ATT_GROUPS))))
    ntile = 3 * ATT_OUT // LANES

    @pl.when(step == 0)
    def _():
        _norm_rows_to(h_scr, x_ref, g_ref)

    def project():
        h = h_scr[...]
        q = jnp.dot(h, wq_ref[...], preferred_element_type=F32) * (HEAD_DIM ** -0.5)
        k = jnp.dot(h, wk_ref[...], preferred_element_type=F32)
        v = jnp.dot(h, wv_ref[...], preferred_element_type=F32)
        return jnp.concatenate([q, k, v], axis=-1)

    def write_residue_major(slot, gi):
        dil = ATT_GROUPS[gi][1]
        rows = tm // dil
        for r in range(dil):
            o_refs[gi][r] = jnp.concatenate(
                [res_scr[slot, c, pl.ds(r, rows, stride=dil), :] for c in range(ntile)], axis=-1).astype(BF16)

    for s, gi in enumerate(order):
        @pl.when(step == s)
        def _(s=s, gi=gi):
            res = project()
            if s > 0 and ATT_GROUPS[order[s - 1]][1] > 1:
                write_residue_major((s - 1) % 2, order[s - 1])
            if ATT_GROUPS[gi][1] == 1:
                o_refs[gi][0] = res.astype(BF16)
            else:
                for c in range(ntile):
                    res_scr[s % 2, c] = res[:, c * LANES:(c + 1) * LANES]


def _proj_qkv(x, g, w_in, layer):
    n, d = x.shape
    tm = 512
    qb = COL_QKV // ATT_OUT

    ng = len(ATT_GROUPS)
    assert ATT_GROUPS[0][1] == 1

    def wspec(off):
        return pl.BlockSpec((None, d, ATT_OUT), lambda i, s: (layer, 0, qb + off + ng - 1 - s))

    return pl.pallas_call(
        functools.partial(_qkv_kernel, tm=tm),
        grid=(n // tm, ng),
        in_specs=[pl.BlockSpec((tm, d), lambda i, gq: (i, 0)), pl.BlockSpec((1, d), lambda i, gq: (0, 0)),
                  wspec(0), wspec(ng), wspec(2 * ng)],
        out_specs=[pl.BlockSpec((dil, tm // dil, 3 * ATT_OUT), lambda i, gq: (0, i, 0)) for _, dil in ATT_GROUPS],
        out_shape=[jax.ShapeDtypeStruct((dil, n // dil, 3 * ATT_OUT), BF16) for _, dil in ATT_GROUPS],
        scratch_shapes=[pltpu.VMEM((tm, d), BF16), pltpu.VMEM((2, 3 * ATT_OUT // LANES, tm, LANES), F32)],
        compiler_params=_cparams(("parallel", "arbitrary")),
        name="mixer_qkv",
    )(x, g.reshape(1, d), w_in, w_in, w_in)


def _lru_kernel(*refs, tc, nt, seg_first, seg_last, d):
    xm_ref, xp_ref, xn_ref, cw_ref, cb_ref, wg_ref, bg_ref, lam_ref = refs[:8]
    if d == 0:
        o_ref, perm_scr, a_scr, b_scr, h_scr = refs[8:]
    else:
        hf_ref, gl_ref, o_ref, perm_scr, a_scr, b_scr, h_scr = refs[8:]
    i = pl.program_id(0)
    ti = i if d == 0 else nt - 1 - i
    w = LRU_WIDTH
    at_first = _any_eq(ti, seg_first)
    at_last = _any_eq(ti, seg_last)

    sub = tc // SUBLANES
    pitch = sub + SUBLANES
    ntile = w // LANES
    for c in range(ntile):
        for jj in range(SUBLANES):
            perm_scr[c, pl.ds(jj * pitch, sub), :] = xm_ref[pl.ds(jj * sub, sub), c * LANES:(c + 1) * LANES]

    def block(k):
        return jnp.concatenate([perm_scr[c, pl.ds(k, SUBLANES, stride=pitch), :] for c in range(ntile)], axis=-1)

    sub_id = lax.broadcasted_iota(jnp.int32, (SUBLANES, w), 0)
    x_blk = [block(k) for k in range(sub)]
    before = xp_ref[...] * jnp.where(at_first, 0.0, 1.0)
    after = xn_ref[...] * jnp.where(at_last, 0.0, 1.0)

    def from_previous_subchunk(blk, edge_row):
        return jnp.where(sub_id == 0, edge_row, pltpu.roll(blk, 1, axis=0))

    def from_next_subchunk(blk, edge_row):
        return jnp.where(sub_id == SUBLANES - 1, edge_row, pltpu.roll(blk, SUBLANES - 1, axis=0))

    def shifted(k, off):
        kk = k + off
        if 0 <= kk < sub:
            return x_blk[kk]
        if kk < 0:
            return from_previous_subchunk(x_blk[kk + sub], before[SUBLANES + kk:SUBLANES + kk + 1, :])
        return from_next_subchunk(x_blk[kk - sub], after[kk - sub:kk - sub + 1, :])

    left = LRU_CONV_W // 2
    taps = [cw_ref[pl.ds(kk, 1), :] for kk in range(LRU_CONV_W)]
    xc = jnp.concatenate(
        [cb_ref[...] + sum(shifted(k, kk - left) * taps[kk] for kk in range(LRU_CONV_W)) for k in range(sub)], axis=0)

    xcb = xc.astype(BF16)
    parts = [jnp.dot(xcb[:, p * LANES:(p + 1) * LANES], wg_ref[p], preferred_element_type=F32)
             for p in range(w // LANES)]
    r = _sigmoid(jnp.concatenate([g[:, :LANES] for g in parts], axis=-1) + bg_ref[:, :w])
    ig = _sigmoid(jnp.concatenate([g[:, LANES:] for g in parts], axis=-1) + bg_ref[:, w:])
    nlam = -lam_ref[...]
    softplus = jnp.maximum(nlam, 0.0) + jnp.log(1.0 + jnp.exp(-jnp.abs(nlam)))
    a = jnp.exp(-LRU_C * r * softplus)
    a_scr[...] = a
    b_scr[...] = jnp.sqrt(1.0 - a * a) * (ig * xc)

    @pl.when(at_first if d == 0 else at_last)
    def _():
        h_scr[...] = jnp.zeros_like(h_scr)

    hl = jnp.zeros((SUBLANES, w), F32)
    pp = jnp.ones((SUBLANES, w), F32)
    for k in (range(sub) if d == 0 else range(sub - 1, -1, -1)):
        rows = pl.ds(k * SUBLANES, SUBLANES)
        av = a_scr[rows, :]
        hl = av * hl + b_scr[rows, :]
        pp = av * pp
        a_scr[rows, :] = pp
        b_scr[rows, :] = hl
    cur = h_scr[pl.ds(0, 1), :]
    carry = jnp.zeros((SUBLANES, w), F32)
    for jj in (range(SUBLANES) if d == 0 else range(SUBLANES - 1, -1, -1)):
        carry = jnp.where(sub_id == jj, cur, carry)
        cur = hl[jj:jj + 1, :] + pp[jj:jj + 1, :] * cur
    h_scr[pl.ds(0, 1), :] = cur
    for k in range(sub):
        rows = pl.ds(k * SUBLANES, SUBLANES)
        hv = b_scr[rows, :] + a_scr[rows, :] * carry
        for c in range(ntile):
            perm_scr[c, pl.ds(k, SUBLANES, stride=pitch), :] = hv[:, c * LANES:(c + 1) * LANES]
    for jj in range(SUBLANES):
        rows = pl.ds(jj * sub, sub)
        hv = jnp.concatenate([perm_scr[c, pl.ds(jj * pitch, sub), :] for c in range(ntile)], axis=-1)
        if d == 0:
            o_ref[rows, :] = hv
        else:
            o_ref[rows, :] = ((hf_ref[rows, :] + hv) * _gelu_tanh(gl_ref[rows, :])).astype(o_ref.dtype)


def _lru(ze, conv_w, conv_b, wg, bg, lam, seg_lens):
    n = ze.shape[0]
    w = LRU_WIDTH
    tc = 256
    nt = n // tc
    starts = np.cumsum((0,) + tuple(seg_lens))
    seg_first = tuple(int(s) // tc for s in starts[:-1])
    seg_last = tuple(int(s) // tc - 1 for s in starts[1:])
    hb = tc // SUBLANES
    nhb = n // SUBLANES
    pitched = SUBLANES * (tc // SUBLANES + SUBLANES)

    def one_direction(d, extra_specs, extra_args, out_dtype):
        tile = (lambda i: i) if d == 0 else (lambda i: nt - 1 - i)
        return pl.pallas_call(
            functools.partial(_lru_kernel, tc=tc, nt=nt, seg_first=seg_first, seg_last=seg_last, d=d),
            grid=(nt,),
            in_specs=[pl.BlockSpec((tc, w), lambda i: (tile(i), E_XL)),
                      pl.BlockSpec((SUBLANES, w), lambda i: (jnp.maximum(tile(i) * hb - 1, 0), E_XL)),
                      pl.BlockSpec((SUBLANES, w), lambda i: (jnp.minimum((tile(i) + 1) * hb, nhb - 1), E_XL)),
                      pl.BlockSpec((LRU_CONV_W, w), lambda i: (0, 0)),
                      pl.BlockSpec((1, w), lambda i: (0, 0)),
                      _resident((None, w // LANES, LANES, 2 * LANES), lambda i: (d, 0, 0, 0)),
                      pl.BlockSpec((None, 1, 2 * w), lambda i: (d, 0, 0)),
                      pl.BlockSpec((None, 1, w), lambda i: (d, 0, 0))] + [spec(tile) for spec in extra_specs],
            out_specs=pl.BlockSpec((tc, w), lambda i: (tile(i), 0)),
            out_shape=jax.ShapeDtypeStruct((n, w), out_dtype),
            scratch_shapes=[pltpu.VMEM((w // LANES, pitched, LANES), F32), pltpu.VMEM((tc, w), F32),
                            pltpu.VMEM((tc, w), F32), pltpu.VMEM((SUBLANES, w), F32)],
            compiler_params=_cparams(("arbitrary",)),
            name="rglru_fwd" if d == 0 else "rglru_bwd",
        )(ze, ze, ze, conv_w, conv_b.reshape(1, w), wg, bg, lam, *extra_args)

    hf = one_direction(0, [], [], F32)
    return one_direction(1, [lambda tile: pl.BlockSpec((tc, w), lambda i: (tile(i), 0)),
                             lambda tile: pl.BlockSpec((tc, w), lambda i: (tile(i), E_GL))], [hf, ze], BF16)


def _lru_gate_weights(wa, ba, wx, bx):
    per_tile = LANES * LRU_BLOCKS // LRU_WIDTH

    def tiles(wb):
        bw = wb.shape[-1]
        eye = jnp.eye(per_tile, dtype=wb.dtype)
        grouped = wb.reshape(2, LRU_BLOCKS // per_tile, per_tile, bw, bw)
        full = grouped[:, :, :, :, None, :] * eye[None, None, :, None, :, None]
        return full.reshape(2, LRU_BLOCKS // per_tile, LANES, LANES)

    wg = jnp.concatenate([tiles(wa), tiles(wx)], axis=-1).astype(BF16)
    bg = jnp.concatenate([ba, bx], axis=-1).reshape(2, 1, 2 * LRU_WIDTH)
    return wg, bg


S5_GROUP_BATCH = 8


def _s5_ktable_kernel(b_ref, w_ref, o_ref):
    for i in range(S5_GROUP_BATCH):
        o_ref[i] = jnp.dot(b_ref[i], w_ref[i], preferred_element_type=F32, precision=lax.Precision.HIGHEST)


def _s5_tables(lam_re, lam_im, log_dt, b_re, b_im, c_re, c_im):
    L, G, P, C = S5_CHUNK, S5_GROUPS, S5_STATE, S5_GROUP_CH
    dt = jnp.exp(log_dt)[..., None]
    mag = jnp.exp(lam_re * dt)
    ar = mag * jnp.cos(lam_im * dt)
    ai = mag * jnp.sin(lam_im * dt)
    den = lam_re * lam_re + lam_im * lam_im
    cr = ((ar - 1.0) * lam_re + ai * lam_im) / den
    ci = (ai * lam_re - (ar - 1.0) * lam_im) / den
    bbr = cr[..., None] * b_re - ci[..., None] * b_im
    bbi = cr[..., None] * b_im + ci[..., None] * b_re
    pr, pi = jnp.ones_like(ar)[None], jnp.zeros_like(ai)[None]
    nr, ni = ar, ai
    while pr.shape[0] < L + 1:
        pr, pi = (jnp.concatenate([pr, pr * nr - pi * ni], axis=0),
                  jnp.concatenate([pi, pr * ni + pi * nr], axis=0))
        nr, ni = nr * nr - ni * ni, 2.0 * nr * ni
    pr, pi = pr[:L + 1], pi[:L + 1]
    zr = pr[:L, ..., None] * bbr - pi[:L, ..., None] * bbi
    zi = pr[:L, ..., None] * bbi + pi[:L, ..., None] * bbr

    def c_pow(powers_r, powers_i):
        ctr = jnp.transpose(c_re, (0, 1, 3, 2))[:, :, :, None, :]
        cti = jnp.transpose(c_im, (0, 1, 3, 2))[:, :, :, None, :]
        qr = jnp.transpose(powers_r, (1, 2, 3, 0))[..., None]
        qi = jnp.transpose(powers_i, (1, 2, 3, 0))[..., None]
        return ctr * qr - cti * qi, ctr * qi + cti * qr

    wr, wi = c_pow(pr[:L], pi[:L])
    wmat = jnp.concatenate([wr, wi], axis=2).reshape(2 * G, 2 * P, L * C)
    bmat = jnp.concatenate([jnp.transpose(bbr, (0, 1, 3, 2)), -jnp.transpose(bbi, (0, 1, 3, 2))],
                           axis=-1).reshape(2 * G, C, 2 * P)

    def w_in(z, flip):
        zf = z[::-1] if flip else z
        return jnp.transpose(zf, (1, 0, 3, 2)).reshape(G, L * C, P)

    win = jnp.concatenate([w_in(zr[:, 0], True), w_in(zr[:, 1], False),
                           w_in(zi[:, 0], True), w_in(zi[:, 1], False)], axis=-1).astype(BF16)

    fr, fi = c_pow(pr[1:], pi[1:])
    br, bi = c_pow(pr[:0:-1], pi[:0:-1])
    wout = jnp.concatenate([fr[0], br[1], -fi[0], -bi[1]], axis=1).reshape(G, 4 * P, L * C).astype(BF16)
    al = jnp.concatenate([pr[L, 0], pr[L, 1], pi[L, 0], pi[L, 1]], axis=-1)
    return bmat, wmat, win, wout, al


def _s5_lag_table(kt):
    L, G, C = S5_CHUNK, S5_GROUPS, S5_GROUP_CH
    kk = jnp.concatenate([kt[1, :, :, :0:-1], kt[0, :, :, :1] + kt[1, :, :, :1], kt[0, :, :, 1:]], axis=2)
    kk = kk.reshape(G, C, (2 * L - 1) * C)
    return jnp.pad(kk, ((0, 0), (0, 0), (0, 2 * L * C - kk.shape[-1])))


def _s5_prepare(lam_re, lam_im, log_dt, b_re, b_im, c_re, c_im):
    L, G, P, C = S5_CHUNK, S5_GROUPS, S5_STATE, S5_GROUP_CH
    bmat, wmat, win, wout, al = _s5_tables(lam_re, lam_im, log_dt, b_re, b_im, c_re, c_im)
    kt = pl.pallas_call(
        _s5_ktable_kernel,
        grid=(2 * G // S5_GROUP_BATCH,),
        in_specs=[pl.BlockSpec((S5_GROUP_BATCH, C, 2 * P), lambda g: (g, 0, 0)),
                  pl.BlockSpec((S5_GROUP_BATCH, 2 * P, L * C), lambda g: (g, 0, 0))],
        out_specs=pl.BlockSpec((S5_GROUP_BATCH, C, L * C), lambda g: (g, 0, 0)),
        out_shape=jax.ShapeDtypeStruct((2 * G, C, L * C), F32),
        compiler_params=_cparams(("parallel",)),
        name="s5_ktable",
    )(bmat, wmat)
    return _s5_lag_table(kt.reshape(2, G, C, L, C)), win, wout, al


def _s5_state_kernel(v_ref, win_ref, o_ref):
    width = win_ref.shape[-1]
    for i in range(S5_GROUP_BATCH):
        o_ref[:, i * width:(i + 1) * width] = jnp.dot(v_ref[i], win_ref[i], preferred_element_type=F32)


def _s5_scan_kernel(s_ref, al_ref, o_ref, *, seg_chunks):
    p2 = 2 * S5_STATE
    alr = al_ref[:, :p2]
    ali = al_ref[:, p2:]
    is_fwd = lax.broadcasted_iota(jnp.int32, alr.shape, 1) < S5_STATE
    zero = jnp.zeros_like(alr)
    start = 0
    for n_chunks in seg_chunks:
        def fwd(k, carry, start=start):
            xr, xi = carry
            c = start + k
            o_ref[c, :, :p2] = xr
            o_ref[c, :, p2:] = xi
            sr = s_ref[c, :, :p2]
            si = s_ref[c, :, p2:]
            return alr * xr - ali * xi + sr, alr * xi + ali * xr + si

        lax.fori_loop(0, n_chunks, fwd, (zero, zero))

        def bwd(k, carry, start=start, n_chunks=n_chunks):
            xr, xi = carry
            c = start + n_chunks - 1 - k
            o_ref[c, :, :p2] = jnp.where(is_fwd, o_ref[c, :, :p2], xr)
            o_ref[c, :, p2:] = jnp.where(is_fwd, o_ref[c, :, p2:], xi)
            sr = s_ref[c, :, :p2]
            si = s_ref[c, :, p2:]
            return alr * xr - ali * xi + sr, alr * xi + ali * xr + si

        lax.fori_loop(0, n_chunks, bwd, (zero, zero))
        start += n_chunks


def _s5_out_kernel(v_ref, kk_ref, x_ref, wout_ref, o_ref, mt_scr):
    L, C = S5_CHUNK, S5_GROUP_CH
    kk = kk_ref[...]
    width = kk.shape[-1]
    per_tile = LANES // C
    for rot in range(per_tile):
        shifted = kk if rot == 0 else pltpu.roll(kk, width - rot * C, axis=1)
        shifted = shifted.astype(BF16)
        for s in range(L):
            lag0 = L - 1 - s
            if lag0 % per_tile == rot:
                col = (lag0 // per_tile) * LANES
                mt_scr[pl.ds(s * C, C), :] = shifted[:, col:col + L * C]
    o_ref[...] = (jnp.dot(v_ref[...], mt_scr[...], preferred_element_type=F32)
                  + jnp.dot(x_ref[...].astype(BF16), wout_ref[...], preferred_element_type=F32)).astype(o_ref.dtype)


def _s5(u, tables, seg_lens):
    kk, win, wout, al = tables
    L, G, P, C = S5_CHUNK, S5_GROUPS, S5_STATE, S5_GROUP_CH
    n = u.shape[0]
    nc = n // L
    lc = L * C
    v = jnp.transpose(u.reshape(nc, L, G, C), (2, 0, 1, 3)).reshape(G, nc, lc).astype(BF16)
    states = pl.pallas_call(
        _s5_state_kernel,
        grid=(G // S5_GROUP_BATCH,),
        in_specs=[pl.BlockSpec((S5_GROUP_BATCH, nc, lc), lambda g: (g, 0, 0)),
                  pl.BlockSpec((S5_GROUP_BATCH, lc, 4 * P), lambda g: (g, 0, 0))],
        out_specs=pl.BlockSpec((nc, S5_GROUP_BATCH * 4 * P), lambda g: (0, g)),
        out_shape=jax.ShapeDtypeStruct((nc, G * 4 * P), F32),
        compiler_params=_cparams(("parallel",)),
        name="s5_chunk_state",
    )(v, win)
    gb = SUBLANES
    carried = pl.pallas_call(
        functools.partial(_s5_scan_kernel, seg_chunks=tuple(t // L for t in seg_lens)),
        grid=(G // gb,),
        in_specs=[pl.BlockSpec((nc, gb, 4 * P), lambda g: (0, g, 0)),
                  pl.BlockSpec((gb, 4 * P), lambda g: (g, 0))],
        out_specs=pl.BlockSpec((nc, gb, 4 * P), lambda g: (0, g, 0)),
        out_shape=jax.ShapeDtypeStruct((nc, G, 4 * P), F32),
        compiler_params=_cparams(("parallel",)),
        name="s5_chunk_scan",
    )(states.reshape(nc, G, 4 * P), al)
    y = pl.pallas_call(
        _s5_out_kernel,
        grid=(G,),
        in_specs=[pl.BlockSpec((None, nc, lc), lambda g: (g, 0, 0)),
                  pl.BlockSpec((None, C, 2 * lc), lambda g: (g, 0, 0)),
                  pl.BlockSpec((nc, 4 * P), lambda g: (0, g)),
                  pl.BlockSpec((None, 4 * P, lc), lambda g: (g, 0, 0))],
        out_specs=pl.BlockSpec((None, nc, lc), lambda g: (g, 0, 0)),
        out_shape=jax.ShapeDtypeStruct((G, nc, lc), BF16),
        scratch_shapes=[pltpu.VMEM((lc, lc), BF16)],
        compiler_params=_cparams(("parallel",)),
        name="s5_chunk_out",
    )(v, kk, carried.reshape(nc, G * 4 * P), wout)
    return jnp.transpose(y.reshape(G, nc, L, C), (1, 2, 0, 3)).reshape(n, G * C)


def _s5_post_kernel(y_ref, u_ref, d_ref, w_ref, b_ref, o_ref):
    y1 = _gelu_tanh(y_ref[...].astype(F32) + d_ref[...] * u_ref[...])
    gate = jnp.dot(y1.astype(BF16), w_ref[...], preferred_element_type=F32) + b_ref[...]
    o_ref[...] = (y1 * _sigmoid(gate)).astype(o_ref.dtype)


def _s5_post(y5, ze, s5_d, glu_w, glu_b, layer):
    n, w = y5.shape
    tm = 512
    return pl.pallas_call(
        _s5_post_kernel,
        grid=(n // tm,),
        in_specs=[pl.BlockSpec((tm, w), lambda i: (i, 0)), pl.BlockSpec((tm, w), lambda i: (i, E_U)),
                  pl.BlockSpec((1, w), lambda i: (0, 0)), pl.BlockSpec((None, w, w), lambda i: (layer, 0, 0)),
                  pl.BlockSpec((1, w), lambda i: (0, 0))],
        out_specs=pl.BlockSpec((tm, w), lambda i: (i, 0)),
        out_shape=jax.ShapeDtypeStruct((n, w), BF16),
        compiler_params=_cparams(("parallel",)),
        name="s5_post",
    )(y5, ze, s5_d.reshape(1, w), glu_w, glu_b.reshape(1, w))


def _t5_buckets(rel):
    half = REL_BUCKETS // 2
    max_exact = half // 2
    sign = (rel > 0).astype(np.int32) * half
    n = np.abs(rel)
    large = max_exact + (np.log(np.maximum(n, 1) / max_exact)
                         / np.log(REL_MAX_DIST / max_exact) * (half - max_exact)).astype(np.int32)
    large = np.minimum(large, half - 1)
    return sign + np.where(n < max_exact, n, large)


def _att_bias_tile(rel_bias, group, dil):
    ncol = Q_BLOCK + 2 * HALF_WIN
    hs = slice(group * HEADS_PER_GROUP, (group + 1) * HEADS_PER_GROUP)
    offs = np.arange(-HALF_WIN, HALF_WIN + 1)
    vals = jnp.transpose(rel_bias[:, hs][_t5_buckets(offs * dil)]).astype(F32)
    width = 2 * ncol
    pad_lo = Q_BLOCK
    line = jnp.pad(vals, ((0, 0), (pad_lo, width - pad_lo - vals.shape[1])), constant_values=NEG_INF)
    rows = jnp.broadcast_to(line[:, None, :], (HEADS_PER_GROUP, Q_BLOCK, width)).reshape(HEADS_PER_GROUP, -1)
    skew = rows[:, :Q_BLOCK * (width - 1)].reshape(HEADS_PER_GROUP, Q_BLOCK, width - 1)
    tile = skew[:, :, pad_lo:pad_lo + ncol]
    col = np.arange(ncol)[None, None, :]
    before = col < HALF_WIN
    after = col >= HALF_WIN + Q_BLOCK
    return jnp.stack([tile, jnp.where(before, NEG_INF, tile), jnp.where(after, NEG_INF, tile),
                      jnp.where(before | after, NEG_INF, tile)])


def _att_kernel(q_ref, kp_ref, km_ref, kn_ref, vp_ref, vm_ref, vn_ref, *rest):
    bias_refs = rest[:ATT_STEP_BLOCKS]
    o_ref, lse_ref = rest[ATT_STEP_BLOCKS:]
    ncol = Q_BLOCK + 2 * HALF_WIN
    main_rows = ATT_STEP_BLOCKS * Q_BLOCK
    low = lax.broadcasted_iota(jnp.int32, (Q_BLOCK, LANES), 1) < HEAD_DIM
    ones = jnp.ones((ncol, LANES), BF16)
    zero = jnp.zeros((Q_BLOCK, LANES), BF16)
    heads = range(HEADS_PER_GROUP)
    pair_cols = [slice((h // 2) * LANES, (h // 2 + 1) * LANES) for h in heads]

    def window(prev_ref, main_ref, next_ref, sb):
        lo = sb * Q_BLOCK - HALF_WIN
        parts = []
        if lo < 0:
            parts.append(prev_ref[...])
        m_lo, m_hi = max(lo, 0), min(lo + ncol, main_rows)
        parts.append(main_ref[pl.ds(m_lo, m_hi - m_lo), :])
        if lo + ncol > main_rows:
            parts.append(next_ref[...])
        return jnp.concatenate(parts, axis=0) if len(parts) > 1 else parts[0]

    for sb in range(ATT_STEP_BLOCKS):
        rows = pl.ds(sb * Q_BLOCK, Q_BLOCK)
        q = q_ref[rows, :]
        k = window(kp_ref, km_ref, kn_ref, sb)
        v = window(vp_ref, vm_ref, vn_ref, sb)
        scores = []
        for h in heads:
            q2 = q[:, pair_cols[h]]
            qh = jnp.where(low, q2, zero) if h % 2 == 0 else jnp.where(low, zero, q2)
            s = lax.dot_general(qh, k[:, pair_cols[h]], (((1,), (1,)), ((), ())), preferred_element_type=F32)
            scores.append(s + bias_refs[sb][h])
        maxes = [jnp.max(s, axis=-1, keepdims=True) for s in scores]
        probs = [jnp.exp(s - m).astype(BF16) for s, m in zip(scores, maxes)]
        outs = [jnp.dot(p, v[:, pair_cols[h]], preferred_element_type=F32) for h, p in zip(heads, probs)]
        sums = [jnp.dot(p, ones, preferred_element_type=F32) for p in probs]
        for pair in range(HEADS_PER_GROUP // 2):
            a, b = 2 * pair, 2 * pair + 1
            l = jnp.where(low, sums[a], sums[b])
            o_ref[rows, pair_cols[a]] = (jnp.where(low, outs[a], outs[b]) / l).astype(o_ref.dtype)
            lse_ref[rows, pair_cols[a]] = jnp.where(low, maxes[a], maxes[b]) + jnp.log(l)


def _attention_group(qkv, bias_tile, dil, seg_lens):
    nd = qkv.shape[1]
    step_rows = ATT_STEP_BLOCKS * Q_BLOCK
    halves_per_step = step_rows // HALF_WIN
    assert nd % step_rows == 0
    nhalf = nd // HALF_WIN
    starts = np.cumsum((0,) + tuple(seg_lens)) // (dil * Q_BLOCK)
    blk_first = tuple(int(s) for s in starts[:-1])
    blk_last = tuple(int(s) - 1 for s in starts[1:])
    ncol = Q_BLOCK + 2 * HALF_WIN

    def variant(blk):
        return _any_eq(blk, blk_first).astype(jnp.int32) + 2 * _any_eq(blk, blk_last).astype(jnp.int32)

    def main(cblk):
        return pl.BlockSpec((None, step_rows, ATT_OUT), lambda r, b: (r, b, cblk))

    def prev(cblk):
        return pl.BlockSpec((None, HALF_WIN, ATT_OUT),
                            lambda r, b: (r, jnp.maximum(halves_per_step * b - 1, 0), cblk))

    def nxt(cblk):
        return pl.BlockSpec((None, HALF_WIN, ATT_OUT),
                            lambda r, b: (r, jnp.minimum(halves_per_step * (b + 1), nhalf - 1), cblk))

    bias_specs = [pl.BlockSpec((None, HEADS_PER_GROUP, Q_BLOCK, ncol),
                               lambda r, b, sb=sb: (variant(ATT_STEP_BLOCKS * b + sb), 0, 0, 0))
                  for sb in range(ATT_STEP_BLOCKS)]
    out_spec = pl.BlockSpec((None, step_rows, ATT_OUT), lambda r, b: (r, b, 0))
    return pl.pallas_call(
        _att_kernel,
        grid=(dil, nd // step_rows),
        in_specs=[main(0), prev(1), main(1), nxt(1), prev(2), main(2), nxt(2)] + bias_specs,
        out_specs=[out_spec, out_spec],
        out_shape=[jax.ShapeDtypeStruct((dil, nd, ATT_OUT), BF16), jax.ShapeDtypeStruct((dil, nd, ATT_OUT), F32)],
        compiler_params=_cparams(("parallel", "parallel")),
        name=f"attention_d{dil}",
    )(qkv, qkv, qkv, qkv, qkv, qkv, qkv, *([bias_tile] * ATT_STEP_BLOCKS))


def _merge_kernel(yl_ref, ys_ref, o0_ref, l0_ref, o1_ref, l1_ref, o2_ref, l2_ref,
                  ga_ref, gb_ref, gc_ref, wl_ref, ws_ref, wa_ref, m_ref, o1_scr, l1_scr, o2_scr, l2_scr, *, tm):
    ntile = ATT_OUT // LANES

    def sequence_order(src, dst, dil):
        rows = tm // dil
        for r in range(dil):
            blk = src[r].astype(F32)
            for c in range(ntile):
                dst[c, pl.ds(r, rows, stride=dil), :] = blk[:, c * LANES:(c + 1) * LANES]
        return jnp.concatenate([dst[c] for c in range(ntile)], axis=-1)

    def gate(ref):
        return _sigmoid(ref[...].astype(F32))

    part = (gate(ga_ref) * jnp.dot(yl_ref[...], wl_ref[...], preferred_element_type=F32)
            + gate(gb_ref) * jnp.dot(ys_ref[...], ws_ref[...], preferred_element_type=F32))

    o1 = sequence_order(o1_ref, o1_scr, ATT_GROUPS[1][1])
    l1 = sequence_order(l1_ref, l1_scr, ATT_GROUPS[1][1])
    o2 = sequence_order(o2_ref, o2_scr, ATT_GROUPS[2][1])
    l2 = sequence_order(l2_ref, l2_scr, ATT_GROUPS[2][1])
    l0 = l0_ref[...]
    mx = jnp.maximum(jnp.maximum(l0, l1), l2)
    e0, e1, e2 = jnp.exp(l0 - mx), jnp.exp(l1 - mx), jnp.exp(l2 - mx)
    yatt = ((o0_ref[...].astype(F32) * e0 + o1 * e1 + o2 * e2) / (e0 + e1 + e2)).astype(BF16)
    m_ref[...] = (part + gate(gc_ref) * jnp.dot(yatt, wa_ref[...], preferred_element_type=F32)).astype(m_ref.dtype)


def _merge(ylru, gates, ys5, att, w_br_lru, w_br_s5, w_br_att, layer):
    n = gates.shape[0]
    d = D_MODEL
    tm = 256
    wl, wa = LRU_WIDTH, ATT_OUT
    gspec = [pl.BlockSpec((tm, d), lambda i, c=c: (i, c)) for c in range(3)]
    att_specs, att_args = [], []
    for (_, dil), (o, l) in zip(ATT_GROUPS, att):
        blk = (None, tm, wa) if dil == 1 else (dil, tm // dil, wa)
        att_specs += [pl.BlockSpec(blk, lambda i: (0, i, 0))] * 2
        att_args += [o, l]
    return pl.pallas_call(
        functools.partial(_merge_kernel, tm=tm),
        grid=(n // tm,),
        in_specs=[pl.BlockSpec((tm, wl), lambda i: (i, 0)), pl.BlockSpec((tm, S5_WIDTH), lambda i: (i, 0))]
        + att_specs + gspec
        + [_resident((None, wl, d), lambda i: (layer, 0, 0)), _resident((None, S5_WIDTH, d), lambda i: (layer, 0, 0)),
           _resident((None, wa, d), lambda i: (layer, 0, 0))],
        out_specs=pl.BlockSpec((tm, d), lambda i: (i, 0)),
        out_shape=jax.ShapeDtypeStruct((n, d), BF16),
        scratch_shapes=[pltpu.VMEM((wa // LANES, tm, LANES), F32)] * 4,
        compiler_params=_cparams(("parallel",)),
        name="merge",
    )(ylru, ys5, *att_args, gates, gates, gates, w_br_lru, w_br_s5, w_br_att)


def _out_proj_kernel(x_ref, m_ref, w_ref, g_ref, o_ref):
    mix = jnp.dot(m_ref[...], w_ref[...], preferred_element_type=F32)
    o_ref[...] = x_ref[...] + _rms(mix, g_ref[...])


def _out_proj(x, m, w_out, g, layer):
    n, d = x.shape
    tm = 512
    return pl.pallas_call(
        _out_proj_kernel,
        grid=(n // tm,),
        in_specs=[pl.BlockSpec((tm, d), lambda i: (i, 0)), pl.BlockSpec((tm, d), lambda i: (i, 0)),
                  _resident((None, d, d), lambda i: (layer, 0, 0)), pl.BlockSpec((1, d), lambda i: (0, 0))],
        out_specs=pl.BlockSpec((tm, d), lambda i: (i, 0)),
        out_shape=jax.ShapeDtypeStruct((n, d), F32),
        compiler_params=_cparams(("parallel",)),
        name="mixer_out",
    )(x, m, w_out, g.reshape(1, d))


def _mixer(x, g_pre, g_post, seg_lens, layer, w_in, conv_w, conv_b, lru_wa, lru_ba, lru_wx, lru_bx, lru_L,
           s5_params, s5_d, glu_w, glu_b, w_br_lru, w_br_s5, w_br_att, w_out, bias_tiles):
    gates, ze = _proj_elementwise(x, g_pre, w_in, layer)
    qkv = _proj_qkv(x, g_pre, w_in, layer)

    wg, bg = _lru_gate_weights(lru_wa, lru_ba, lru_wx, lru_bx)
    ylru = _lru(ze, conv_w, conv_b, wg, bg, lru_L.reshape(2, 1, LRU_WIDTH), seg_lens)

    y5 = _s5(ze[:, E_U * S5_WIDTH:(E_U + 1) * S5_WIDTH], _s5_prepare(*s5_params), seg_lens)
    ys5 = _s5_post(y5, ze, s5_d, glu_w, glu_b, layer)

    att = [_attention_group(qkv[g], bias_tiles[g], dil, seg_lens) for g, (_, dil) in enumerate(ATT_GROUPS)]

    m = _merge(ylru, gates, ys5, att, w_br_lru, w_br_s5, w_br_att, layer)
    return _out_proj(x, m, w_out, g_post, layer)


def kernel(x_prompt, x_sample, norm_g, w_in, lru_conv_w, lru_conv_b, lru_wa, lru_ba, lru_wx, lru_bx, lru_L,
           s5_lam_re, s5_lam_im, s5_log_dt, s5_b_re, s5_b_im, s5_c_re, s5_c_im, s5_d, s5_glu_w, s5_glu_b,
           rel_bias, w_br_lru, w_br_s5, w_br_att, w_out, ffn_w1, ffn_w3, ffn_w2):
    bp, tp, d = x_prompt.shape
    bs, ts, _ = x_sample.shape
    seg_lens = (tp,) * bp + (ts,) * bs
    rows = (bp * tp, bs * ts)
    n = sum(rows)
    depth = norm_g.shape[0]
    w1, w3, w2 = ffn_w1.astype(BF16), ffn_w3.astype(BF16), ffn_w2.astype(BF16)
    w_in_b, glu_b16, w_out_b = w_in.astype(BF16), s5_glu_w.astype(BF16), w_out.astype(BF16)
    wbl, wbs, wba = w_br_lru.astype(BF16), w_br_s5.astype(BF16), w_br_att.astype(BF16)
    bias_tiles = [_att_bias_tile(rel_bias, g, dil) for g, (_, dil) in enumerate(ATT_GROUPS)]
    s5_stack = (s5_lam_re, s5_lam_im, s5_log_dt, s5_b_re, s5_b_im, s5_c_re, s5_c_im)

    xs = [x_prompt.reshape(rows[0], d), x_sample.reshape(rows[1], d)]
    for l in range(depth):
        g = norm_g[l]
        (x,) = _ffn(xs, g[0], g[1], w1, w3, w2, l, 0, (n,))
        x = _mixer(x, g[2], g[3], seg_lens, l, w_in_b, lru_conv_w[l], lru_conv_b[l], lru_wa[l], lru_ba[l],
                   lru_wx[l], lru_bx[l], lru_L[l], [p[l] for p in s5_stack], s5_d[l], glu_b16, s5_glu_b[l],
                   wbl, wbs, wba, w_out_b,
                   bias_tiles)
        xs = _ffn([x], g[4], g[5], w1, w3, w2, l, 1, rows if l == depth - 1 else (n,))
    return (xs[0].reshape(bp, tp, d), xs[1].reshape(bs, ts, d))
```

```python
import functools
import math

import numpy as np
import jax
import jax.numpy as jnp
from jax import lax
from jax.experimental import pallas as pl
from jax.experimental.pallas import tpu as pltpu

F32 = jnp.float32
BF16 = jnp.bfloat16

D_MODEL = 2048
LRU_WIDTH = 1024
LRU_BLOCKS = 16
LRU_CONV_W = 4
LRU_C = 8.0
S5_WIDTH = 1024
S5_GROUP_CH = 16
S5_GROUPS = 64
S5_STATE = 64
HEAD_DIM = 64
ATT_GROUPS = ((128, 1), (512, 4), (2048, 16))
ATT_WIDTH = 1536
HEADS_PER_GROUP = 8
ATT_OUT = 512
REL_BUCKETS = 32
REL_MAX_DIST = 1024
RMS_EPS = 1e-6
NEG_INF = -1e30

Q_BLOCK = 128
HALF_WIN = 64
ATT_STEP_BLOCKS = 4
S5_CHUNK = 64
LANES = 128
SUBLANES = 8
COL_QKV = 3 * 1024
E_TILE = 1536
E_XL, E_GL, E_U = 0, 1, 2
VMEM_LIMIT = 48 * 1024 * 1024
MIB = 1024 * 1024
ROW_TILE = 512
PROJ_ROW_TILE = 1024
MERGE_ROW_TILE = 256
FFN_FF_TILE = 512
LRU_TILE = 256
ROW_LOOP_UNROLL = 16


def _cparams(sem, vmem_limit=VMEM_LIMIT):
    return pltpu.CompilerParams(dimension_semantics=sem, vmem_limit_bytes=vmem_limit)


def _rms(v, g):
    width = v.shape[-1]
    sq = v * v
    part = sq[:, :LANES]
    for c in range(1, width // LANES):
        part = part + sq[:, c * LANES:(c + 1) * LANES]
    ms = jnp.sum(part, axis=-1, keepdims=True) * (1.0 / width)
    return v * lax.rsqrt(ms + RMS_EPS) * g


ROW_BLOCK = 16


def _for_row_blocks(n_rows, body):
    def step(t, carry):
        body(pl.ds(pl.multiple_of(t * ROW_BLOCK, ROW_BLOCK), ROW_BLOCK))
        return carry

    lax.fori_loop(0, n_rows // ROW_BLOCK, step, 0, unroll=ROW_LOOP_UNROLL)


def _norm_rows_to(h_scr, x_ref, g_ref):
    g = g_ref[...]

    def body(rows):
        h_scr[rows, :] = _rms(x_ref[rows, :], g).astype(h_scr.dtype)

    _for_row_blocks(x_ref.shape[0], body)


def _gelu_tanh(v):
    return 0.5 * v * (1.0 + jnp.tanh(math.sqrt(2.0 / math.pi) * (v + 0.044715 * (v * v * v))))


def _sigmoid(v):
    return 0.5 * jnp.tanh(0.5 * v) + 0.5


def _any_eq(idx, values):
    hit = idx == values[0]
    for v in values[1:]:
        hit = jnp.logical_or(hit, idx == v)
    return hit


def _resident(shape, index_map):
    return pl.BlockSpec(shape, index_map, pipeline_mode=pl.Buffered(1))


def _ffn_kernel(*refs, nj, in_tiles, out_tiles):
    n_in, n_out = len(in_tiles), len(out_tiles)
    x_refs = refs[:n_in]
    gpre_ref, gpost_ref, w1_ref, w3_ref, w2_ref = refs[n_in:n_in + 5]
    o_refs = refs[n_in + 5:n_in + 5 + n_out]
    h_scr, acc_scr = refs[n_in + 5 + n_out:]
    i = pl.program_id(0)
    j = pl.program_id(1)

    def active(bounds, k):
        lo = sum(bounds[:k])
        return jnp.logical_and(i >= lo, i < lo + bounds[k])

    for k in range(n_in):
        @pl.when(jnp.logical_and(j == 0, active(in_tiles, k)))
        def _(k=k):
            _norm_rows_to(h_scr, x_refs[k], gpre_ref)
            acc_scr[...] = jnp.zeros_like(acc_scr)

    h = h_scr[...]
    a = jnp.dot(h, w1_ref[...], preferred_element_type=F32)
    b = jnp.dot(h, w3_ref[...], preferred_element_type=F32)
    g = (a * _sigmoid(a) * b).astype(BF16)
    acc_scr[...] += jnp.dot(g, w2_ref[...], preferred_element_type=F32)

    for ki in range(n_in):
        for ko in range(n_out):
            @pl.when(jnp.logical_and(j == nj - 1, jnp.logical_and(active(in_tiles, ki), active(out_tiles, ko))))
            def _(ki=ki, ko=ko):
                g_half = 0.5 * gpost_ref[...]

                def finish(rows):
                    o_refs[ko][rows, :] = x_refs[ki][rows, :] + _rms(acc_scr[rows, :], g_half)

                _for_row_blocks(acc_scr.shape[0], finish)


def _ffn(xs, g_pre, g_post, w1, w3, w2, layer, which, out_rows):
    d = xs[0].shape[1]
    dff = w1.shape[-1]
    tm, tf = ROW_TILE, FFN_FF_TILE
    nj = dff // tf
    vmem = ((len(xs) + len(out_rows)) * 2 * tm * d * 4 + 2 * 3 * d * tf * 2 + tm * d * 2 + tm * d * 4
            + 12 * MIB)
    in_tiles = tuple(x.shape[0] // tm for x in xs)
    out_tiles = tuple(r // tm for r in out_rows)

    def piece(bounds, k):
        lo = sum(bounds[:k])
        return lambda i, j: (jnp.clip(i - lo, 0, bounds[k] - 1), 0)

    const = lambda i, j: (0, 0)
    wcol = pl.BlockSpec((None, None, d, tf), lambda i, j: (layer, which, 0, j))
    outs = pl.pallas_call(
        functools.partial(_ffn_kernel, nj=nj, in_tiles=in_tiles, out_tiles=out_tiles),
        grid=(sum(in_tiles), nj),
        in_specs=[pl.BlockSpec((tm, d), piece(in_tiles, k)) for k in range(len(xs))]
        + [pl.BlockSpec((1, d), const), pl.BlockSpec((1, d), const), wcol, wcol,
           pl.BlockSpec((None, None, tf, d), lambda i, j: (layer, which, j, 0))],
        out_specs=[pl.BlockSpec((tm, d), piece(out_tiles, k)) for k in range(len(out_rows))],
        out_shape=[jax.ShapeDtypeStruct((r, d), F32) for r in out_rows],
        scratch_shapes=[pltpu.VMEM((tm, d), BF16), pltpu.VMEM((tm, d), F32)],
        compiler_params=_cparams(("parallel", "arbitrary"), vmem),
        name="ffn",
    )(*xs, g_pre.reshape(1, d), g_post.reshape(1, d), w1, w3, w2)
    return list(outs)


def _norm_mm_kernel(x_ref, g_ref, w_ref, gates_ref, e_ref, h_scr, *, ngate):
    j = pl.program_id(1)

    @pl.when(j == 0)
    def _():
        _norm_rows_to(h_scr, x_ref, g_ref)

    @pl.when(j < ngate)
    def _():
        gates_ref[...] = jnp.dot(h_scr[...], w_ref[...], preferred_element_type=F32).astype(gates_ref.dtype)

    @pl.when(j >= ngate)
    def _():
        e_ref[...] = jnp.dot(h_scr[...], w_ref[...], preferred_element_type=F32)


def _proj_elementwise(x, g, w_in, layer):
    n, d = x.shape
    tm, tn = PROJ_ROW_TILE, E_TILE
    vmem = 2 * (tm * d * 4 + d * tn * 2 + tm * tn * (2 + 4)) + tm * d * 2 + 4 * MIB
    lead = COL_QKV // tn
    skip = 3 * ATT_WIDTH // tn
    ncol = (w_in.shape[-1] - 3 * ATT_WIDTH) // tn
    ngate = ncol - lead
    return pl.pallas_call(
        functools.partial(_norm_mm_kernel, ngate=ngate),
        grid=(n // tm, ncol),
        in_specs=[pl.BlockSpec((tm, d), lambda i, j: (i, 0)), pl.BlockSpec((1, d), lambda i, j: (0, 0)),
                  pl.BlockSpec((None, d, tn),
                               lambda i, j: (layer, 0, jnp.where(j < ngate, j + lead + skip, j - ngate)))],
        out_specs=[pl.BlockSpec((tm, tn), lambda i, j: (i, jnp.minimum(j, ngate - 1))),
                   pl.BlockSpec((tm, tn), lambda i, j: (i, jnp.maximum(j - ngate, 0)))],
        out_shape=[jax.ShapeDtypeStruct((n, ngate * tn), BF16), jax.ShapeDtypeStruct((n, lead * tn), F32)],
        scratch_shapes=[pltpu.VMEM((tm, d), BF16)],
        compiler_params=_cparams(("parallel", "arbitrary"), vmem),
        name="mixer_in",
    )(x, g.reshape(1, d), w_in)


def _qkv_kernel(x_ref, g_ref, wq_ref, wk_ref, wv_ref, o0_ref, o1_ref, o2_ref, h_scr, res_scr, *, tm):
    step = pl.program_id(1)
    o_refs = (o0_ref, o1_ref, o2_ref)
    order = tuple(reversed(range(len(ATT_GROUPS))))
    ntile = 3 * ATT_OUT // LANES

    @pl.when(step == 0)
    def _():
        _norm_rows_to(h_scr, x_ref, g_ref)

    def project(gi):
        h = h_scr[...]
        cols = slice(gi * ATT_OUT, (gi + 1) * ATT_OUT)
        q = jnp.dot(h, wq_ref[:, cols], preferred_element_type=F32) * (HEAD_DIM ** -0.5)
        k = jnp.dot(h, wk_ref[:, cols], preferred_element_type=F32)
        v = jnp.dot(h, wv_ref[:, cols], preferred_element_type=F32)
        return jnp.concatenate([q, k, v], axis=-1)

    def write_residue_major(slot, gi):
        dil = ATT_GROUPS[gi][1]
        rows = tm // dil
        for r in range(dil):
            o_refs[gi][r] = jnp.concatenate(
                [res_scr[slot, c, pl.ds(r, rows, stride=dil), :] for c in range(ntile)], axis=-1).astype(BF16)

    for s, gi in enumerate(order):
        @pl.when(step == s)
        def _(s=s, gi=gi):
            res = project(gi)
            if s > 0 and ATT_GROUPS[order[s - 1]][1] > 1:
                write_residue_major((s - 1) % 2, order[s - 1])
            if ATT_GROUPS[gi][1] == 1:
                o_refs[gi][0] = res.astype(BF16)
            else:
                for c in range(ntile):
                    res_scr[s % 2, c] = res[:, c * LANES:(c + 1) * LANES]


def _proj_qkv(x, g, w_in, layer):
    n, d = x.shape
    tm = ROW_TILE
    ng = len(ATT_GROUPS)
    assert ATT_GROUPS[0][1] == 1
    first = COL_QKV // ATT_WIDTH

    def wspec(which):
        return _resident((None, d, ATT_WIDTH), lambda i, s: (layer, 0, first + which))

    return pl.pallas_call(
        functools.partial(_qkv_kernel, tm=tm),
        grid=(n // tm, ng),
        in_specs=[pl.BlockSpec((tm, d), lambda i, gq: (i, 0)), pl.BlockSpec((1, d), lambda i, gq: (0, 0)),
                  wspec(0), wspec(1), wspec(2)],
        out_specs=[pl.BlockSpec((dil, tm // dil, 3 * ATT_OUT), lambda i, gq: (0, i, 0)) for _, dil in ATT_GROUPS],
        out_shape=[jax.ShapeDtypeStruct((dil, n // dil, 3 * ATT_OUT), BF16) for _, dil in ATT_GROUPS],
        scratch_shapes=[pltpu.VMEM((tm, d), BF16), pltpu.VMEM((2, 3 * ATT_OUT // LANES, tm, LANES), F32)],
        compiler_params=_cparams(("parallel", "arbitrary"),
                                 2 * tm * d * 4 + tm * d * 2 + 3 * d * ATT_WIDTH * 2 + 2 * ng * tm * 3 * ATT_OUT * 2
                                 + 2 * tm * 3 * ATT_OUT * 4 + 8 * MIB),
        name="mixer_qkv",
    )(x, g.reshape(1, d), w_in, w_in, w_in)


def _lru_kernel(*refs, tc, nt, seg_first, seg_last, d):
    xm_ref, xp_ref, xn_ref, cw_ref, cb_ref, wg_ref, bg_ref, lam_ref = refs[:8]
    if d == 0:
        o_ref, perm_scr, a_scr, b_scr, h_scr = refs[8:]
    else:
        hf_ref, gl_ref, o_ref, perm_scr, a_scr, b_scr, h_scr = refs[8:]
    i = pl.program_id(0)
    ti = i if d == 0 else nt - 1 - i
    w = LRU_WIDTH
    at_first = _any_eq(ti, seg_first)
    at_last = _any_eq(ti, seg_last)

    sub = tc // SUBLANES
    pitch = sub + SUBLANES
    ntile = w // LANES
    for c in range(ntile):
        for jj in range(SUBLANES):
            perm_scr[c, pl.ds(jj * pitch, sub), :] = xm_ref[pl.ds(jj * sub, sub), c * LANES:(c + 1) * LANES]

    def block(k):
        return jnp.concatenate([perm_scr[c, pl.ds(k, SUBLANES, stride=pitch), :] for c in range(ntile)], axis=-1)

    sub_id = lax.broadcasted_iota(jnp.int32, (SUBLANES, w), 0)
    x_blk = [block(k) for k in range(sub)]
    before = xp_ref[...] * jnp.where(at_first, 0.0, 1.0)
    after = xn_ref[...] * jnp.where(at_last, 0.0, 1.0)

    def from_previous_subchunk(blk, edge_row):
        return jnp.where(sub_id == 0, edge_row, pltpu.roll(blk, 1, axis=0))

    def from_next_subchunk(blk, edge_row):
        return jnp.where(sub_id == SUBLANES - 1, edge_row, pltpu.roll(blk, SUBLANES - 1, axis=0))

    def shifted(k, off):
        kk = k + off
        if 0 <= kk < sub:
            return x_blk[kk]
        if kk < 0:
            return from_previous_subchunk(x_blk[kk + sub], before[SUBLANES + kk:SUBLANES + kk + 1, :])
        return from_next_subchunk(x_blk[kk - sub], after[kk - sub:kk - sub + 1, :])

    left = LRU_CONV_W // 2
    taps = [cw_ref[pl.ds(kk, 1), :] for kk in range(LRU_CONV_W)]
    xc = jnp.concatenate(
        [cb_ref[...] + sum(shifted(k, kk - left) * taps[kk] for kk in range(LRU_CONV_W)) for k in range(sub)], axis=0)

    xcb = xc.astype(BF16)
    parts = [jnp.dot(xcb[:, p * LANES:(p + 1) * LANES], wg_ref[p], preferred_element_type=F32)
             for p in range(w // LANES)]
    r = _sigmoid(jnp.concatenate([g[:, :LANES] for g in parts], axis=-1) + bg_ref[:, :w])
    ig = _sigmoid(jnp.concatenate([g[:, LANES:] for g in parts], axis=-1) + bg_ref[:, w:])
    nlam = -lam_ref[...]
    softplus = jnp.maximum(nlam, 0.0) + jnp.log(1.0 + jnp.exp(-jnp.abs(nlam)))
    a = jnp.exp(-LRU_C * r * softplus)
    a_scr[...] = a
    b_scr[...] = jnp.sqrt(1.0 - a * a) * (ig * xc)

    @pl.when(at_first if d == 0 else at_last)
    def _():
        h_scr[...] = jnp.zeros_like(h_scr)

    hl = jnp.zeros((SUBLANES, w), F32)
    pp = jnp.ones((SUBLANES, w), F32)
    for k in (range(sub) if d == 0 else range(sub - 1, -1, -1)):
        rows = pl.ds(k * SUBLANES, SUBLANES)
        av = a_scr[rows, :]
        hl = av * hl + b_scr[rows, :]
        pp = av * pp
        a_scr[rows, :] = pp
        b_scr[rows, :] = hl
    cur = h_scr[pl.ds(0, 1), :]
    carry = jnp.zeros((SUBLANES, w), F32)
    for jj in (range(SUBLANES) if d == 0 else range(SUBLANES - 1, -1, -1)):
        carry = jnp.where(sub_id == jj, cur, carry)
        cur = hl[jj:jj + 1, :] + pp[jj:jj + 1, :] * cur
    h_scr[pl.ds(0, 1), :] = cur
    for k in range(sub):
        rows = pl.ds(k * SUBLANES, SUBLANES)
        hv = b_scr[rows, :] + a_scr[rows, :] * carry
        for c in range(ntile):
            perm_scr[c, pl.ds(k, SUBLANES, stride=pitch), :] = hv[:, c * LANES:(c + 1) * LANES]
    for jj in range(SUBLANES):
        rows = pl.ds(jj * sub, sub)
        hv = jnp.concatenate([perm_scr[c, pl.ds(jj * pitch, sub), :] for c in range(ntile)], axis=-1)
        if d == 0:
            o_ref[rows, :] = hv
        else:
            o_ref[rows, :] = ((hf_ref[rows, :] + hv) * _gelu_tanh(gl_ref[rows, :])).astype(o_ref.dtype)


def _lru(ze, conv_w, conv_b, wg, bg, lam, seg_lens):
    n = ze.shape[0]
    w = LRU_WIDTH
    tc = LRU_TILE
    nt = n // tc
    starts = np.cumsum((0,) + tuple(seg_lens))
    seg_first = tuple(int(s) // tc for s in starts[:-1])
    seg_last = tuple(int(s) // tc - 1 for s in starts[1:])
    hb = tc // SUBLANES
    nhb = n // SUBLANES
    pitched = SUBLANES * (tc // SUBLANES + SUBLANES)

    def one_direction(d, extra_specs, extra_args, out_dtype):
        tile = (lambda i: i) if d == 0 else (lambda i: nt - 1 - i)
        return pl.pallas_call(
            functools.partial(_lru_kernel, tc=tc, nt=nt, seg_first=seg_first, seg_last=seg_last, d=d),
            grid=(nt,),
            in_specs=[pl.BlockSpec((tc, w), lambda i: (tile(i), E_XL)),
                      pl.BlockSpec((SUBLANES, w), lambda i: (jnp.maximum(tile(i) * hb - 1, 0), E_XL)),
                      pl.BlockSpec((SUBLANES, w), lambda i: (jnp.minimum((tile(i) + 1) * hb, nhb - 1), E_XL)),
                      pl.BlockSpec((LRU_CONV_W, w), lambda i: (0, 0)),
                      pl.BlockSpec((1, w), lambda i: (0, 0)),
                      _resident((None, w // LANES, LANES, 2 * LANES), lambda i: (d, 0, 0, 0)),
                      pl.BlockSpec((None, 1, 2 * w), lambda i: (d, 0, 0)),
                      pl.BlockSpec((None, 1, w), lambda i: (d, 0, 0))] + [spec(tile) for spec in extra_specs],
            out_specs=pl.BlockSpec((tc, w), lambda i: (tile(i), 0)),
            out_shape=jax.ShapeDtypeStruct((n, w), out_dtype),
            scratch_shapes=[pltpu.VMEM((w // LANES, pitched, LANES), F32), pltpu.VMEM((tc, w), F32),
                            pltpu.VMEM((tc, w), F32), pltpu.VMEM((SUBLANES, w), F32)],
            compiler_params=_cparams(("arbitrary",)),
            name="rglru_fwd" if d == 0 else "rglru_bwd",
        )(ze, ze, ze, conv_w, conv_b.reshape(1, w), wg, bg, lam, *extra_args)

    hf = one_direction(0, [], [], F32)
    return one_direction(1, [lambda tile: pl.BlockSpec((tc, w), lambda i: (tile(i), 0)),
                             lambda tile: pl.BlockSpec((tc, w), lambda i: (tile(i), E_GL))], [hf, ze], BF16)


def _lru_gate_weights(wa, ba, wx, bx):
    per_tile = LANES * LRU_BLOCKS // LRU_WIDTH

    def tiles(wb):
        bw = wb.shape[-1]
        eye = jnp.eye(per_tile, dtype=wb.dtype)
        grouped = wb.reshape(2, LRU_BLOCKS // per_tile, per_tile, bw, bw)
        full = grouped[:, :, :, :, None, :] * eye[None, None, :, None, :, None]
        return full.reshape(2, LRU_BLOCKS // per_tile, LANES, LANES)

    wg = jnp.concatenate([tiles(wa), tiles(wx)], axis=-1).astype(BF16)
    bg = jnp.concatenate([ba, bx], axis=-1).reshape(2, 1, 2 * LRU_WIDTH)
    return wg, bg


S5_GROUP_BATCH = 8


def _s5_ktable_kernel(b_ref, w_ref, o_ref):
    for i in range(S5_GROUP_BATCH):
        o_ref[i] = jnp.dot(b_ref[i], w_ref[i], preferred_element_type=F32, precision=lax.Precision.HIGHEST)


def _s5_tables(lam_re, lam_im, log_dt, b_re, b_im, c_re, c_im):
    L, G, P, C = S5_CHUNK, S5_GROUPS, S5_STATE, S5_GROUP_CH
    dt = jnp.exp(log_dt)[..., None]
    mag = jnp.exp(lam_re * dt)
    ar = mag * jnp.cos(lam_im * dt)
    ai = mag * jnp.sin(lam_im * dt)
    den = lam_re * lam_re + lam_im * lam_im
    cr = ((ar - 1.0) * lam_re + ai * lam_im) / den
    ci = (ai * lam_re - (ar - 1.0) * lam_im) / den
    bbr = cr[..., None] * b_re - ci[..., None] * b_im
    bbi = cr[..., None] * b_im + ci[..., None] * b_re
    pr, pi = jnp.ones_like(ar)[None], jnp.zeros_like(ai)[None]
    nr, ni = ar, ai
    while pr.shape[0] < L + 1:
        pr, pi = (jnp.concatenate([pr, pr * nr - pi * ni], axis=0),
                  jnp.concatenate([pi, pr * ni + pi * nr], axis=0))
        nr, ni = nr * nr - ni * ni, 2.0 * nr * ni
    pr, pi = pr[:L + 1], pi[:L + 1]
    zr = pr[:L, ..., None] * bbr - pi[:L, ..., None] * bbi
    zi = pr[:L, ..., None] * bbi + pi[:L, ..., None] * bbr

    def c_pow(powers_r, powers_i):
        ctr = jnp.transpose(c_re, (0, 1, 3, 2))[:, :, :, None, :]
        cti = jnp.transpose(c_im, (0, 1, 3, 2))[:, :, :, None, :]
        qr = jnp.transpose(powers_r, (1, 2, 3, 0))[..., None]
        qi = jnp.transpose(powers_i, (1, 2, 3, 0))[..., None]
        return ctr * qr - cti * qi, ctr * qi + cti * qr

    wr, wi = c_pow(pr[:L], pi[:L])
    wmat = jnp.concatenate([wr, wi], axis=2).reshape(2 * G, 2 * P, L * C)
    bmat = jnp.concatenate([jnp.transpose(bbr, (0, 1, 3, 2)), -jnp.transpose(bbi, (0, 1, 3, 2))],
                           axis=-1).reshape(2 * G, C, 2 * P)

    def w_in(z, flip):
        zf = z[::-1] if flip else z
        return jnp.transpose(zf, (1, 0, 3, 2)).reshape(G, L * C, P)

    win = jnp.concatenate([w_in(zr[:, 0], True), w_in(zr[:, 1], False),
                           w_in(zi[:, 0], True), w_in(zi[:, 1], False)], axis=-1).astype(BF16)

    fr, fi = c_pow(pr[1:], pi[1:])
    br, bi = c_pow(pr[:0:-1], pi[:0:-1])
    wout = jnp.concatenate([fr[0], br[1], -fi[0], -bi[1]], axis=1).reshape(G, 4 * P, L * C).astype(BF16)
    al = jnp.concatenate([pr[L, 0], pr[L, 1], pi[L, 0], pi[L, 1]], axis=-1)
    return bmat, wmat, win, wout, al


def _s5_lag_table(kt):
    L, G, C = S5_CHUNK, S5_GROUPS, S5_GROUP_CH
    kk = jnp.concatenate([kt[1, :, :, :0:-1], kt[0, :, :, :1] + kt[1, :, :, :1], kt[0, :, :, 1:]], axis=2)
    kk = kk.reshape(G, C, (2 * L - 1) * C)
    return jnp.pad(kk, ((0, 0), (0, 0), (0, 2 * L * C - kk.shape[-1])))


def _s5_prepare(lam_re, lam_im, log_dt, b_re, b_im, c_re, c_im):
    L, G, P, C = S5_CHUNK, S5_GROUPS, S5_STATE, S5_GROUP_CH
    bmat, wmat, win, wout, al = _s5_tables(lam_re, lam_im, log_dt, b_re, b_im, c_re, c_im)
    kt = pl.pallas_call(
        _s5_ktable_kernel,
        grid=(2 * G // S5_GROUP_BATCH,),
        in_specs=[pl.BlockSpec((S5_GROUP_BATCH, C, 2 * P), lambda g: (g, 0, 0)),
                  pl.BlockSpec((S5_GROUP_BATCH, 2 * P, L * C), lambda g: (g, 0, 0))],
        out_specs=pl.BlockSpec((S5_GROUP_BATCH, C, L * C), lambda g: (g, 0, 0)),
        out_shape=jax.ShapeDtypeStruct((2 * G, C, L * C), F32),
        compiler_params=_cparams(("parallel",)),
        name="s5_ktable",
    )(bmat, wmat)
    return _s5_lag_table(kt.reshape(2, G, C, L, C)), win, wout, al


def _s5_state_kernel(v_ref, win_ref, o_ref):
    width = win_ref.shape[-1]
    for i in range(S5_GROUP_BATCH):
        o_ref[:, i * width:(i + 1) * width] = jnp.dot(v_ref[i], win_ref[i], preferred_element_type=F32)


def _s5_scan_kernel(s_ref, al_ref, o_ref, *, seg_chunks):
    p2 = 2 * S5_STATE
    alr = al_ref[:, :p2]
    ali = al_ref[:, p2:]
    is_fwd = lax.broadcasted_iota(jnp.int32, alr.shape, 1) < S5_STATE
    zero = jnp.zeros_like(alr)
    start = 0
    for n_chunks in seg_chunks:
        def fwd(k, carry, start=start):
            xr, xi = carry
            c = start + k
            o_ref[c, :, :p2] = xr
            o_ref[c, :, p2:] = xi
            sr = s_ref[c, :, :p2]
            si = s_ref[c, :, p2:]
            return alr * xr - ali * xi + sr, alr * xi + ali * xr + si

        lax.fori_loop(0, n_chunks, fwd, (zero, zero))

        def bwd(k, carry, start=start, n_chunks=n_chunks):
            xr, xi = carry
            c = start + n_chunks - 1 - k
            o_ref[c, :, :p2] = jnp.where(is_fwd, o_ref[c, :, :p2], xr)
            o_ref[c, :, p2:] = jnp.where(is_fwd, o_ref[c, :, p2:], xi)
            sr = s_ref[c, :, :p2]
            si = s_ref[c, :, p2:]
            return alr * xr - ali * xi + sr, alr * xi + ali * xr + si

        lax.fori_loop(0, n_chunks, bwd, (zero, zero))
        start += n_chunks


def _s5_out_kernel(v_ref, kk_ref, x_ref, wout_ref, o_ref, mt_scr):
    L, C = S5_CHUNK, S5_GROUP_CH
    kk = kk_ref[...]
    width = kk.shape[-1]
    per_tile = LANES // C
    for rot in range(per_tile):
        shifted = kk if rot == 0 else pltpu.roll(kk, width - rot * C, axis=1)
        shifted = shifted.astype(BF16)
        for s in range(L):
            lag0 = L - 1 - s
            if lag0 % per_tile == rot:
                col = (lag0 // per_tile) * LANES
                mt_scr[pl.ds(s * C, C), :] = shifted[:, col:col + L * C]
    o_ref[...] = (jnp.dot(v_ref[...], mt_scr[...], preferred_element_type=F32)
                  + jnp.dot(x_ref[...].astype(BF16), wout_ref[...], preferred_element_type=F32)).astype(o_ref.dtype)


def _s5(u, tables, seg_lens):
    kk, win, wout, al = tables
    L, G, P, C = S5_CHUNK, S5_GROUPS, S5_STATE, S5_GROUP_CH
    n = u.shape[0]
    nc = n // L
    lc = L * C
    v = jnp.transpose(u.reshape(nc, L, G, C), (2, 0, 1, 3)).reshape(G, nc, lc).astype(BF16)
    states = pl.pallas_call(
        _s5_state_kernel,
        grid=(G // S5_GROUP_BATCH,),
        in_specs=[pl.BlockSpec((S5_GROUP_BATCH, nc, lc), lambda g: (g, 0, 0)),
                  pl.BlockSpec((S5_GROUP_BATCH, lc, 4 * P), lambda g: (g, 0, 0))],
        out_specs=pl.BlockSpec((nc, S5_GROUP_BATCH * 4 * P), lambda g: (0, g)),
        out_shape=jax.ShapeDtypeStruct((nc, G * 4 * P), F32),
        compiler_params=_cparams(("parallel",)),
        name="s5_chunk_state",
    )(v, win)
    gb = SUBLANES
    carried = pl.pallas_call(
        functools.partial(_s5_scan_kernel, seg_chunks=tuple(t // L for t in seg_lens)),
        grid=(G // gb,),
        in_specs=[pl.BlockSpec((nc, gb, 4 * P), lambda g: (0, g, 0)),
                  pl.BlockSpec((gb, 4 * P), lambda g: (g, 0))],
        out_specs=pl.BlockSpec((nc, gb, 4 * P), lambda g: (0, g, 0)),
        out_shape=jax.ShapeDtypeStruct((nc, G, 4 * P), F32),
        compiler_params=_cparams(("parallel",)),
        name="s5_chunk_scan",
    )(states.reshape(nc, G, 4 * P), al)
    y = pl.pallas_call(
        _s5_out_kernel,
        grid=(G,),
        in_specs=[pl.BlockSpec((None, nc, lc), lambda g: (g, 0, 0)),
                  pl.BlockSpec((None, C, 2 * lc), lambda g: (g, 0, 0)),
                  pl.BlockSpec((nc, 4 * P), lambda g: (0, g)),
                  pl.BlockSpec((None, 4 * P, lc), lambda g: (g, 0, 0))],
        out_specs=pl.BlockSpec((None, nc, lc), lambda g: (g, 0, 0)),
        out_shape=jax.ShapeDtypeStruct((G, nc, lc), BF16),
        scratch_shapes=[pltpu.VMEM((lc, lc), BF16)],
        compiler_params=_cparams(("parallel",)),
        name="s5_chunk_out",
    )(v, kk, carried.reshape(nc, G * 4 * P), wout)
    return jnp.transpose(y.reshape(G, nc, L, C), (1, 2, 0, 3)).reshape(n, G * C)


def _s5_post_kernel(y_ref, u_ref, d_ref, w_ref, b_ref, o_ref):
    y1 = _gelu_tanh(y_ref[...].astype(F32) + d_ref[...] * u_ref[...])
    gate = jnp.dot(y1.astype(BF16), w_ref[...], preferred_element_type=F32) + b_ref[...]
    o_ref[...] = (y1 * _sigmoid(gate)).astype(o_ref.dtype)


def _s5_post(y5, ze, s5_d, glu_w, glu_b, layer):
    n, w = y5.shape
    tm = ROW_TILE
    return pl.pallas_call(
        _s5_post_kernel,
        grid=(n // tm,),
        in_specs=[pl.BlockSpec((tm, w), lambda i: (i, 0)), pl.BlockSpec((tm, w), lambda i: (i, E_U)),
                  pl.BlockSpec((1, w), lambda i: (0, 0)), pl.BlockSpec((None, w, w), lambda i: (layer, 0, 0)),
                  pl.BlockSpec((1, w), lambda i: (0, 0))],
        out_specs=pl.BlockSpec((tm, w), lambda i: (i, 0)),
        out_shape=jax.ShapeDtypeStruct((n, w), BF16),
        compiler_params=_cparams(("parallel",)),
        name="s5_post",
    )(y5, ze, s5_d.reshape(1, w), glu_w, glu_b.reshape(1, w))


def _t5_buckets(rel):
    half = REL_BUCKETS // 2
    max_exact = half // 2
    sign = (rel > 0).astype(np.int32) * half
    n = np.abs(rel)
    large = max_exact + (np.log(np.maximum(n, 1) / max_exact)
                         / np.log(REL_MAX_DIST / max_exact) * (half - max_exact)).astype(np.int32)
    large = np.minimum(large, half - 1)
    return sign + np.where(n < max_exact, n, large)


def _att_bias_tile(rel_bias, group, dil):
    ncol = Q_BLOCK + 2 * HALF_WIN
    hs = slice(group * HEADS_PER_GROUP, (group + 1) * HEADS_PER_GROUP)
    offs = np.arange(-HALF_WIN, HALF_WIN + 1)
    vals = jnp.transpose(rel_bias[:, hs][_t5_buckets(offs * dil)]).astype(F32)
    width = 2 * ncol
    pad_lo = Q_BLOCK
    line = jnp.pad(vals, ((0, 0), (pad_lo, width - pad_lo - vals.shape[1])), constant_values=NEG_INF)
    rows = jnp.broadcast_to(line[:, None, :], (HEADS_PER_GROUP, Q_BLOCK, width)).reshape(HEADS_PER_GROUP, -1)
    skew = rows[:, :Q_BLOCK * (width - 1)].reshape(HEADS_PER_GROUP, Q_BLOCK, width - 1)
    tile = skew[:, :, pad_lo:pad_lo + ncol]
    col = np.arange(ncol)[None, None, :]
    before = col < HALF_WIN
    after = col >= HALF_WIN + Q_BLOCK
    return jnp.stack([tile, jnp.where(before, NEG_INF, tile), jnp.where(after, NEG_INF, tile),
                      jnp.where(before | after, NEG_INF, tile)])


def _att_kernel(q_ref, kp_ref, km_ref, kn_ref, vp_ref, vm_ref, vn_ref, *rest):
    bias_refs = rest[:ATT_STEP_BLOCKS]
    o_ref, lse_ref = rest[ATT_STEP_BLOCKS:]
    ncol = Q_BLOCK + 2 * HALF_WIN
    main_rows = ATT_STEP_BLOCKS * Q_BLOCK
    low = lax.broadcasted_iota(jnp.int32, (Q_BLOCK, LANES), 1) < HEAD_DIM
    ones = jnp.ones((ncol, LANES), BF16)
    zero = jnp.zeros((Q_BLOCK, LANES), BF16)
    heads = range(HEADS_PER_GROUP)
    pair_cols = [slice((h // 2) * LANES, (h // 2 + 1) * LANES) for h in heads]

    def window(prev_ref, main_ref, next_ref, sb):
        lo = sb * Q_BLOCK - HALF_WIN
        parts = []
        if lo < 0:
            parts.append(prev_ref[...])
        m_lo, m_hi = max(lo, 0), min(lo + ncol, main_rows)
        parts.append(main_ref[pl.ds(m_lo, m_hi - m_lo), :])
        if lo + ncol > main_rows:
            parts.append(next_ref[...])
        return jnp.concatenate(parts, axis=0) if len(parts) > 1 else parts[0]

    for sb in range(ATT_STEP_BLOCKS):
        rows = pl.ds(sb * Q_BLOCK, Q_BLOCK)
        q = q_ref[rows, :]
        k = window(kp_ref, km_ref, kn_ref, sb)
        v = window(vp_ref, vm_ref, vn_ref, sb)
        scores = []
        for h in heads:
            q2 = q[:, pair_cols[h]]
            qh = jnp.where(low, q2, zero) if h % 2 == 0 else jnp.where(low, zero, q2)
            s = lax.dot_general(qh, k[:, pair_cols[h]], (((1,), (1,)), ((), ())), preferred_element_type=F32)
            scores.append(s + bias_refs[sb][h])
        maxes = [jnp.max(s, axis=-1, keepdims=True) for s in scores]
        probs = [jnp.exp(s - m).astype(BF16) for s, m in zip(scores, maxes)]
        outs = [jnp.dot(p, v[:, pair_cols[h]], preferred_element_type=F32) for h, p in zip(heads, probs)]
        sums = [jnp.dot(p, ones, preferred_element_type=F32) for p in probs]
        for pair in range(HEADS_PER_GROUP // 2):
            a, b = 2 * pair, 2 * pair + 1
            l = jnp.where(low, sums[a], sums[b])
            o_ref[rows, pair_cols[a]] = (jnp.where(low, outs[a], outs[b]) / l).astype(o_ref.dtype)
            lse_ref[rows, pair_cols[a]] = jnp.where(low, maxes[a], maxes[b]) + jnp.log(l)


def _attention_group(qkv, bias_tile, dil, seg_lens):
    nd = qkv.shape[1]
    step_rows = ATT_STEP_BLOCKS * Q_BLOCK
    halves_per_step = step_rows // HALF_WIN
    assert nd % step_rows == 0
    nhalf = nd // HALF_WIN
    starts = np.cumsum((0,) + tuple(seg_lens)) // (dil * Q_BLOCK)
    blk_first = tuple(int(s) for s in starts[:-1])
    blk_last = tuple(int(s) - 1 for s in starts[1:])
    ncol = Q_BLOCK + 2 * HALF_WIN

    def variant(blk):
        return _any_eq(blk, blk_first).astype(jnp.int32) + 2 * _any_eq(blk, blk_last).astype(jnp.int32)

    def main(cblk):
        return pl.BlockSpec((None, step_rows, ATT_OUT), lambda r, b: (r, b, cblk))

    def prev(cblk):
        return pl.BlockSpec((None, HALF_WIN, ATT_OUT),
                            lambda r, b: (r, jnp.maximum(halves_per_step * b - 1, 0), cblk))

    def nxt(cblk):
        return pl.BlockSpec((None, HALF_WIN, ATT_OUT),
                            lambda r, b: (r, jnp.minimum(halves_per_step * (b + 1), nhalf - 1), cblk))

    bias_specs = [pl.BlockSpec((None, HEADS_PER_GROUP, Q_BLOCK, ncol),
                               lambda r, b, sb=sb: (variant(ATT_STEP_BLOCKS * b + sb), 0, 0, 0))
                  for sb in range(ATT_STEP_BLOCKS)]
    out_spec = pl.BlockSpec((None, step_rows, ATT_OUT), lambda r, b: (r, b, 0))
    return pl.pallas_call(
        _att_kernel,
        grid=(dil, nd // step_rows),
        in_specs=[main(0), prev(1), main(1), nxt(1), prev(2), main(2), nxt(2)] + bias_specs,
        out_specs=[out_spec, out_spec],
        out_shape=[jax.ShapeDtypeStruct((dil, nd, ATT_OUT), BF16), jax.ShapeDtypeStruct((dil, nd, ATT_OUT), F32)],
        compiler_params=_cparams(("parallel", "parallel")),
        name=f"attention_d{dil}",
    )(qkv, qkv, qkv, qkv, qkv, qkv, qkv, *([bias_tile] * ATT_STEP_BLOCKS))


def _merge_kernel(yl_ref, ys_ref, o0_ref, l0_ref, o1_ref, l1_ref, o2_ref, l2_ref,
                  ga_ref, gb_ref, gc_ref, wl_ref, ws_ref, wa_ref, m_ref, o1_scr, l1_scr, o2_scr, l2_scr, *, tm):
    ntile = ATT_OUT // LANES

    def sequence_order(src, dst, dil):
        rows = tm // dil
        for r in range(dil):
            blk = src[r].astype(F32)
            for c in range(ntile):
                dst[c, pl.ds(r, rows, stride=dil), :] = blk[:, c * LANES:(c + 1) * LANES]
        return jnp.concatenate([dst[c] for c in range(ntile)], axis=-1)

    def gate(ref):
        return _sigmoid(ref[...].astype(F32))

    part = (gate(ga_ref) * jnp.dot(yl_ref[...], wl_ref[...], preferred_element_type=F32)
            + gate(gb_ref) * jnp.dot(ys_ref[...], ws_ref[...], preferred_element_type=F32))

    o1 = sequence_order(o1_ref, o1_scr, ATT_GROUPS[1][1])
    l1 = sequence_order(l1_ref, l1_scr, ATT_GROUPS[1][1])
    o2 = sequence_order(o2_ref, o2_scr, ATT_GROUPS[2][1])
    l2 = sequence_order(l2_ref, l2_scr, ATT_GROUPS[2][1])
    l0 = l0_ref[...]
    mx = jnp.maximum(jnp.maximum(l0, l1), l2)
    e0, e1, e2 = jnp.exp(l0 - mx), jnp.exp(l1 - mx), jnp.exp(l2 - mx)
    yatt = ((o0_ref[...].astype(F32) * e0 + o1 * e1 + o2 * e2) / (e0 + e1 + e2)).astype(BF16)
    m_ref[...] = (part + gate(gc_ref) * jnp.dot(yatt, wa_ref[...], preferred_element_type=F32)).astype(m_ref.dtype)


def _merge(ylru, gates, ys5, att, w_br_lru, w_br_s5, w_br_att, layer):
    n = gates.shape[0]
    d = D_MODEL
    tm = MERGE_ROW_TILE
    wl, wa = LRU_WIDTH, ATT_OUT
    gspec = [pl.BlockSpec((tm, d), lambda i, c=c: (i, c)) for c in range(3)]
    att_specs, att_args = [], []
    for (_, dil), (o, l) in zip(ATT_GROUPS, att):
        blk = (None, tm, wa) if dil == 1 else (dil, tm // dil, wa)
        att_specs += [pl.BlockSpec(blk, lambda i: (0, i, 0))] * 2
        att_args += [o, l]
    return pl.pallas_call(
        functools.partial(_merge_kernel, tm=tm),
        grid=(n // tm,),
        in_specs=[pl.BlockSpec((tm, wl), lambda i: (i, 0)), pl.BlockSpec((tm, S5_WIDTH), lambda i: (i, 0))]
        + att_specs + gspec
        + [_resident((None, wl, d), lambda i: (layer, 0, 0)), _resident((None, S5_WIDTH, d), lambda i: (layer, 0, 0)),
           _resident((None, wa, d), lambda i: (layer, 0, 0))],
        out_specs=pl.BlockSpec((tm, d), lambda i: (i, 0)),
        out_shape=jax.ShapeDtypeStruct((n, d), BF16),
        scratch_shapes=[pltpu.VMEM((wa // LANES, tm, LANES), F32)] * 4,
        compiler_params=_cparams(("parallel",)),
        name="merge",
    )(ylru, ys5, *att_args, gates, gates, gates, w_br_lru, w_br_s5, w_br_att)


def _out_proj_kernel(x_ref, m_ref, w_ref, g_ref, o_ref):
    mix = jnp.dot(m_ref[...], w_ref[...], preferred_element_type=F32)
    o_ref[...] = x_ref[...] + _rms(mix, g_ref[...])


def _out_proj(x, m, w_out, g, layer):
    n, d = x.shape
    tm = ROW_TILE
    return pl.pallas_call(
        _out_proj_kernel,
        grid=(n // tm,),
        in_specs=[pl.BlockSpec((tm, d), lambda i: (i, 0)), pl.BlockSpec((tm, d), lambda i: (i, 0)),
                  _resident((None, d, d), lambda i: (layer, 0, 0)), pl.BlockSpec((1, d), lambda i: (0, 0))],
        out_specs=pl.BlockSpec((tm, d), lambda i: (i, 0)),
        out_shape=jax.ShapeDtypeStruct((n, d), F32),
        compiler_params=_cparams(("parallel",)),
        name="mixer_out",
    )(x, m, w_out, g.reshape(1, d))


def _mixer(x, g_pre, g_post, seg_lens, layer, w_in, conv_w, conv_b, lru_wa, lru_ba, lru_wx, lru_bx, lru_L,
           s5_params, s5_d, glu_w, glu_b, w_br_lru, w_br_s5, w_br_att, w_out, bias_tiles):
    gates, ze = _proj_elementwise(x, g_pre, w_in, layer)
    qkv = _proj_qkv(x, g_pre, w_in, layer)

    wg, bg = _lru_gate_weights(lru_wa, lru_ba, lru_wx, lru_bx)
    ylru = _lru(ze, conv_w, conv_b, wg, bg, lru_L.reshape(2, 1, LRU_WIDTH), seg_lens)

    y5 = _s5(ze[:, E_U * S5_WIDTH:(E_U + 1) * S5_WIDTH], _s5_prepare(*s5_params), seg_lens)
    ys5 = _s5_post(y5, ze, s5_d, glu_w, glu_b, layer)

    att = [_attention_group(qkv[g], bias_tiles[g], dil, seg_lens) for g, (_, dil) in enumerate(ATT_GROUPS)]

    m = _merge(ylru, gates, ys5, att, w_br_lru, w_br_s5, w_br_att, layer)
    return _out_proj(x, m, w_out, g_post, layer)


def kernel(x_prompt, x_sample, norm_g, w_in, lru_conv_w, lru_conv_b, lru_wa, lru_ba, lru_wx, lru_bx, lru_L,
           s5_lam_re, s5_lam_im, s5_log_dt, s5_b_re, s5_b_im, s5_c_re, s5_c_im, s5_d, s5_glu_w, s5_glu_b,
           rel_bias, w_br_lru, w_br_s5, w_br_att, w_out, ffn_w1, ffn_w3, ffn_w2):
    bp, tp, d = x_prompt.shape
    bs, ts, _ = x_sample.shape
    seg_lens = (tp,) * bp + (ts,) * bs
    rows = (bp * tp, bs * ts)
    n = sum(rows)
    depth = norm_g.shape[0]
    w1, w3, w2 = ffn_w1.astype(BF16), ffn_w3.astype(BF16), ffn_w2.astype(BF16)
    w_in_b, glu_b16, w_out_b = w_in.astype(BF16), s5_glu_w.astype(BF16), w_out.astype(BF16)
    wbl, wbs, wba = w_br_lru.astype(BF16), w_br_s5.astype(BF16), w_br_att.astype(BF16)
    bias_tiles = [_att_bias_tile(rel_bias, g, dil) for g, (_, dil) in enumerate(ATT_GROUPS)]
    s5_stack = (s5_lam_re, s5_lam_im, s5_log_dt, s5_b_re, s5_b_im, s5_c_re, s5_c_im)

    xs = [x_prompt.reshape(rows[0], d), x_sample.reshape(rows[1], d)]
    for l in range(depth):
        g = norm_g[l]
        (x,) = _ffn(xs, g[0], g[1], w1, w3, w2, l, 0, (n,))
        x = _mixer(x, g[2], g[3], seg_lens, l, w_in_b, lru_conv_w[l], lru_conv_b[l], lru_wa[l], lru_ba[l],
                   lru_wx[l], lru_bx[l], lru_L[l], [p[l] for p in s5_stack], s5_d[l], glu_b16, s5_glu_b[l],
                   wbl, wbs, wba, w_out_b,
                   bias_tiles)
        xs = _ffn([x], g[4], g[5], w1, w3, w2, l, 1, rows if l == depth - 1 else (n,))
    return (xs[0].reshape(bp, tp, d), xs[1].reshape(bs, ts, d))
```

```python
import functools
import math

import numpy as np
import jax
import jax.numpy as jnp
from jax import lax
from jax.experimental import pallas as pl
from jax.experimental.pallas import tpu as pltpu

F32 = jnp.float32
BF16 = jnp.bfloat16

D_MODEL = 2048
LRU_WIDTH = 1024
LRU_BLOCKS = 16
LRU_CONV_W = 4
LRU_C = 8.0
S5_WIDTH = 1024
S5_GROUP_CH = 16
S5_GROUPS = 64
S5_STATE = 64
HEAD_DIM = 64
ATT_GROUPS = ((128, 1), (512, 4), (2048, 16))
ATT_WIDTH = 1536
HEADS_PER_GROUP = 8
ATT_OUT = 512
REL_BUCKETS = 32
REL_MAX_DIST = 1024
RMS_EPS = 1e-6
NEG_INF = -1e30

Q_BLOCK = 128
HALF_WIN = 64
ATT_STEP_BLOCKS = 4
S5_CHUNK = 64
LANES = 128
SUBLANES = 8
COL_QKV = 3 * 1024
E_TILE = 1536
E_XL, E_GL, E_U = 0, 1, 2
VMEM_LIMIT = 48 * 1024 * 1024
MIB = 1024 * 1024
ROW_TILE = 512
PROJ_ROW_TILE = 1024
MERGE_ROW_TILE = 256
FFN_FF_TILE = 512
LRU_TILE = 256
ROW_LOOP_UNROLL = 16


def _cparams(sem, vmem_limit=VMEM_LIMIT):
    return pltpu.CompilerParams(dimension_semantics=sem, vmem_limit_bytes=vmem_limit)


def _rms(v, g):
    width = v.shape[-1]
    sq = v * v
    part = sq[:, :LANES]
    for c in range(1, width // LANES):
        part = part + sq[:, c * LANES:(c + 1) * LANES]
    ms = jnp.sum(part, axis=-1, keepdims=True) * (1.0 / width)
    return v * lax.rsqrt(ms + RMS_EPS) * g


ROW_BLOCK = 16


def _for_row_blocks(n_rows, body):
    def step(t, carry):
        body(pl.ds(pl.multiple_of(t * ROW_BLOCK, ROW_BLOCK), ROW_BLOCK))
        return carry

    lax.fori_loop(0, n_rows // ROW_BLOCK, step, 0, unroll=ROW_LOOP_UNROLL)


def _norm_rows_to(h_scr, x_ref, g_ref):
    g = g_ref[...]

    def body(rows):
        h_scr[rows, :] = _rms(x_ref[rows, :], g).astype(h_scr.dtype)

    _for_row_blocks(x_ref.shape[0], body)


def _gelu_tanh(v):
    return 0.5 * v * (1.0 + jnp.tanh(math.sqrt(2.0 / math.pi) * (v + 0.044715 * (v * v * v))))


def _sigmoid(v):
    return 0.5 * jnp.tanh(0.5 * v) + 0.5


def _any_eq(idx, values):
    hit = idx == values[0]
    for v in values[1:]:
        hit = jnp.logical_or(hit, idx == v)
    return hit


def _resident(shape, index_map):
    return pl.BlockSpec(shape, index_map, pipeline_mode=pl.Buffered(1))


def _ffn_kernel(*refs, nj, in_tiles, out_tiles):
    n_in, n_out = len(in_tiles), len(out_tiles)
    x_refs = refs[:n_in]
    gpre_ref, gpost_ref, w1_ref, w3_ref, w2_ref = refs[n_in:n_in + 5]
    o_refs = refs[n_in + 5:n_in + 5 + n_out]
    h_scr, acc_scr = refs[n_in + 5 + n_out:]
    i = pl.program_id(0)
    j = pl.program_id(1)

    def active(bounds, k):
        lo = sum(bounds[:k])
        return jnp.logical_and(i >= lo, i < lo + bounds[k])

    for k in range(n_in):
        @pl.when(jnp.logical_and(j == 0, active(in_tiles, k)))
        def _(k=k):
            _norm_rows_to(h_scr, x_refs[k], gpre_ref)
            acc_scr[...] = jnp.zeros_like(acc_scr)

    h = h_scr[...]
    a = jnp.dot(h, w1_ref[...], preferred_element_type=F32)
    b = jnp.dot(h, w3_ref[...], preferred_element_type=F32)
    g = (a * _sigmoid(a) * b).astype(BF16)
    acc_scr[...] += jnp.dot(g, w2_ref[...], preferred_element_type=F32)

    for ki in range(n_in):
        for ko in range(n_out):
            @pl.when(jnp.logical_and(j == nj - 1, jnp.logical_and(active(in_tiles, ki), active(out_tiles, ko))))
            def _(ki=ki, ko=ko):
                g_half = 0.5 * gpost_ref[...]

                def finish(rows):
                    o_refs[ko][rows, :] = x_refs[ki][rows, :] + _rms(acc_scr[rows, :], g_half)

                _for_row_blocks(acc_scr.shape[0], finish)


def _ffn(xs, g_pre, g_post, w1, w3, w2, layer, which, out_rows):
    d = xs[0].shape[1]
    dff = w1.shape[-1]
    tm, tf = ROW_TILE, FFN_FF_TILE
    nj = dff // tf
    vmem = ((len(xs) + len(out_rows)) * 2 * tm * d * 4 + 2 * 3 * d * tf * 2 + tm * d * 2 + tm * d * 4
            + 12 * MIB)
    in_tiles = tuple(x.shape[0] // tm for x in xs)
    out_tiles = tuple(r // tm for r in out_rows)

    def piece(bounds, k):
        lo = sum(bounds[:k])
        return lambda i, j: (jnp.clip(i - lo, 0, bounds[k] - 1), 0)

    const = lambda i, j: (0, 0)
    wcol = pl.BlockSpec((None, None, d, tf), lambda i, j: (layer, which, 0, j))
    outs = pl.pallas_call(
        functools.partial(_ffn_kernel, nj=nj, in_tiles=in_tiles, out_tiles=out_tiles),
        grid=(sum(in_tiles), nj),
        in_specs=[pl.BlockSpec((tm, d), piece(in_tiles, k)) for k in range(len(xs))]
        + [pl.BlockSpec((1, d), const), pl.BlockSpec((1, d), const), wcol, wcol,
           pl.BlockSpec((None, None, tf, d), lambda i, j: (layer, which, j, 0))],
        out_specs=[pl.BlockSpec((tm, d), piece(out_tiles, k)) for k in range(len(out_rows))],
        out_shape=[jax.ShapeDtypeStruct((r, d), F32) for r in out_rows],
        scratch_shapes=[pltpu.VMEM((tm, d), BF16), pltpu.VMEM((tm, d), F32)],
        compiler_params=_cparams(("parallel", "arbitrary"), vmem),
        name="ffn",
    )(*xs, g_pre.reshape(1, d), g_post.reshape(1, d), w1, w3, w2)
    return list(outs)


def _norm_mm_kernel(x_ref, g_ref, w_ref, gates_ref, e_ref, h_scr, *, ngate):
    j = pl.program_id(1)

    @pl.when(j == 0)
    def _():
        _norm_rows_to(h_scr, x_ref, g_ref)

    @pl.when(j < ngate)
    def _():
        gates_ref[...] = jnp.dot(h_scr[...], w_ref[...], preferred_element_type=F32).astype(gates_ref.dtype)

    @pl.when(j >= ngate)
    def _():
        e_ref[...] = jnp.dot(h_scr[...], w_ref[...], preferred_element_type=F32)


def _proj_elementwise(x, g, w_in, layer):
    n, d = x.shape
    tm, tn = PROJ_ROW_TILE, E_TILE
    vmem = 2 * (tm * d * 4 + d * tn * 2 + tm * tn * (2 + 4)) + tm * d * 2 + 4 * MIB
    lead = COL_QKV // tn
    skip = 3 * ATT_WIDTH // tn
    ncol = (w_in.shape[-1] - 3 * ATT_WIDTH) // tn
    ngate = ncol - lead
    return pl.pallas_call(
        functools.partial(_norm_mm_kernel, ngate=ngate),
        grid=(n // tm, ncol),
        in_specs=[pl.BlockSpec((tm, d), lambda i, j: (i, 0)), pl.BlockSpec((1, d), lambda i, j: (0, 0)),
                  pl.BlockSpec((None, d, tn),
                               lambda i, j: (layer, 0, jnp.where(j < ngate, j + lead + skip, j - ngate)))],
        out_specs=[pl.BlockSpec((tm, tn), lambda i, j: (i, jnp.minimum(j, ngate - 1))),
                   pl.BlockSpec((tm, tn), lambda i, j: (i, jnp.maximum(j - ngate, 0)))],
        out_shape=[jax.ShapeDtypeStruct((n, ngate * tn), BF16), jax.ShapeDtypeStruct((n, lead * tn), F32)],
        scratch_shapes=[pltpu.VMEM((tm, d), BF16)],
        compiler_params=_cparams(("parallel", "arbitrary"), vmem),
        name="mixer_in",
    )(x, g.reshape(1, d), w_in)


def _qkv_kernel(x_ref, g_ref, wq_ref, wk_ref, wv_ref, o0_ref, o1_ref, o2_ref, h_scr, res_scr, *, tm):
    step = pl.program_id(1)
    o_refs = (o0_ref, o1_ref, o2_ref)
    order = tuple(reversed(range(len(ATT_GROUPS))))
    ntile = 3 * ATT_OUT // LANES

    @pl.when(step == 0)
    def _():
        _norm_rows_to(h_scr, x_ref, g_ref)

    def project(gi):
        h = h_scr[...]
        cols = slice(gi * ATT_OUT, (gi + 1) * ATT_OUT)
        q = jnp.dot(h, wq_ref[:, cols], preferred_element_type=F32) * (HEAD_DIM ** -0.5)
        k = jnp.dot(h, wk_ref[:, cols], preferred_element_type=F32)
        v = jnp.dot(h, wv_ref[:, cols], preferred_element_type=F32)
        return jnp.concatenate([q, k, v], axis=-1)

    def write_residue_major(slot, gi):
        dil = ATT_GROUPS[gi][1]
        rows = tm // dil
        for r in range(dil):
            o_refs[gi][r] = jnp.concatenate(
                [res_scr[slot, c, pl.ds(r, rows, stride=dil), :] for c in range(ntile)], axis=-1).astype(BF16)

    for s, gi in enumerate(order):
        @pl.when(step == s)
        def _(s=s, gi=gi):
            res = project(gi)
            if s > 0 and ATT_GROUPS[order[s - 1]][1] > 1:
                write_residue_major((s - 1) % 2, order[s - 1])
            if ATT_GROUPS[gi][1] == 1:
                o_refs[gi][0] = res.astype(BF16)
            else:
                for c in range(ntile):
                    res_scr[s % 2, c] = res[:, c * LANES:(c + 1) * LANES]


def _proj_qkv(x, g, w_in, layer):
    n, d = x.shape
    tm = ROW_TILE
    ng = len(ATT_GROUPS)
    assert ATT_GROUPS[0][1] == 1
    first = COL_QKV // ATT_WIDTH

    def wspec(which):
        return _resident((None, d, ATT_WIDTH), lambda i, s: (layer, 0, first + which))

    return pl.pallas_call(
        functools.partial(_qkv_kernel, tm=tm),
        grid=(n // tm, ng),
        in_specs=[pl.BlockSpec((tm, d), lambda i, gq: (i, 0)), pl.BlockSpec((1, d), lambda i, gq: (0, 0)),
                  wspec(0), wspec(1), wspec(2)],
        out_specs=[pl.BlockSpec((dil, tm // dil, 3 * ATT_OUT), lambda i, gq: (0, i, 0)) for _, dil in ATT_GROUPS],
        out_shape=[jax.ShapeDtypeStruct((dil, n // dil, 3 * ATT_OUT), BF16) for _, dil in ATT_GROUPS],
        scratch_shapes=[pltpu.VMEM((tm, d), BF16), pltpu.VMEM((2, 3 * ATT_OUT // LANES, tm, LANES), F32)],
        compiler_params=_cparams(("parallel", "arbitrary"),
                                 2 * tm * d * 4 + tm * d * 2 + 3 * d * ATT_WIDTH * 2 + 2 * ng * tm * 3 * ATT_OUT * 2
                                 + 2 * tm * 3 * ATT_OUT * 4 + 8 * MIB),
        name="mixer_qkv",
    )(x, g.reshape(1, d), w_in, w_in, w_in)


def _lru_kernel(*refs, tc, nt, seg_first, seg_last, d):
    xm_ref, xp_ref, xn_ref, cw_ref, cb_ref, wg_ref, bg_ref, lam_ref = refs[:8]
    if d == 0:
        o_ref, perm_scr, a_scr, b_scr, h_scr = refs[8:]
    else:
        hf_ref, gl_ref, o_ref, perm_scr, a_scr, b_scr, h_scr = refs[8:]
    i = pl.program_id(0)
    ti = i if d == 0 else nt - 1 - i
    w = LRU_WIDTH
    at_first = _any_eq(ti, seg_first)
    at_last = _any_eq(ti, seg_last)

    sub = tc // SUBLANES
    pitch = sub + SUBLANES
    ntile = w // LANES
    for c in range(ntile):
        for jj in range(SUBLANES):
            perm_scr[c, pl.ds(jj * pitch, sub), :] = xm_ref[pl.ds(jj * sub, sub), c * LANES:(c + 1) * LANES]

    def block(k):
        return jnp.concatenate([perm_scr[c, pl.ds(k, SUBLANES, stride=pitch), :] for c in range(ntile)], axis=-1)

    sub_id = lax.broadcasted_iota(jnp.int32, (SUBLANES, w), 0)
    x_blk = [block(k) for k in range(sub)]
    before = xp_ref[...] * jnp.where(at_first, 0.0, 1.0)
    after = xn_ref[...] * jnp.where(at_last, 0.0, 1.0)

    def from_previous_subchunk(blk, edge_row):
        return jnp.where(sub_id == 0, edge_row, pltpu.roll(blk, 1, axis=0))

    def from_next_subchunk(blk, edge_row):
        return jnp.where(sub_id == SUBLANES - 1, edge_row, pltpu.roll(blk, SUBLANES - 1, axis=0))

    def shifted(k, off):
        kk = k + off
        if 0 <= kk < sub:
            return x_blk[kk]
        if kk < 0:
            return from_previous_subchunk(x_blk[kk + sub], before[SUBLANES + kk:SUBLANES + kk + 1, :])
        return from_next_subchunk(x_blk[kk - sub], after[kk - sub:kk - sub + 1, :])

    left = LRU_CONV_W // 2
    taps = [cw_ref[pl.ds(kk, 1), :] for kk in range(LRU_CONV_W)]
    xc = jnp.concatenate(
        [cb_ref[...] + sum(shifted(k, kk - left) * taps[kk] for kk in range(LRU_CONV_W)) for k in range(sub)], axis=0)

    xcb = xc.astype(BF16)
    parts = [jnp.dot(xcb[:, p * LANES:(p + 1) * LANES], wg_ref[p], preferred_element_type=F32)
             for p in range(w // LANES)]
    r = _sigmoid(jnp.concatenate([g[:, :LANES] for g in parts], axis=-1) + bg_ref[:, :w])
    ig = _sigmoid(jnp.concatenate([g[:, LANES:] for g in parts], axis=-1) + bg_ref[:, w:])
    nlam = -lam_ref[...]
    softplus = jnp.maximum(nlam, 0.0) + jnp.log(1.0 + jnp.exp(-jnp.abs(nlam)))
    a = jnp.exp(-LRU_C * r * softplus)
    a_scr[...] = a
    b_scr[...] = jnp.sqrt(1.0 - a * a) * (ig * xc)

    @pl.when(at_first if d == 0 else at_last)
    def _():
        h_scr[...] = jnp.zeros_like(h_scr)

    hl = jnp.zeros((SUBLANES, w), F32)
    pp = jnp.ones((SUBLANES, w), F32)
    for k in (range(sub) if d == 0 else range(sub - 1, -1, -1)):
        rows = pl.ds(k * SUBLANES, SUBLANES)
        av = a_scr[rows, :]
        hl = av * hl + b_scr[rows, :]
        pp = av * pp
        a_scr[rows, :] = pp
        b_scr[rows, :] = hl
    cur = h_scr[pl.ds(0, 1), :]
    carry = jnp.zeros((SUBLANES, w), F32)
    for jj in (range(SUBLANES) if d == 0 else range(SUBLANES - 1, -1, -1)):
        carry = jnp.where(sub_id == jj, cur, carry)
        cur = hl[jj:jj + 1, :] + pp[jj:jj + 1, :] * cur
    h_scr[pl.ds(0, 1), :] = cur
    for k in range(sub):
        rows = pl.ds(k * SUBLANES, SUBLANES)
        hv = b_scr[rows, :] + a_scr[rows, :] * carry
        for c in range(ntile):
            perm_scr[c, pl.ds(k, SUBLANES, stride=pitch), :] = hv[:, c * LANES:(c + 1) * LANES]
    for jj in range(SUBLANES):
        rows = pl.ds(jj * sub, sub)
        hv = jnp.concatenate([perm_scr[c, pl.ds(jj * pitch, sub), :] for c in range(ntile)], axis=-1)
        if d == 0:
            o_ref[rows, :] = hv
        else:
            o_ref[rows, :] = ((hf_ref[rows, :] + hv) * _gelu_tanh(gl_ref[rows, :])).astype(o_ref.dtype)


def _lru(ze, conv_w, conv_b, wg, bg, lam, seg_lens):
    n = ze.shape[0]
    w = LRU_WIDTH
    tc = LRU_TILE
    nt = n // tc
    starts = np.cumsum((0,) + tuple(seg_lens))
    seg_first = tuple(int(s) // tc for s in starts[:-1])
    seg_last = tuple(int(s) // tc - 1 for s in starts[1:])
    hb = tc // SUBLANES
    nhb = n // SUBLANES
    pitched = SUBLANES * (tc // SUBLANES + SUBLANES)

    def one_direction(d, extra_specs, extra_args, out_dtype):
        tile = (lambda i: i) if d == 0 else (lambda i: nt - 1 - i)
        return pl.pallas_call(
            functools.partial(_lru_kernel, tc=tc, nt=nt, seg_first=seg_first, seg_last=seg_last, d=d),
            grid=(nt,),
            in_specs=[pl.BlockSpec((tc, w), lambda i: (tile(i), E_XL)),
                      pl.BlockSpec((SUBLANES, w), lambda i: (jnp.maximum(tile(i) * hb - 1, 0), E_XL)),
                      pl.BlockSpec((SUBLANES, w), lambda i: (jnp.minimum((tile(i) + 1) * hb, nhb - 1), E_XL)),
                      pl.BlockSpec((LRU_CONV_W, w), lambda i: (0, 0)),
                      pl.BlockSpec((1, w), lambda i: (0, 0)),
                      _resident((None, w // LANES, LANES, 2 * LANES), lambda i: (d, 0, 0, 0)),
                      pl.BlockSpec((None, 1, 2 * w), lambda i: (d, 0, 0)),
                      pl.BlockSpec((None, 1, w), lambda i: (d, 0, 0))] + [spec(tile) for spec in extra_specs],
            out_specs=pl.BlockSpec((tc, w), lambda i: (tile(i), 0)),
            out_shape=jax.ShapeDtypeStruct((n, w), out_dtype),
            scratch_shapes=[pltpu.VMEM((w // LANES, pitched, LANES), F32), pltpu.VMEM((tc, w), F32),
                            pltpu.VMEM((tc, w), F32), pltpu.VMEM((SUBLANES, w), F32)],
            compiler_params=_cparams(("arbitrary",)),
            name="rglru_fwd" if d == 0 else "rglru_bwd",
        )(ze, ze, ze, conv_w, conv_b.reshape(1, w), wg, bg, lam, *extra_args)

    hf = one_direction(0, [], [], F32)
    return one_direction(1, [lambda tile: pl.BlockSpec((tc, w), lambda i: (tile(i), 0)),
                             lambda tile: pl.BlockSpec((tc, w), lambda i: (tile(i), E_GL))], [hf, ze], BF16)


def _lru_gate_weights(wa, ba, wx, bx):
    per_tile = LANES * LRU_BLOCKS // LRU_WIDTH

    def tiles(wb):
        bw = wb.shape[-1]
        eye = jnp.eye(per_tile, dtype=wb.dtype)
        grouped = wb.reshape(2, LRU_BLOCKS // per_tile, per_tile, bw, bw)
        full = grouped[:, :, :, :, None, :] * eye[None, None, :, None, :, None]
        return full.reshape(2, LRU_BLOCKS // per_tile, LANES, LANES)

    wg = jnp.concatenate([tiles(wa), tiles(wx)], axis=-1).astype(BF16)
    bg = jnp.concatenate([ba, bx], axis=-1).reshape(2, 1, 2 * LRU_WIDTH)
    return wg, bg


S5_GROUP_BATCH = 8


def _s5_ktable_kernel(b_ref, w_ref, o_ref):
    for i in range(S5_GROUP_BATCH):
        o_ref[i] = jnp.dot(b_ref[i], w_ref[i], preferred_element_type=F32, precision=lax.Precision.HIGHEST)


def _s5_tables(lam_re, lam_im, log_dt, b_re, b_im, c_re, c_im):
    L, G, P, C = S5_CHUNK, S5_GROUPS, S5_STATE, S5_GROUP_CH
    dt = jnp.exp(log_dt)[..., None]
    mag = jnp.exp(lam_re * dt)
    ar = mag * jnp.cos(lam_im * dt)
    ai = mag * jnp.sin(lam_im * dt)
    den = lam_re * lam_re + lam_im * lam_im
    cr = ((ar - 1.0) * lam_re + ai * lam_im) / den
    ci = (ai * lam_re - (ar - 1.0) * lam_im) / den
    bbr = cr[..., None] * b_re - ci[..., None] * b_im
    bbi = cr[..., None] * b_im + ci[..., None] * b_re
    pr, pi = jnp.ones_like(ar)[None], jnp.zeros_like(ai)[None]
    nr, ni = ar, ai
    while pr.shape[0] < L + 1:
        pr, pi = (jnp.concatenate([pr, pr * nr - pi * ni], axis=0),
                  jnp.concatenate([pi, pr * ni + pi * nr], axis=0))
        nr, ni = nr * nr - ni * ni, 2.0 * nr * ni
    pr, pi = pr[:L + 1], pi[:L + 1]
    zr = pr[:L, ..., None] * bbr - pi[:L, ..., None] * bbi
    zi = pr[:L, ..., None] * bbi + pi[:L, ..., None] * bbr

    def c_pow(powers_r, powers_i):
        ctr = jnp.transpose(c_re, (0, 1, 3, 2))[:, :, :, None, :]
        cti = jnp.transpose(c_im, (0, 1, 3, 2))[:, :, :, None, :]
        qr = jnp.transpose(powers_r, (1, 2, 3, 0))[..., None]
        qi = jnp.transpose(powers_i, (1, 2, 3, 0))[..., None]
        return ctr * qr - cti * qi, ctr * qi + cti * qr

    wr, wi = c_pow(pr[:L], pi[:L])
    wmat = jnp.concatenate([wr, wi], axis=2).reshape(2 * G, 2 * P, L * C)
    bmat = jnp.concatenate([jnp.transpose(bbr, (0, 1, 3, 2)), -jnp.transpose(bbi, (0, 1, 3, 2))],
                           axis=-1).reshape(2 * G, C, 2 * P)

    def w_in(z, flip):
        zf = z[::-1] if flip else z
        return jnp.transpose(zf, (1, 0, 3, 2)).reshape(G, L * C, P)

    win = jnp.concatenate([w_in(zr[:, 0], True), w_in(zr[:, 1], False),
                           w_in(zi[:, 0], True), w_in(zi[:, 1], False)], axis=-1).astype(BF16)

    fr, fi = c_pow(pr[1:], pi[1:])
    br, bi = c_pow(pr[:0:-1], pi[:0:-1])
    wout = jnp.concatenate([fr[0], br[1], -fi[0], -bi[1]], axis=1).reshape(G, 4 * P, L * C).astype(BF16)
    al = jnp.concatenate([pr[L, 0], pr[L, 1], pi[L, 0], pi[L, 1]], axis=-1)
    return bmat, wmat, win, wout, al


def _s5_lag_table(kt):
    L, G, C = S5_CHUNK, S5_GROUPS, S5_GROUP_CH
    kk = jnp.concatenate([kt[1, :, :, :0:-1], kt[0, :, :, :1] + kt[1, :, :, :1], kt[0, :, :, 1:]], axis=2)
    kk = kk.reshape(G, C, (2 * L - 1) * C)
    return jnp.pad(kk, ((0, 0), (0, 0), (0, 2 * L * C - kk.shape[-1])))


def _s5_prepare(lam_re, lam_im, log_dt, b_re, b_im, c_re, c_im):
    L, G, P, C = S5_CHUNK, S5_GROUPS, S5_STATE, S5_GROUP_CH
    bmat, wmat, win, wout, al = _s5_tables(lam_re, lam_im, log_dt, b_re, b_im, c_re, c_im)
    kt = pl.pallas_call(
        _s5_ktable_kernel,
        grid=(2 * G // S5_GROUP_BATCH,),
        in_specs=[pl.BlockSpec((S5_GROUP_BATCH, C, 2 * P), lambda g: (g, 0, 0)),
                  pl.BlockSpec((S5_GROUP_BATCH, 2 * P, L * C), lambda g: (g, 0, 0))],
        out_specs=pl.BlockSpec((S5_GROUP_BATCH, C, L * C), lambda g: (g, 0, 0)),
        out_shape=jax.ShapeDtypeStruct((2 * G, C, L * C), F32),
        compiler_params=_cparams(("parallel",)),
        name="s5_ktable",
    )(bmat, wmat)
    return _s5_lag_table(kt.reshape(2, G, C, L, C)), win, wout, al


def _s5_state_kernel(v_ref, win_ref, o_ref):
    width = win_ref.shape[-1]
    for i in range(S5_GROUP_BATCH):
        o_ref[:, i * width:(i + 1) * width] = jnp.dot(v_ref[i], win_ref[i], preferred_element_type=F32)


def _s5_scan_kernel(s_ref, al_ref, o_ref, *, seg_chunks):
    p2 = 2 * S5_STATE
    alr = al_ref[:, :p2]
    ali = al_ref[:, p2:]
    is_fwd = lax.broadcasted_iota(jnp.int32, alr.shape, 1) < S5_STATE
    zero = jnp.zeros_like(alr)
    start = 0
    for n_chunks in seg_chunks:
        def fwd(k, carry, start=start):
            xr, xi = carry
            c = start + k
            o_ref[c, :, :p2] = xr
            o_ref[c, :, p2:] = xi
            sr = s_ref[c, :, :p2]
            si = s_ref[c, :, p2:]
            return alr * xr - ali * xi + sr, alr * xi + ali * xr + si

        lax.fori_loop(0, n_chunks, fwd, (zero, zero))

        def bwd(k, carry, start=start, n_chunks=n_chunks):
            xr, xi = carry
            c = start + n_chunks - 1 - k
            o_ref[c, :, :p2] = jnp.where(is_fwd, o_ref[c, :, :p2], xr)
            o_ref[c, :, p2:] = jnp.where(is_fwd, o_ref[c, :, p2:], xi)
            sr = s_ref[c, :, :p2]
            si = s_ref[c, :, p2:]
            return alr * xr - ali * xi + sr, alr * xi + ali * xr + si

        lax.fori_loop(0, n_chunks, bwd, (zero, zero))
        start += n_chunks


def _s5_out_kernel(v_ref, kk_ref, x_ref, wout_ref, o_ref, mt_scr):
    L, C = S5_CHUNK, S5_GROUP_CH
    kk = kk_ref[...]
    width = kk.shape[-1]
    per_tile = LANES // C
    for rot in range(per_tile):
        shifted = kk if rot == 0 else pltpu.roll(kk, width - rot * C, axis=1)
        shifted = shifted.astype(BF16)
        for s in range(L):
            lag0 = L - 1 - s
            if lag0 % per_tile == rot:
                col = (lag0 // per_tile) * LANES
                mt_scr[pl.ds(s * C, C), :] = shifted[:, col:col + L * C]
    o_ref[...] = (jnp.dot(v_ref[...], mt_scr[...], preferred_element_type=F32)
                  + jnp.dot(x_ref[...].astype(BF16), wout_ref[...], preferred_element_type=F32)).astype(o_ref.dtype)


def _s5(u, tables, seg_lens):
    kk, win, wout, al = tables
    L, G, P, C = S5_CHUNK, S5_GROUPS, S5_STATE, S5_GROUP_CH
    n = u.shape[0]
    nc = n // L
    lc = L * C
    v = jnp.transpose(u.reshape(nc, L, G, C), (2, 0, 1, 3)).reshape(G, nc, lc).astype(BF16)
    states = pl.pallas_call(
        _s5_state_kernel,
        grid=(G // S5_GROUP_BATCH,),
        in_specs=[pl.BlockSpec((S5_GROUP_BATCH, nc, lc), lambda g: (g, 0, 0)),
                  pl.BlockSpec((S5_GROUP_BATCH, lc, 4 * P), lambda g: (g, 0, 0))],
        out_specs=pl.BlockSpec((nc, S5_GROUP_BATCH * 4 * P), lambda g: (0, g)),
        out_shape=jax.ShapeDtypeStruct((nc, G * 4 * P), F32),
        compiler_params=_cparams(("parallel",)),
        name="s5_chunk_state",
    )(v, win)
    gb = SUBLANES
    carried = pl.pallas_call(
        functools.partial(_s5_scan_kernel, seg_chunks=tuple(t // L for t in seg_lens)),
        grid=(G // gb,),
        in_specs=[pl.BlockSpec((nc, gb, 4 * P), lambda g: (0, g, 0)),
                  pl.BlockSpec((gb, 4 * P), lambda g: (g, 0))],
        out_specs=pl.BlockSpec((nc, gb, 4 * P), lambda g: (0, g, 0)),
        out_shape=jax.ShapeDtypeStruct((nc, G, 4 * P), F32),
        compiler_params=_cparams(("parallel",)),
        name="s5_chunk_scan",
    )(states.reshape(nc, G, 4 * P), al)
    y = pl.pallas_call(
        _s5_out_kernel,
        grid=(G,),
        in_specs=[pl.BlockSpec((None, nc, lc), lambda g: (g, 0, 0)),
                  pl.BlockSpec((None, C, 2 * lc), lambda g: (g, 0, 0)),
                  pl.BlockSpec((nc, 4 * P), lambda g: (0, g)),
                  pl.BlockSpec((None, 4 * P, lc), lambda g: (g, 0, 0))],
        out_specs=pl.BlockSpec((None, nc, lc), lambda g: (g, 0, 0)),
        out_shape=jax.ShapeDtypeStruct((G, nc, lc), BF16),
        scratch_shapes=[pltpu.VMEM((lc, lc), BF16)],
        compiler_params=_cparams(("parallel",)),
        name="s5_chunk_out",
    )(v, kk, carried.reshape(nc, G * 4 * P), wout)
    return jnp.transpose(y.reshape(G, nc, L, C), (1, 2, 0, 3)).reshape(n, G * C)


def _s5_post_kernel(y_ref, u_ref, d_ref, w_ref, b_ref, o_ref):
    y1 = _gelu_tanh(y_ref[...].astype(F32) + d_ref[...] * u_ref[...])
    gate = jnp.dot(y1.astype(BF16), w_ref[...], preferred_element_type=F32) + b_ref[...]
    o_ref[...] = (y1 * _sigmoid(gate)).astype(o_ref.dtype)


def _s5_post(y5, ze, s5_d, glu_w, glu_b, layer):
    n, w = y5.shape
    tm = ROW_TILE
    return pl.pallas_call(
        _s5_post_kernel,
        grid=(n // tm,),
        in_specs=[pl.BlockSpec((tm, w), lambda i: (i, 0)), pl.BlockSpec((tm, w), lambda i: (i, E_U)),
                  pl.BlockSpec((1, w), lambda i: (0, 0)), pl.BlockSpec((None, w, w), lambda i: (layer, 0, 0)),
                  pl.BlockSpec((1, w), lambda i: (0, 0))],
        out_specs=pl.BlockSpec((tm, w), lambda i: (i, 0)),
        out_shape=jax.ShapeDtypeStruct((n, w), BF16),
        compiler_params=_cparams(("parallel",)),
        name="s5_post",
    )(y5, ze, s5_d.reshape(1, w), glu_w, glu_b.reshape(1, w))


def _t5_buckets(rel):
    half = REL_BUCKETS // 2
    max_exact = half // 2
    sign = (rel > 0).astype(np.int32) * half
    n = np.abs(rel)
    large = max_exact + (np.log(np.maximum(n, 1) / max_exact)
                         / np.log(REL_MAX_DIST / max_exact) * (half - max_exact)).astype(np.int32)
    large = np.minimum(large, half - 1)
    return sign + np.where(n < max_exact, n, large)


def _att_bias_tile(rel_bias, group, dil):
    ncol = Q_BLOCK + 2 * HALF_WIN
    hs = slice(group * HEADS_PER_GROUP, (group + 1) * HEADS_PER_GROUP)
    offs = np.arange(-HALF_WIN, HALF_WIN + 1)
    vals = jnp.transpose(rel_bias[:, hs][_t5_buckets(offs * dil)]).astype(F32)
    width = 2 * ncol
    pad_lo = Q_BLOCK
    line = jnp.pad(vals, ((0, 0), (pad_lo, width - pad_lo - vals.shape[1])), constant_values=NEG_INF)
    rows = jnp.broadcast_to(line[:, None, :], (HEADS_PER_GROUP, Q_BLOCK, width)).reshape(HEADS_PER_GROUP, -1)
    skew = rows[:, :Q_BLOCK * (width - 1)].reshape(HEADS_PER_GROUP, Q_BLOCK, width - 1)
    tile = skew[:, :, pad_lo:pad_lo + ncol]
    col = np.arange(ncol)[None, None, :]
    before = col < HALF_WIN
    after = col >= HALF_WIN + Q_BLOCK
    return jnp.stack([tile, jnp.where(before, NEG_INF, tile), jnp.where(after, NEG_INF, tile),
                      jnp.where(before | after, NEG_INF, tile)])


def _att_kernel(q_ref, kp_ref, km_ref, kn_ref, vp_ref, vm_ref, vn_ref, *rest):
    bias_refs = rest[:ATT_STEP_BLOCKS]
    o_ref, lse_ref = rest[ATT_STEP_BLOCKS:]
    ncol = Q_BLOCK + 2 * HALF_WIN
    main_rows = ATT_STEP_BLOCKS * Q_BLOCK
    low = lax.broadcasted_iota(jnp.int32, (Q_BLOCK, LANES), 1) < HEAD_DIM
    ones = jnp.ones((ncol, LANES), BF16)
    zero = jnp.zeros((Q_BLOCK, LANES), BF16)
    heads = range(HEADS_PER_GROUP)
    pair_cols = [slice((h // 2) * LANES, (h // 2 + 1) * LANES) for h in heads]

    def window(prev_ref, main_ref, next_ref, sb):
        lo = sb * Q_BLOCK - HALF_WIN
        parts = []
        if lo < 0:
            parts.append(prev_ref[...])
        m_lo, m_hi = max(lo, 0), min(lo + ncol, main_rows)
        parts.append(main_ref[pl.ds(m_lo, m_hi - m_lo), :])
        if lo + ncol > main_rows:
            parts.append(next_ref[...])
        return jnp.concatenate(parts, axis=0) if len(parts) > 1 else parts[0]

    for sb in range(ATT_STEP_BLOCKS):
        rows = pl.ds(sb * Q_BLOCK, Q_BLOCK)
        q = q_ref[rows, :]
        k = window(kp_ref, km_ref, kn_ref, sb)
        v = window(vp_ref, vm_ref, vn_ref, sb)
        scores = []
        for h in heads:
            q2 = q[:, pair_cols[h]]
            qh = jnp.where(low, q2, zero) if h % 2 == 0 else jnp.where(low, zero, q2)
            s = lax.dot_general(qh, k[:, pair_cols[h]], (((1,), (1,)), ((), ())), preferred_element_type=F32)
            scores.append(s + bias_refs[sb][h])
        maxes = [jnp.max(s, axis=-1, keepdims=True) for s in scores]
        probs = [jnp.exp(s - m).astype(BF16) for s, m in zip(scores, maxes)]
        outs = [jnp.dot(p, v[:, pair_cols[h]], preferred_element_type=F32) for h, p in zip(heads, probs)]
        sums = [jnp.dot(p, ones, preferred_element_type=F32) for p in probs]
        for pair in range(HEADS_PER_GROUP // 2):
            a, b = 2 * pair, 2 * pair + 1
            l = jnp.where(low, sums[a], sums[b])
            o_ref[rows, pair_cols[a]] = (jnp.where(low, outs[a], outs[b]) / l).astype(o_ref.dtype)
            lse_ref[rows, pair_cols[a]] = jnp.where(low, maxes[a], maxes[b]) + jnp.log(l)


def _attention_group(qkv, bias_tile, dil, seg_lens):
    nd = qkv.shape[1]
    step_rows = ATT_STEP_BLOCKS * Q_BLOCK
    halves_per_step = step_rows // HALF_WIN
    assert nd % step_rows == 0
    nhalf = nd // HALF_WIN
    starts = np.cumsum((0,) + tuple(seg_lens)) // (dil * Q_BLOCK)
    blk_first = tuple(int(s) for s in starts[:-1])
    blk_last = tuple(int(s) - 1 for s in starts[1:])
    ncol = Q_BLOCK + 2 * HALF_WIN

    def variant(blk):
        return _any_eq(blk, blk_first).astype(jnp.int32) + 2 * _any_eq(blk, blk_last).astype(jnp.int32)

    def main(cblk):
        return pl.BlockSpec((None, step_rows, ATT_OUT), lambda r, b: (r, b, cblk))

    def prev(cblk):
        return pl.BlockSpec((None, HALF_WIN, ATT_OUT),
                            lambda r, b: (r, jnp.maximum(halves_per_step * b - 1, 0), cblk))

    def nxt(cblk):
        return pl.BlockSpec((None, HALF_WIN, ATT_OUT),
                            lambda r, b: (r, jnp.minimum(halves_per_step * (b + 1), nhalf - 1), cblk))

    bias_specs = [pl.BlockSpec((None, HEADS_PER_GROUP, Q_BLOCK, ncol),
                               lambda r, b, sb=sb: (variant(ATT_STEP_BLOCKS * b + sb), 0, 0, 0))
                  for sb in range(ATT_STEP_BLOCKS)]
    out_spec = pl.BlockSpec((None, step_rows, ATT_OUT), lambda r, b: (r, b, 0))
    return pl.pallas_call(
        _att_kernel,
        grid=(dil, nd // step_rows),
        in_specs=[main(0), prev(1), main(1), nxt(1), prev(2), main(2), nxt(2)] + bias_specs,
        out_specs=[out_spec, out_spec],
        out_shape=[jax.ShapeDtypeStruct((dil, nd, ATT_OUT), BF16), jax.ShapeDtypeStruct((dil, nd, ATT_OUT), F32)],
        compiler_params=_cparams(("parallel", "parallel")),
        name=f"attention_d{dil}",
    )(qkv, qkv, qkv, qkv, qkv, qkv, qkv, *([bias_tile] * ATT_STEP_BLOCKS))


def _merge_kernel(yl_ref, ys_ref, o0_ref, l0_ref, o1_ref, l1_ref, o2_ref, l2_ref,
                  ga_ref, gb_ref, gc_ref, wl_ref, ws_ref, wa_ref, m_ref, o1_scr, l1_scr, o2_scr, l2_scr, *, tm):
    ntile = ATT_OUT // LANES

    def sequence_order(src, dst, dil):
        rows = tm // dil
        for r in range(dil):
            blk = src[r].astype(F32)
            for c in range(ntile):
                dst[c, pl.ds(r, rows, stride=dil), :] = blk[:, c * LANES:(c + 1) * LANES]
        return jnp.concatenate([dst[c] for c in range(ntile)], axis=-1)

    o1 = sequence_order(o1_ref, o1_scr, ATT_GROUPS[1][1])
    l1 = sequence_order(l1_ref, l1_scr, ATT_GROUPS[1][1])
    o2 = sequence_order(o2_ref, o2_scr, ATT_GROUPS[2][1])
    l2 = sequence_order(l2_ref, l2_scr, ATT_GROUPS[2][1])
    l0 = l0_ref[...]
    mx = jnp.maximum(jnp.maximum(l0, l1), l2)
    e0, e1, e2 = jnp.exp(l0 - mx), jnp.exp(l1 - mx), jnp.exp(l2 - mx)
    yatt = ((o0_ref[...].astype(F32) * e0 + o1 * e1 + o2 * e2) / (e0 + e1 + e2)).astype(BF16)
    yl, ys = yl_ref[...], ys_ref[...]
    chunk = 4 * LANES
    for c0 in range(0, m_ref.shape[-1], chunk):
        cs = slice(c0, c0 + chunk)

        def gated(g_ref, y, w_ref):
            return _sigmoid(g_ref[:, cs].astype(F32)) * jnp.dot(y, w_ref[:, cs], preferred_element_type=F32)

        m_ref[:, cs] = (gated(ga_ref, yl, wl_ref) + gated(gb_ref, ys, ws_ref)
                        + gated(gc_ref, yatt, wa_ref)).astype(m_ref.dtype)


def _merge(ylru, gates, ys5, att, w_br_lru, w_br_s5, w_br_att, layer):
    n = gates.shape[0]
    d = D_MODEL
    tm = MERGE_ROW_TILE
    wl, wa = LRU_WIDTH, ATT_OUT
    gspec = [pl.BlockSpec((tm, d), lambda i, c=c: (i, c)) for c in range(3)]
    att_specs, att_args = [], []
    for (_, dil), (o, l) in zip(ATT_GROUPS, att):
        blk = (None, tm, wa) if dil == 1 else (dil, tm // dil, wa)
        att_specs += [pl.BlockSpec(blk, lambda i: (0, i, 0))] * 2
        att_args += [o, l]
    return pl.pallas_call(
        functools.partial(_merge_kernel, tm=tm),
        grid=(n // tm,),
        in_specs=[pl.BlockSpec((tm, wl), lambda i: (i, 0)), pl.BlockSpec((tm, S5_WIDTH), lambda i: (i, 0))]
        + att_specs + gspec
        + [_resident((None, wl, d), lambda i: (layer, 0, 0)), _resident((None, S5_WIDTH, d), lambda i: (layer, 0, 0)),
           _resident((None, wa, d), lambda i: (layer, 0, 0))],
        out_specs=pl.BlockSpec((tm, d), lambda i: (i, 0)),
        out_shape=jax.ShapeDtypeStruct((n, d), BF16),
        scratch_shapes=[pltpu.VMEM((wa // LANES, tm, LANES), F32)] * 4,
        compiler_params=_cparams(("parallel",)),
        name="merge",
    )(ylru, ys5, *att_args, gates, gates, gates, w_br_lru, w_br_s5, w_br_att)


def _out_proj_kernel(x_ref, m_ref, w_ref, g_ref, o_ref):
    mix = jnp.dot(m_ref[...], w_ref[...], preferred_element_type=F32)
    o_ref[...] = x_ref[...] + _rms(mix, g_ref[...])


def _out_proj(x, m, w_out, g, layer):
    n, d = x.shape
    tm = ROW_TILE
    return pl.pallas_call(
        _out_proj_kernel,
        grid=(n // tm,),
        in_specs=[pl.BlockSpec((tm, d), lambda i: (i, 0)), pl.BlockSpec((tm, d), lambda i: (i, 0)),
                  _resident((None, d, d), lambda i: (layer, 0, 0)), pl.BlockSpec((1, d), lambda i: (0, 0))],
        out_specs=pl.BlockSpec((tm, d), lambda i: (i, 0)),
        out_shape=jax.ShapeDtypeStruct((n, d), F32),
        compiler_params=_cparams(("parallel",)),
        name="mixer_out",
    )(x, m, w_out, g.reshape(1, d))


def _mixer(x, g_pre, g_post, seg_lens, layer, w_in, conv_w, conv_b, lru_wa, lru_ba, lru_wx, lru_bx, lru_L,
           s5_params, s5_d, glu_w, glu_b, w_br_lru, w_br_s5, w_br_att, w_out, bias_tiles):
    gates, ze = _proj_elementwise(x, g_pre, w_in, layer)
    qkv = _proj_qkv(x, g_pre, w_in, layer)

    wg, bg = _lru_gate_weights(lru_wa, lru_ba, lru_wx, lru_bx)
    ylru = _lru(ze, conv_w, conv_b, wg, bg, lru_L.reshape(2, 1, LRU_WIDTH), seg_lens)

    y5 = _s5(ze[:, E_U * S5_WIDTH:(E_U + 1) * S5_WIDTH], _s5_prepare(*s5_params), seg_lens)
    ys5 = _s5_post(y5, ze, s5_d, glu_w, glu_b, layer)

    att = [_attention_group(qkv[g], bias_tiles[g], dil, seg_lens) for g, (_, dil) in enumerate(ATT_GROUPS)]

    m = _merge(ylru, gates, ys5, att, w_br_lru, w_br_s5, w_br_att, layer)
    return _out_proj(x, m, w_out, g_post, layer)


def kernel(x_prompt, x_sample, norm_g, w_in, lru_conv_w, lru_conv_b, lru_wa, lru_ba, lru_wx, lru_bx, lru_L,
           s5_lam_re, s5_lam_im, s5_log_dt, s5_b_re, s5_b_im, s5_c_re, s5_c_im, s5_d, s5_glu_w, s5_glu_b,
           rel_bias, w_br_lru, w_br_s5, w_br_att, w_out, ffn_w1, ffn_w3, ffn_w2):
    bp, tp, d = x_prompt.shape
    bs, ts, _ = x_sample.shape
    seg_lens = (tp,) * bp + (ts,) * bs
    rows = (bp * tp, bs * ts)
    n = sum(rows)
    depth = norm_g.shape[0]
    w1, w3, w2 = ffn_w1.astype(BF16), ffn_w3.astype(BF16), ffn_w2.astype(BF16)
    w_in_b, glu_b16, w_out_b = w_in.astype(BF16), s5_glu_w.astype(BF16), w_out.astype(BF16)
    wbl, wbs, wba = w_br_lru.astype(BF16), w_br_s5.astype(BF16), w_br_att.astype(BF16)
    bias_tiles = [_att_bias_tile(rel_bias, g, dil) for g, (_, dil) in enumerate(ATT_GROUPS)]
    s5_stack = (s5_lam_re, s5_lam_im, s5_log_dt, s5_b_re, s5_b_im, s5_c_re, s5_c_im)

    xs = [x_prompt.reshape(rows[0], d), x_sample.reshape(rows[1], d)]
    for l in range(depth):
        g = norm_g[l]
        (x,) = _ffn(xs, g[0], g[1], w1, w3, w2, l, 0, (n,))
        x = _mixer(x, g[2], g[3], seg_lens, l, w_in_b, lru_conv_w[l], lru_conv_b[l], lru_wa[l], lru_ba[l],
                   lru_wx[l], lru_bx[l], lru_L[l], [p[l] for p in s5_stack], s5_d[l], glu_b16, s5_glu_b[l],
                   wbl, wbs, wba, w_out_b,
                   bias_tiles)
        xs = _ffn([x], g[4], g[5], w1, w3, w2, l, 1, rows if l == depth - 1 else (n,))
    return (xs[0].reshape(bp, tp, d), xs[1].reshape(bs, ts, d))
```
